```python
import math
import jax, jax.numpy as jnp
from jax import lax
import numpy as np

D_MODEL = 1024
BATCH = 2
SEQ = 8192
DEPTH = 4
DEC_BATCH = 128
DEC_SEQ = 1
PAST_LEN = 8192
PAGE_SIZE = 128

N_MIXERS = 4
MIX_MLSTM, MIX_DIL, MIX_HGRN, MIX_SWA = 0, 1, 2, 3
N_MLSTM = (DEPTH - MIX_MLSTM + N_MIXERS - 1) // N_MIXERS
N_DIL = (DEPTH - MIX_DIL + N_MIXERS - 1) // N_MIXERS
N_HGRN = (DEPTH - MIX_HGRN + N_MIXERS - 1) // N_MIXERS
N_SWA = (DEPTH - MIX_SWA + N_MIXERS - 1) // N_MIXERS

D_FF = 4 * D_MODEL
ALPHA = (2 * DEPTH) ** 0.25
BETA = (8 * DEPTH) ** -0.25
LN_EPS = 1e-5
NORM_EPS = 1e-6
ROPE_THETA = 10000.0

ML_HEADS = 4
ML_DK = D_MODEL // (2 * ML_HEADS)
ML_DV = D_MODEL // ML_HEADS
ML_CHUNK = 64
ML_IN = 2 * ML_HEADS * ML_DK + 2 * ML_HEADS * ML_DV + 2 * ML_HEADS

DIL_GROUPS = ((128, 1), (512, 4), (2048, 16))
N_DIL_GROUPS = len(DIL_GROUPS)
DIL_HEADS = 8
DIL_HD = 64
DIL_IN = N_DIL_GROUPS * 3 * DIL_HEADS * DIL_HD

HG_HEADS = 8
HG_DK = D_MODEL // HG_HEADS
HG_DV = D_MODEL // HG_HEADS
HG_CHUNK = 16
HG_IN = 2 * HG_HEADS * HG_DK + 2 * HG_HEADS * HG_DV

SWA_HEADS = 16
SWA_KV_HEADS = 2
SWA_HD = 64
SWA_WINDOW = 128
SWA_IN = (SWA_HEADS + 2 * SWA_KV_HEADS) * SWA_HD

kernel_name = 'hybrid_mlstm_dilated_hgrn2_swa_step'


def layer_norm(x, g, b):
    xf = x.astype(jnp.float32)
    mu = xf.mean(-1, keepdims=True)
    var = jnp.mean(jnp.square(xf - mu), -1, keepdims=True)
    y = (xf - mu) * lax.rsqrt(var + LN_EPS) * g.astype(jnp.float32) + b.astype(jnp.float32)
    return y.astype(x.dtype)


def rms_norm_heads(h, g):
    nh, dv = h.shape[-2:]
    y = h * lax.rsqrt(jnp.mean(jnp.square(h), -1, keepdims=True) + NORM_EPS)
    return y * g.astype(jnp.float32).reshape(nh, dv)


def rope(x, pos):
    half = x.shape[-1] // 2
    inv_freq = jnp.power(ROPE_THETA, -jnp.arange(half, dtype=jnp.float32) / half)
    ang = pos.astype(jnp.float32)[:, None] * inv_freq[None, :]
    cos = jnp.cos(ang)[None, :, None, :]
    sin = jnp.sin(ang)[None, :, None, :]
    xf = x.astype(jnp.float32)
    x1, x2 = xf[..., :half], xf[..., half:]
    return jnp.concatenate([x1 * cos - x2 * sin, x2 * cos + x1 * sin], axis=-1)


def to_chunks(x, L):
    b, t = x.shape[:2]
    x = x.reshape((b, t // L, L) + x.shape[2:])
    return jnp.transpose(x, (1, 0, 3, 2) + tuple(range(4, x.ndim)))


def from_chunks(y):
    y = jnp.transpose(y, (1, 0, 3, 2) + tuple(range(4, y.ndim)))
    return y.reshape((y.shape[0], y.shape[1] * y.shape[2]) + y.shape[3:])


def sq_relu_mlp(x, w1, w2):
    return jnp.square(jax.nn.relu(x @ w1)) @ w2


def mlstm_scan(q, k, v, ig, lf, C0, n0, m0):
    T = q.shape[1]
    L = math.gcd(T, ML_CHUNK)
    causal = jnp.tril(jnp.ones((L, L), dtype=bool))

    def step(carry, blk):
        C, n, m = carry
        qc, kc, vc, ic, fc = blk
        b = jnp.cumsum(fc, axis=-1)
        log_intra = jnp.where(causal, b[..., :, None] - b[..., None, :] + ic[..., None, :], -jnp.inf)
        log_prev = b + m[..., None]
        m_t = jnp.maximum(log_prev, log_intra.max(-1))
        w_intra = jnp.exp(log_intra - m_t[..., None])
        w_prev = jnp.exp(log_prev - m_t)
        a = jnp.einsum('bhtd,bhsd->bhts', qc, kc) * w_intra
        num = jnp.einsum('bhts,bhsv->bhtv', a, vc) + w_prev[..., None] * jnp.einsum('bhtd,bhdv->bhtv', qc, C)
        den = a.sum(-1) + w_prev * jnp.einsum('bhtd,bhd->bht', qc, n)
        h = num / jnp.maximum(jnp.abs(den), jnp.exp(-m_t))[..., None]
        m_new = m_t[..., -1]
        w_end = jnp.exp(b[..., -1:] - b + ic - m_new[..., None])
        decay = jnp.exp(b[..., -1] + m - m_new)
        C_new = decay[..., None, None] * C + jnp.einsum('bhs,bhsd,bhsv->bhdv', w_end, kc, vc)
        n_new = decay[..., None] * n + jnp.einsum('bhs,bhsd->bhd', w_end, kc)
        return (C_new, n_new, m_new), h

    blocks = (to_chunks(q, L), to_chunks(k, L), to_chunks(v, L), to_chunks(ig, L), to_chunks(lf, L))
    (C, n, m), h = lax.scan(step, (C0, n0, m0), blocks)
    return from_chunks(h), C, n, m


def mlstm_mixer(x, C0, n0, m0, w_in, b_gates, norm_g, w_out):
    bsz, T, _ = x.shape
    qk = ML_HEADS * ML_DK
    vd = ML_HEADS * ML_DV
    p = (x @ w_in).astype(jnp.float32)
    q = p[..., :qk].reshape(bsz, T, ML_HEADS, ML_DK)
    k = p[..., qk:2 * qk].reshape(bsz, T, ML_HEADS, ML_DK) * ML_DK ** -0.5
    v = p[..., 2 * qk:2 * qk + vd].reshape(bsz, T, ML_HEADS, ML_DV)
    og = p[..., 2 * qk + vd:2 * qk + 2 * vd]
    gates = p[..., 2 * qk + 2 * vd:] + b_gates.astype(jnp.float32)
    ig = gates[..., :ML_HEADS]
    lf = jax.nn.log_sigmoid(gates[..., ML_HEADS:])
    h, C, n, m = mlstm_scan(q, k, v, ig, lf, C0.astype(jnp.float32), n0.astype(jnp.float32), m0.astype(jnp.float32))
    h = rms_norm_heads(h, norm_g).reshape(bsz, T, vd) * jax.nn.sigmoid(og)
    return h.astype(x.dtype) @ w_out, C, n, m


def banded_causal_attention(q, k, v, span, sinks=None):
    bsz, T, hq, dh = q.shape
    hk = k.shape[2]
    grp = hq // hk
    W = span
    Tp = -(-T // W) * W
    if Tp != T:
        padw = ((0, 0), (0, Tp - T), (0, 0), (0, 0))
        q, k, v = jnp.pad(q, padw), jnp.pad(k, padw), jnp.pad(v, padw)
    nb = Tp // W
    qb = q.reshape(bsz, nb, W, hk, grp, dh) * dh ** -0.5
    kb = k.reshape(bsz, nb, W, hk, dh)
    vb = v.reshape(bsz, nb, W, hk, dh)
    shift = ((0, 0), (1, 0), (0, 0), (0, 0), (0, 0))
    kk = jnp.concatenate([jnp.pad(kb, shift)[:, :-1], kb], axis=2)
    vv = jnp.concatenate([jnp.pad(vb, shift)[:, :-1], vb], axis=2)
    s = jnp.einsum('bnqhgd,bnkhd->bnhgqk', qb, kk)
    qi = jnp.arange(W)[:, None]
    ki = jnp.arange(2 * W)[None, :]
    dist = W + qi - ki
    band = (dist >= 0) & (dist <= W)
    valid = band[None] & ((jnp.arange(nb)[:, None, None] > 0) | (ki >= W)[None])
    s = jnp.where(valid[None, :, None, None], s, -jnp.inf)
    m = s.max(-1)
    if sinks is not None:
        sk = sinks.astype(jnp.float32).reshape(hk, grp)[None, None, :, :, None]
        m = jnp.maximum(m, sk)
    p = jnp.exp(s - m[..., None])
    l = p.sum(-1)
    if sinks is not None:
        l = l + jnp.exp(sk - m)
    o = jnp.einsum('bnhgqk,bnkhd->bnqhgd', p, vv) / jnp.transpose(l, (0, 1, 4, 2, 3))[..., None]
    lse = jnp.transpose(m + jnp.log(l), (0, 1, 4, 2, 3))
    return o.reshape(bsz, Tp, hq, dh)[:, :T], lse.reshape(bsz, Tp, hq)[:, :T]


def dil_project(x, pos, w_in):
    bsz, T, _ = x.shape
    p = (x @ w_in).reshape(bsz, T, N_DIL_GROUPS, 3, DIL_HEADS, DIL_HD)
    return [(rope(p[:, :, g, 0], pos), rope(p[:, :, g, 1], pos), p[:, :, g, 2].astype(jnp.float32))
            for g in range(N_DIL_GROUPS)]


def dil_merge(outs, lses, x, w_out):
    alpha = jax.nn.softmax(jnp.stack(lses, 0), axis=0)
    y = jnp.sum(alpha[..., None] * jnp.stack(outs, 0), axis=0)
    bsz, T = y.shape[:2]
    return y.reshape(bsz, T, DIL_HEADS * DIL_HD).astype(x.dtype) @ w_out


def dilated_group_prompt(q, k, v, dilation, span):
    bsz, T, nh, dh = q.shape
    ts = T // dilation
    def fold(a):
        return a.reshape(bsz, ts, dilation, nh, dh).transpose(0, 2, 1, 3, 4).reshape(bsz * dilation, ts, nh, dh)
    o, lse = banded_causal_attention(fold(q), fold(k), fold(v), span)
    o = o.reshape(bsz, dilation, ts, nh, dh).transpose(0, 2, 1, 3, 4).reshape(bsz, T, nh, dh)
    lse = lse.reshape(bsz, dilation, ts, nh).transpose(0, 2, 1, 3).reshape(bsz, T, nh)
    return o, lse


def dilated_mixer_prompt(x, pos, w_in, w_out):
    T = x.shape[1]
    outs, lses, kvs = [], [], []
    for g, (q, k, v) in enumerate(dil_project(x, pos, w_in)):
        win, dil = DIL_GROUPS[g]
        o, lse = dilated_group_prompt(q, k, v, dil, win // dil)
        outs.append(o)
        lses.append(lse)
        kvs.append(jnp.stack([k, v], axis=2)[:, T - min(win, T):])
    return dil_merge(outs, lses, x, w_out), kvs


def dilated_group_sample(q, k, v, kv_buf, dilation, span):
    nbat, S, nh, dh = q.shape
    L = kv_buf.shape[1]
    buf = kv_buf.astype(jnp.float32)
    kc = jnp.concatenate([buf[:, :, 0], k], axis=1)
    vc = jnp.concatenate([buf[:, :, 1], v], axis=1)
    idx = L + jnp.arange(S)[:, None] - dilation * jnp.arange(span + 1)[None, :]
    valid = idx >= 0
    idx = jnp.maximum(idx, 0)
    kg = kc[:, idx]
    vg = vc[:, idx]
    s = jnp.einsum('nshd,nsjhd->nshj', q * dh ** -0.5, kg)
    s = jnp.where(valid[None, :, None, :], s, -jnp.inf)
    m = s.max(-1)
    p = jnp.exp(s - m[..., None])
    l = p.sum(-1)
    o = jnp.einsum('nshj,nsjhd->nshd', p, vg) / l[..., None]
    new_buf = jnp.concatenate([buf, jnp.stack([k, v], axis=2)], axis=1)[:, S:]
    return o, m + jnp.log(l), new_buf


def dilated_mixer_sample(x, pos, bufs, w_in, w_out):
    outs, lses, kvs = [], [], []
    for g, (q, k, v) in enumerate(dil_project(x, pos, w_in)):
        win, dil = DIL_GROUPS[g]
        o, lse, nb = dilated_group_sample(q, k, v, bufs[g], dil, win // dil)
        outs.append(o)
        lses.append(lse)
        kvs.append(nb)
    return dil_merge(outs, lses, x, w_out), kvs


def hgrn2_scan(q, k, lf, v, S0):
    T = q.shape[1]
    L = math.gcd(T, HG_CHUNK)
    causal = jnp.tril(jnp.ones((L, L), dtype=bool))

    def step(S, blk):
        qc, kc, fc, vc = blk
        b = jnp.cumsum(fc, axis=2)
        rel = jnp.where(causal[:, :, None], b[:, :, :, None, :] - b[:, :, None, :, :], -jnp.inf)
        a = jnp.einsum('bhtd,bhtsd,bhsd->bhts', qc, jnp.exp(rel), kc)
        o = jnp.einsum('bhts,bhsv->bhtv', a, vc) + jnp.einsum('bhtd,bhdv->bhtv', qc * jnp.exp(b), S)
        b_end = b[:, :, -1]
        S_new = jnp.exp(b_end)[..., None] * S + jnp.einsum('bhsd,bhsv->bhdv', jnp.exp(b_end[:, :, None, :] - b) * kc, vc)
        return S_new, o

    blocks = (to_chunks(q, L), to_chunks(k, L), to_chunks(lf, L), to_chunks(v, L))
    S, o = lax.scan(step, S0, blocks)
    return from_chunks(o), S


def hgrn2_mixer(x, S0, w_in, b_f, lb_logits, layer_idx, norm_g, w_out):
    bsz, T, _ = x.shape
    wk = HG_HEADS * HG_DK
    wv = HG_HEADS * HG_DV
    p = (x @ w_in).astype(jnp.float32)
    q = jax.nn.silu(p[..., :wk])
    fz = p[..., wk:2 * wk] + b_f.astype(jnp.float32)
    i = p[..., 2 * wk:2 * wk + wv]
    g = p[..., 2 * wk + wv:]
    cum = jnp.cumsum(jax.nn.softmax(lb_logits.astype(jnp.float32), axis=0), axis=0)
    lb = cum[layer_idx] - cum[0]
    fg = lb + (1.0 - lb) * jax.nn.sigmoid(fz)
    lf = jnp.log(fg).reshape(bsz, T, HG_HEADS, HG_DK)
    k = (1.0 - fg).reshape(bsz, T, HG_HEADS, HG_DK)
    o, S = hgrn2_scan(q.reshape(bsz, T, HG_HEADS, HG_DK), k, lf, i.reshape(bsz, T, HG_HEADS, HG_DV), S0.astype(jnp.float32))
    o = rms_norm_heads(o, norm_g).reshape(bsz, T, wv) * jax.nn.sigmoid(g)
    return o.astype(x.dtype) @ w_out, S


def swa_project(x, pos, w_in):
    bsz, T, _ = x.shape
    p = x @ w_in
    nq = SWA_HEADS * SWA_HD
    nk = SWA_KV_HEADS * SWA_HD
    q = rope(p[..., :nq].reshape(bsz, T, SWA_HEADS, SWA_HD), pos)
    k = rope(p[..., nq:nq + nk].reshape(bsz, T, SWA_KV_HEADS, SWA_HD), pos)
    v = p[..., nq + nk:].reshape(bsz, T, SWA_KV_HEADS, SWA_HD).astype(jnp.float32)
    return q, k, v


def swa_mixer_prompt(x, pos, w_in, sinks, w_out):
    bsz, T, _ = x.shape
    q, k, v = swa_project(x, pos, w_in)
    o, _ = banded_causal_attention(q, k, v, SWA_WINDOW, sinks)
    kv = jnp.stack([k, v], axis=2)[:, T - min(SWA_WINDOW, T):]
    return o.reshape(bsz, T, SWA_HEADS * SWA_HD).astype(x.dtype) @ w_out, kv


def swa_mixer_sample(x, pos, kv_buf, w_in, sinks, w_out):
    nbat, S, _ = x.shape
    q, k, v = swa_project(x, pos, w_in)
    L = kv_buf.shape[1]
    grp = SWA_HEADS // SWA_KV_HEADS
    buf = kv_buf.astype(jnp.float32)
    kc = jnp.concatenate([buf[:, :, 0], k], axis=1)
    vc = jnp.concatenate([buf[:, :, 1], v], axis=1)
    qg = q.reshape(nbat, S, SWA_KV_HEADS, grp, SWA_HD) * SWA_HD ** -0.5
    s = jnp.einsum('nshgd,nkhd->nhgsk', qg, kc)
    dist = L + jnp.arange(S)[:, None] - jnp.arange(L + S)[None, :]
    valid = (dist >= 0) & (dist <= SWA_WINDOW)
    s = jnp.where(valid, s, -jnp.inf)
    sk = sinks.astype(jnp.float32).reshape(SWA_KV_HEADS, grp)[None, :, :, None]
    m = jnp.maximum(s.max(-1), sk)
    p = jnp.exp(s - m[..., None])
    l = p.sum(-1) + jnp.exp(sk - m)
    o = jnp.einsum('nhgsk,nkhd->nshgd', p, vc) / jnp.transpose(l, (0, 3, 1, 2))[..., None]
    new_buf = jnp.concatenate([buf, jnp.stack([k, v], axis=2)], axis=1)[:, S:]
    return o.reshape(nbat, S, SWA_HEADS * SWA_HD).astype(x.dtype) @ w_out, new_buf


def setup_inputs(seed: int = 0) -> dict:
    key = jax.random.key(seed)
    keys = iter(jax.random.split(key, 40))

    def nrm(shape, scale=1.0):
        return scale * jax.random.normal(next(keys), shape, jnp.float32)

    dil_lens = [min(w, PAST_LEN) for w, _ in DIL_GROUPS]
    swa_len = min(SWA_WINDOW, PAST_LEN)
    return dict(
        x_prompt=nrm((BATCH, SEQ, D_MODEL)),
        x_sample=nrm((DEC_BATCH, DEC_SEQ, D_MODEL)),
        state_mlstm_C=nrm((N_MLSTM, DEC_BATCH, ML_HEADS, ML_DK, ML_DV), 0.5),
        state_mlstm_n=nrm((N_MLSTM, DEC_BATCH, ML_HEADS, ML_DK), 0.5),
        state_mlstm_m=nrm((N_MLSTM, DEC_BATCH, ML_HEADS), 0.5),
        cache_dil_kv0=nrm((N_DIL, DEC_BATCH, dil_lens[0], 2, DIL_HEADS, DIL_HD)),
        cache_dil_kv1=nrm((N_DIL, DEC_BATCH, dil_lens[1], 2, DIL_HEADS, DIL_HD)),
        cache_dil_kv2=nrm((N_DIL, DEC_BATCH, dil_lens[2], 2, DIL_HEADS, DIL_HD)),
        state_hgrn_S=nrm((N_HGRN, DEC_BATCH, HG_HEADS, HG_DK, HG_DV)),
        cache_swa_kv=nrm((N_SWA, DEC_BATCH, swa_len, 2, SWA_KV_HEADS, SWA_HD)),
        mlstm_w_in=nrm((N_MLSTM, D_MODEL, ML_IN), D_MODEL ** -0.5),
        mlstm_b_gates=jnp.concatenate([nrm((N_MLSTM, ML_HEADS), 0.1), 3.0 + nrm((N_MLSTM, ML_HEADS), 0.1)], axis=-1),
        mlstm_norm_g=1.0 + nrm((N_MLSTM, ML_HEADS * ML_DV), 0.02),
        mlstm_w_out=nrm((N_MLSTM, ML_HEADS * ML_DV, D_MODEL), BETA * (ML_HEADS * ML_DV) ** -0.5),
        dil_w_in=nrm((N_DIL, D_MODEL, DIL_IN), D_MODEL ** -0.5),
        dil_w_out=nrm((N_DIL, DIL_HEADS * DIL_HD, D_MODEL), BETA * (DIL_HEADS * DIL_HD) ** -0.5),
        hgrn_w_in=nrm((N_HGRN, D_MODEL, HG_IN), D_MODEL ** -0.5),
        hgrn_b_f=nrm((N_HGRN, HG_HEADS * HG_DK), 0.1),
        hgrn_lb_logits=nrm((DEPTH, HG_HEADS * HG_DK), 0.5),
        hgrn_norm_g=1.0 + nrm((N_HGRN, HG_HEADS * HG_DV), 0.02),
        hgrn_w_out=nrm((N_HGRN, HG_HEADS * HG_DV, D_MODEL), BETA * (HG_HEADS * HG_DV) ** -0.5),
        swa_w_in=nrm((N_SWA, D_MODEL, SWA_IN), D_MODEL ** -0.5),
        swa_sinks=nrm((N_SWA, SWA_HEADS), 0.5),
        swa_w_out=nrm((N_SWA, SWA_HEADS * SWA_HD, D_MODEL), BETA * (SWA_HEADS * SWA_HD) ** -0.5),
        ln1_g=1.0 + nrm((DEPTH, D_MODEL), 0.02),
        ln1_b=nrm((DEPTH, D_MODEL), 0.02),
        ln2_g=1.0 + nrm((DEPTH, D_MODEL), 0.02),
        ln2_b=nrm((DEPTH, D_MODEL), 0.02),
        mlp_w1=nrm((DEPTH, D_MODEL, D_FF), D_MODEL ** -0.5),
        mlp_w2=nrm((DEPTH, D_FF, D_MODEL), BETA * D_FF ** -0.5),
    )


def reference(x_prompt, x_sample, state_mlstm_C, state_mlstm_n, state_mlstm_m,
              cache_dil_kv0, cache_dil_kv1, cache_dil_kv2, state_hgrn_S, cache_swa_kv,
              mlstm_w_in, mlstm_b_gates, mlstm_norm_g, mlstm_w_out,
              dil_w_in, dil_w_out,
              hgrn_w_in, hgrn_b_f, hgrn_lb_logits, hgrn_norm_g, hgrn_w_out,
              swa_w_in, swa_sinks, swa_w_out,
              ln1_g, ln1_b, ln2_g, ln2_b, mlp_w1, mlp_w2):
    hp, hs = x_prompt, x_sample
    bp = x_prompt.shape[0]
    pos_p = jnp.arange(x_prompt.shape[1], dtype=jnp.int32)
    pos_s = PAST_LEN + jnp.arange(x_sample.shape[1], dtype=jnp.int32)
    dil_caches = (cache_dil_kv0, cache_dil_kv1, cache_dil_kv2)
    ml_C_p, ml_C_s, ml_n_p, ml_n_s, ml_m_p, ml_m_s = [], [], [], [], [], []
    dil_p = [[] for _ in DIL_GROUPS]
    dil_s = [[] for _ in DIL_GROUPS]
    hg_p, hg_s, swa_p, swa_s = [], [], [], []
    for i in range(DEPTH):
        kind, occ = i % N_MIXERS, i // N_MIXERS
        if kind == MIX_MLSTM:
            zC = jnp.zeros((bp, ML_HEADS, ML_DK, ML_DV), jnp.float32)
            zn = jnp.zeros((bp, ML_HEADS, ML_DK), jnp.float32)
            zm = jnp.zeros((bp, ML_HEADS), jnp.float32)
            mix_p, C, n, m = mlstm_mixer(hp, zC, zn, zm, mlstm_w_in[occ], mlstm_b_gates[occ], mlstm_norm_g[occ], mlstm_w_out[occ])
            ml_C_p.append(C)
            ml_n_p.append(n)
            ml_m_p.append(m)
            mix_s, C, n, m = mlstm_mixer(hs, state_mlstm_C[occ], state_mlstm_n[occ], state_mlstm_m[occ],
                                         mlstm_w_in[occ], mlstm_b_gates[occ], mlstm_norm_g[occ], mlstm_w_out[occ])
            ml_C_s.append(C)
            ml_n_s.append(n)
            ml_m_s.append(m)
        elif kind == MIX_DIL:
            mix_p, kvs = dilated_mixer_prompt(hp, pos_p, dil_w_in[occ], dil_w_out[occ])
            for g in range(N_DIL_GROUPS):
                dil_p[g].append(kvs[g])
            mix_s, kvs = dilated_mixer_sample(hs, pos_s, [c[occ] for c in dil_caches], dil_w_in[occ], dil_w_out[occ])
            for g in range(N_DIL_GROUPS):
                dil_s[g].append(kvs[g])
        elif kind == MIX_HGRN:
            zS = jnp.zeros((bp, HG_HEADS, HG_DK, HG_DV), jnp.float32)
            mix_p, S_new = hgrn2_mixer(hp, zS, hgrn_w_in[occ], hgrn_b_f[occ], hgrn_lb_logits, i, hgrn_norm_g[occ], hgrn_w_out[occ])
            hg_p.append(S_new)
            mix_s, S_new = hgrn2_mixer(hs, state_hgrn_S[occ], hgrn_w_in[occ], hgrn_b_f[occ], hgrn_lb_logits, i, hgrn_norm_g[occ], hgrn_w_out[occ])
            hg_s.append(S_new)
        else:
            mix_p, kv = swa_mixer_prompt(hp, pos_p, swa_w_in[occ], swa_sinks[occ], swa_w_out[occ])
            swa_p.append(kv)
            mix_s, kv = swa_mixer_sample(hs, pos_s, cache_swa_kv[occ], swa_w_in[occ], swa_sinks[occ], swa_w_out[occ])
            swa_s.append(kv)
        hp = layer_norm(ALPHA * hp + mix_p, ln1_g[i], ln1_b[i])
        hs = layer_norm(ALPHA * hs + mix_s, ln1_g[i], ln1_b[i])
        hp = layer_norm(ALPHA * hp + sq_relu_mlp(hp, mlp_w1[i], mlp_w2[i]), ln2_g[i], ln2_b[i])
        hs = layer_norm(ALPHA * hs + sq_relu_mlp(hs, mlp_w1[i], mlp_w2[i]), ln2_g[i], ln2_b[i])
    return (hp, hs,
            jnp.stack(ml_C_p), jnp.stack(ml_C_s), jnp.stack(ml_n_p), jnp.stack(ml_n_s),
            jnp.stack(ml_m_p), jnp.stack(ml_m_s),
            jnp.stack(dil_p[0]), jnp.stack(dil_s[0]), jnp.stack(dil_p[1]), jnp.stack(dil_s[1]),
            jnp.stack(dil_p[2]), jnp.stack(dil_s[2]),
            jnp.stack(hg_p), jnp.stack(hg_s), jnp.stack(swa_p), jnp.stack(swa_s))
```

```python
import functools
import math

import numpy as np
import jax
import jax.numpy as jnp
from jax import lax
from jax.experimental import pallas as pl
from jax.experimental.pallas import tpu as pltpu

F32 = jnp.float32
BF16 = jnp.bfloat16

D_MODEL = 1024
DEPTH = 4
PAST_LEN = 8192
D_FF = 4 * D_MODEL
ALPHA = (2 * DEPTH) ** 0.25
LN_EPS = 1e-5
NORM_EPS = 1e-6
ROPE_THETA = 10000.0

ML_HEADS = 4
ML_DK = 128
ML_DV = 256
ML_QK = ML_HEADS * ML_DK
ML_VD = ML_HEADS * ML_DV
ML_CHUNK = 256

DIL_GROUPS = ((128, 1), (512, 4), (2048, 16))
DIL_HEADS = 8
HEAD_DIM = 64
DIL_GW = 3 * DIL_HEADS * HEAD_DIM
SPAN = 128

HG_HEADS = 8
HG_DK = 128
HG_CHUNK = 128
HG_W = HG_HEADS * HG_DK

SWA_HEADS = 16
SWA_KV_HEADS = 2
SWA_Q = SWA_HEADS * HEAD_DIM
SWA_KV = SWA_KV_HEADS * HEAD_DIM

LANES = 128
NEG = -1e30
VMEM_LIMIT = 48 * 1024 * 1024


def _params(n_axes, vmem=VMEM_LIMIT):
    return pltpu.CompilerParams(dimension_semantics=("arbitrary",) * n_axes, vmem_limit_bytes=vmem)


def _dot(a, b):
    return jnp.dot(a, b, preferred_element_type=F32)


def _dot_nt(a, b):
    return lax.dot_general(a, b, (((1,), (1,)), ((), ())), preferred_element_type=F32)


def _split3(x):
    hi = x.astype(BF16)
    r1 = x - hi.astype(F32)
    mid = r1.astype(BF16)
    lo = (r1 - mid.astype(F32)).astype(BF16)
    return hi, mid, lo


def _dot01_left(a01, x):
    hi, mid, lo = _split3(x)
    return _dot(a01, hi) + _dot(a01, mid) + _dot(a01, lo)


def _dot01_right(x, a01):
    hi, mid, lo = _split3(x)
    return _dot(hi, a01) + _dot(mid, a01) + _dot(lo, a01)


def _log_sigmoid(x):
    return jnp.minimum(x, 0.0) - jnp.log1p(jnp.exp(-jnp.abs(x)))


def _layer_norm(z, g, b):
    mu = jnp.mean(z, axis=-1, keepdims=True)
    zc = z - mu
    var = jnp.mean(zc * zc, axis=-1, keepdims=True)
    return zc * lax.rsqrt(var + LN_EPS) * g + b


def _rms_gate(h, g, gate):
    ms = jnp.mean(h * h, axis=-1, keepdims=True)
    return h * lax.rsqrt(ms + NORM_EPS) * g * jax.nn.sigmoid(gate)


def _mm_kernel(x_ref, w_ref, o_ref):
    o_ref[...] = _dot(x_ref[...].astype(BF16), w_ref[...])


def _matmul(x, w_bf, tm, tn):
    m, k = x.shape
    n = w_bf.shape[1]
    return pl.pallas_call(
        _mm_kernel,
        grid=(m // tm, n // tn),
        in_specs=[pl.BlockSpec((tm, k), lambda i, j: (i, 0)),
                  pl.BlockSpec((k, tn), lambda i, j: (0, j))],
        out_specs=pl.BlockSpec((tm, tn), lambda i, j: (i, j)),
        out_shape=jax.ShapeDtypeStruct((m, n), F32),
        compiler_params=_params(2),
        name="matmul",
    )(x, w_bf)


def _rope_mm_kernel(flags_ref, x_ref, w_ref, cos_ref, sin_ref, mask_ref, o_ref, *, tn):
    j = pl.program_id(1)
    acc = _dot(x_ref[...].astype(BF16), w_ref[...])

    @pl.when(flags_ref[j] == 0)
    def _():
        o_ref[...] = acc

    @pl.when(flags_ref[j] != 0)
    def _():
        reps = tn // LANES
        cos = jnp.concatenate([cos_ref[...]] * reps, axis=1)
        sin = jnp.concatenate([sin_ref[...]] * reps, axis=1)
        lane = lax.broadcasted_iota(jnp.int32, acc.shape, 1)
        first_half = (lane & (HEAD_DIM - 1)) < HEAD_DIM // 2
        partner = jnp.where(first_half, pltpu.roll(acc, tn - HEAD_DIM // 2, 1),
                            pltpu.roll(acc, HEAD_DIM // 2, 1))
        roped = acc * cos + partner * sin
        o_ref[...] = jnp.where(mask_ref[...] > 0.0, roped, acc)


def _rope_matmul(x, w_bf, cos, sin, flags, mask, tm, tn):
    m, k = x.shape
    n = w_bf.shape[1]
    t_blocks = cos.shape[0] // tm
    grid_spec = pltpu.PrefetchScalarGridSpec(
        num_scalar_prefetch=1,
        grid=(m // tm, n // tn),
        in_specs=[pl.BlockSpec((tm, k), lambda i, j, f: (i, 0)),
                  pl.BlockSpec((k, tn), lambda i, j, f: (0, j)),
                  pl.BlockSpec((tm, LANES), lambda i, j, f: (i % t_blocks, 0)),
                  pl.BlockSpec((tm, LANES), lambda i, j, f: (i % t_blocks, 0)),
                  pl.BlockSpec((1, tn), lambda i, j, f: (0, j))],
        out_specs=pl.BlockSpec((tm, tn), lambda i, j, f: (i, j)),
    )
    return pl.pallas_call(
        functools.partial(_rope_mm_kernel, tn=tn),
        grid_spec=grid_spec,
        out_shape=jax.ShapeDtypeStruct((m, n), F32),
        compiler_params=_params(2),
        name="rope_matmul",
    )(flags, x, w_bf, cos, sin, mask)


def _rope_tables(pos):
    half = HEAD_DIM // 2
    inv_freq = jnp.power(ROPE_THETA, -jnp.arange(half, dtype=F32) / half)
    ang = pos.astype(F32)[:, None] * inv_freq[None, :]
    cos = jnp.cos(ang)
    sin = jnp.sin(ang)
    cos = jnp.concatenate([cos, cos, cos, cos], axis=1)
    sin = jnp.concatenate([-sin, sin, -sin, sin], axis=1)
    return cos, sin


def _out_ln_kernel(o_ref, w_ref, x_ref, g_ref, b_ref, y_ref):
    z = ALPHA * x_ref[...] + _dot(o_ref[...].astype(BF16), w_ref[...])
    y_ref[...] = _layer_norm(z, g_ref[...], b_ref[...])


def _out_ln(o, w_bf, x, g, b, tm):
    m, kin = o.shape
    row = lambda i: (i, 0)
    const = lambda i: (0, 0)
    return pl.pallas_call(
        _out_ln_kernel,
        grid=(m // tm,),
        in_specs=[pl.BlockSpec((tm, kin), row), pl.BlockSpec((kin, D_MODEL), const),
                  pl.BlockSpec((tm, D_MODEL), row), pl.BlockSpec((1, D_MODEL), const),
                  pl.BlockSpec((1, D_MODEL), const)],
        out_specs=pl.BlockSpec((tm, D_MODEL), row),
        out_shape=jax.ShapeDtypeStruct((m, D_MODEL), F32),
        compiler_params=_params(1),
        name="out_ln",
    )(o, w_bf, x, g, b)


def _dil_out_ln_kernel(o0, o1, o2, l0, l1, l2, w_ref, x_ref, g_ref, b_ref, y_ref):
    a0, a1, a2 = l0[...], l1[...], l2[...]
    mx = jnp.maximum(jnp.maximum(a0, a1), a2)
    e0, e1, e2 = jnp.exp(a0 - mx), jnp.exp(a1 - mx), jnp.exp(a2 - mx)
    y = (e0 * o0[...] + e1 * o1[...] + e2 * o2[...]) / (e0 + e1 + e2)
    z = ALPHA * x_ref[...] + _dot(y.astype(BF16), w_ref[...])
    y_ref[...] = _layer_norm(z, g_ref[...], b_ref[...])


def _dil_out_ln(outs, lses, w_bf, x, g, b, tm):
    m, kin = outs[0].shape
    row = lambda i: (i, 0)
    const = lambda i: (0, 0)
    return pl.pallas_call(
        _dil_out_ln_kernel,
        grid=(m // tm,),
        in_specs=[pl.BlockSpec((tm, kin), row)] * 6 + [
            pl.BlockSpec((kin, D_MODEL), const), pl.BlockSpec((tm, D_MODEL), row),
            pl.BlockSpec((1, D_MODEL), const), pl.BlockSpec((1, D_MODEL), const)],
        out_specs=pl.BlockSpec((tm, D_MODEL), row),
        out_shape=jax.ShapeDtypeStruct((m, D_MODEL), F32),
        compiler_params=_params(1),
        name="dil_out_ln",
    )(*outs, *lses, w_bf, x, g, b)


FF_CHUNK = 1024


def _mlp_ln_kernel(x_ref, w1_ref, w2_ref, g_ref, b_ref, y_ref):
    x = x_ref[...]
    xb = x.astype(BF16)
    acc = jnp.zeros(x.shape, F32)
    for c in range(D_FF // FF_CHUNK):
        a = _dot(xb, w1_ref[:, c * FF_CHUNK:(c + 1) * FF_CHUNK])
        a = jnp.square(jnp.maximum(a, 0.0)).astype(BF16)
        acc = acc + _dot(a, w2_ref[c * FF_CHUNK:(c + 1) * FF_CHUNK, :])
    y_ref[...] = _layer_norm(ALPHA * x + acc, g_ref[...], b_ref[...])


def _mlp_ln(x, w1_bf, w2_bf, g, b, tm):
    m = x.shape[0]
    row = lambda i: (i, 0)
    const = lambda i: (0, 0)
    resident = pl.Buffered(1)
    return pl.pallas_call(
        _mlp_ln_kernel,
        grid=(m // tm,),
        in_specs=[pl.BlockSpec((tm, D_MODEL), row),
                  pl.BlockSpec((D_MODEL, D_FF), const, pipeline_mode=resident),
                  pl.BlockSpec((D_FF, D_MODEL), const, pipeline_mode=resident),
                  pl.BlockSpec((1, D_MODEL), const), pl.BlockSpec((1, D_MODEL), const)],
        out_specs=pl.BlockSpec((tm, D_MODEL), row),
        out_shape=jax.ShapeDtypeStruct((m, D_MODEL), F32),
        compiler_params=_params(1),
        name="mlp_ln",
    )(x, w1_bf, w2_bf, g, b)


def _mlstm_prompt_kernel(q_ref, k_ref, v_ref, og_ref, gc_ref, gr_ref, brow_ref, bcol_ref, g_ref,
                         h_ref, c_ref, n_ref, m_ref, *, nb, chunk):
    step = pl.program_id(0)

    @pl.when(step == 0)
    def _():
        c_ref[...] = jnp.zeros(c_ref.shape, F32)
        n_ref[...] = jnp.zeros(n_ref.shape, F32)
        m_ref[...] = jnp.zeros(m_ref.shape, F32)

    row = lax.broadcasted_iota(jnp.int32, (chunk, chunk), 0)
    col = lax.broadcasted_iota(jnp.int32, (chunk, chunk), 1)
    causal = col <= row
    tril = jnp.where(causal, 1.0, 0.0).astype(BF16)
    triu = jnp.where(row <= col, 1.0, 0.0).astype(BF16)
    scale = ML_DK ** -0.5
    last = chunk - 1

    for b in range(nb):
        gc = gc_ref[b] + brow_ref[...]
        ig_cols = gc[:, :LANES]
        b_cols = _dot01_left(tril, _log_sigmoid(gc[:, LANES:]))
        gr = gr_ref[b] + bcol_ref[...]
        b_rows = _dot01_right(_log_sigmoid(gr), triu)
        for h in range(ML_HEADS):
            bc = b_cols[:, h:h + 1]
            igc = ig_cols[:, h:h + 1]
            br = b_rows[ML_HEADS + h:ML_HEADS + h + 1, :]
            igr = gr[h:h + 1, :]
            m_prev = m_ref[b, h][:, 0:1]
            log_intra = jnp.where(causal, bc - br + igr, NEG)
            log_prev = bc + m_prev
            m_t = jnp.maximum(log_prev, jnp.max(log_intra, axis=1, keepdims=True))
            w_intra = jnp.exp(log_intra - m_t)
            w_prev = jnp.exp(log_prev - m_t)
            q = q_ref[b, :, h * ML_DK:(h + 1) * ML_DK]
            k = k_ref[b, :, h * ML_DK:(h + 1) * ML_DK] * scale
            v = v_ref[b, :, h * ML_DV:(h + 1) * ML_DV]
            qb, kb, vb = q.astype(BF16), k.astype(BF16), v.astype(BF16)
            a = _dot_nt(qb, kb) * w_intra
            c_old = c_ref[b, h]
            n_old = n_ref[b, h]
            num = _dot(a.astype(BF16), vb) + w_prev * _dot(qb, c_old.astype(BF16))
            den = jnp.sum(a, axis=1, keepdims=True) + w_prev * jnp.sum(q * n_old, axis=1, keepdims=True)
            hh = num / jnp.maximum(jnp.abs(den), jnp.exp(-m_t))
            m_new = m_t[last:last + 1, :]
            b_end = bc[last:last + 1, :]
            w_end = jnp.exp(b_end - bc + igc - m_new)
            decay = jnp.exp(b_end + m_prev - m_new)
            kw = k * w_end
            c_ref[b, h] = decay * c_old + _dot(kw.T.astype(BF16), vb)
            n_ref[b, h] = decay * n_old + jnp.sum(kw, axis=0, keepdims=True)
            m_ref[b, h] = jnp.broadcast_to(m_new, (1, LANES))
            sl = slice(h * ML_DV, (h + 1) * ML_DV)
            h_ref[b, :, sl] = _rms_gate(hh, g_ref[:, sl], og_ref[b, :, sl])


def _mlstm_prompt(p, gates, b_gates, norm_g, nb, t):
    chunk = ML_CHUNK
    g8 = jnp.concatenate([gates[..., :ML_HEADS], gates[..., LANES:LANES + ML_HEADS]], axis=-1)
    g_rows = jnp.transpose(g8, (0, 2, 1))
    zeros = jnp.zeros((LANES - ML_HEADS,), F32)
    b_row = jnp.concatenate([b_gates[:ML_HEADS], zeros, b_gates[ML_HEADS:], zeros])[None, :]
    b_col = b_gates[:, None]
    kernel = functools.partial(_mlstm_prompt_kernel, nb=nb, chunk=chunk)
    const2 = lambda c: (0, 0)
    return pl.pallas_call(
        kernel,
        grid=(t // chunk,),
        in_specs=[pl.BlockSpec((nb, chunk, ML_QK), lambda c: (0, c, 0)),
                  pl.BlockSpec((nb, chunk, ML_QK), lambda c: (0, c, 1)),
                  pl.BlockSpec((nb, chunk, ML_VD), lambda c: (0, c, 1)),
                  pl.BlockSpec((nb, chunk, ML_VD), lambda c: (0, c, 2)),
                  pl.BlockSpec((nb, chunk, 2 * LANES), lambda c: (0, c, 0)),
                  pl.BlockSpec((nb, 2 * ML_HEADS, chunk), lambda c: (0, 0, c)),
                  pl.BlockSpec((1, 2 * LANES), const2),
                  pl.BlockSpec((2 * ML_HEADS, 1), const2),
                  pl.BlockSpec((1, ML_VD), const2)],
        out_specs=[pl.BlockSpec((nb, chunk, ML_VD), lambda c: (0, c, 0)),
                   pl.BlockSpec((nb, ML_HEADS, ML_DK, ML_DV), lambda c: (0, 0, 0, 0)),
                   pl.BlockSpec((nb, ML_HEADS, 1, ML_DK), lambda c: (0, 0, 0, 0)),
                   pl.BlockSpec((nb, ML_HEADS, 1, LANES), lambda c: (0, 0, 0, 0))],
        out_shape=[jax.ShapeDtypeStruct((nb, t, ML_VD), F32),
                   jax.ShapeDtypeStruct((nb, ML_HEADS, ML_DK, ML_DV), F32),
                   jax.ShapeDtypeStruct((nb, ML_HEADS, 1, ML_DK), F32),
                   jax.ShapeDtypeStruct((nb, ML_HEADS, 1, LANES), F32)],
        compiler_params=_params(1),
        name="mlstm_prompt",
    )(p, p, p, p, gates, g_rows, b_row, b_col, norm_g)


def _column(row_vec, eye):
    return jnp.sum(jnp.where(eye, row_vec, 0.0), axis=1, keepdims=True)


def _mlstm_sample_kernel(p_ref, gate_ref, bias_ref, g_ref, c_ref, n_ref, m_ref,
                         h_ref, c_out, n_out, m_out):
    eye = (lax.broadcasted_iota(jnp.int32, (ML_DK, ML_DK), 0)
           == lax.broadcasted_iota(jnp.int32, (ML_DK, ML_DK), 1))
    gates = gate_ref[0] + bias_ref[...]
    ig = gates[:, :LANES]
    lf = _log_sigmoid(gates[:, LANES:])
    m_old = m_ref[0]
    log_prev = lf + m_old
    m_t = jnp.maximum(log_prev, ig)
    w_i_all = jnp.exp(ig - m_t)
    w_p_all = jnp.exp(log_prev - m_t)
    floor_all = jnp.exp(-m_t)
    m_out[0] = m_t
    scale = ML_DK ** -0.5
    for h in range(ML_HEADS):
        q = p_ref[0, :, h * ML_DK:(h + 1) * ML_DK]
        k = p_ref[0, :, ML_QK + h * ML_DK:ML_QK + (h + 1) * ML_DK] * scale
        v = p_ref[0, :, 2 * ML_QK + h * ML_DV:2 * ML_QK + (h + 1) * ML_DV]
        og = p_ref[0, :, 2 * ML_QK + ML_VD + h * ML_DV:2 * ML_QK + ML_VD + (h + 1) * ML_DV]
        w_i = w_i_all[:, h:h + 1]
        w_p = w_p_all[:, h:h + 1]
        c_old = c_ref[0, 0, h]
        n_old = n_ref[0, 0, h:h + 1, :]
        q_col = _column(q, eye)
        k_col = _column(k, eye)
        a = jnp.sum(q * k, axis=1, keepdims=True) * w_i
        num = a * v + w_p * jnp.sum(c_old * q_col, axis=0, keepdims=True)
        den = a + w_p * jnp.sum(q * n_old, axis=1, keepdims=True)
        hh = num / jnp.maximum(jnp.abs(den), floor_all[:, h:h + 1])
        c_out[0, 0, h] = w_p * c_old + (w_i * k_col) * v
        n_out[0, 0, h:h + 1, :] = w_p * n_old + w_i * k
        sl = slice(h * ML_DV, (h + 1) * ML_DV)
        h_ref[0, :, sl] = _rms_gate(hh, g_ref[:, sl], og)


def _mlstm_sample(p, gates, b_gates, norm_g, c0, n0, m0):
    n = p.shape[0]
    zeros = jnp.zeros((LANES - ML_HEADS,), F32)
    b_row = jnp.concatenate([b_gates[:ML_HEADS], zeros, b_gates[ML_HEADS:], zeros])[None, :]
    m_pad = jnp.pad(m0[0], ((0, 0), (0, LANES - ML_HEADS)))[:, None, :]
    row3 = lambda i: (i, 0, 0)
    const2 = lambda i: (0, 0)
    h, c1, n1, m1 = pl.pallas_call(
        _mlstm_sample_kernel,
        grid=(n,),
        in_specs=[pl.BlockSpec((1, 1, p.shape[1]), row3),
                  pl.BlockSpec((1, 1, 2 * LANES), row3),
                  pl.BlockSpec((1, 2 * LANES), const2),
                  pl.BlockSpec((1, ML_VD), const2),
                  pl.BlockSpec((1, 1, ML_HEADS, ML_DK, ML_DV), lambda i: (0, i, 0, 0, 0)),
                  pl.BlockSpec((1, 1, ML_HEADS, ML_DK), lambda i: (0, i, 0, 0)),
                  pl.BlockSpec((1, 1, LANES), row3)],
        out_specs=[pl.BlockSpec((1, 1, ML_VD), row3),
                   pl.BlockSpec((1, 1, ML_HEADS, ML_DK, ML_DV), lambda i: (0, i, 0, 0, 0)),
                   pl.BlockSpec((1, 1, ML_HEADS, ML_DK), lambda i: (0, i, 0, 0)),
                   pl.BlockSpec((1, 1, LANES), row3)],
        out_shape=[jax.ShapeDtypeStruct((n, 1, ML_VD), F32),
                   jax.ShapeDtypeStruct(c0.shape, F32),
                   jax.ShapeDtypeStruct(n0.shape, F32),
                   jax.ShapeDtypeStruct((n, 1, LANES), F32)],
        compiler_params=_params(1),
        name="mlstm_sample",
    )(p[:, None, :], gates[:, None, :], b_row, norm_g, c0, n0, m_pad)
    return h[:, 0, :], c1, n1, m1[:, 0, :ML_HEADS][None]


HG_LEVELS = (1, 2, 4, 8, 16, 32, 64)


def _bcast_rows(x, group, which):
    n, w = x.shape
    x3 = x.reshape(n // group, group, w)
    return jnp.broadcast_to(x3[:, which:which + 1, :], x3.shape).reshape(n, w)


def _level_log_decay(level, lf, b_incl, t_idx, chunk):
    odd = ((t_idx >> int(math.log2(level))) & 1) == 1
    if level >= 8:
        since = b_incl - _bcast_rows(b_incl - lf, level, 0)
        until = _bcast_rows(b_incl, level, level - 1) - b_incl
        return jnp.where(odd, since, until)
    pos = t_idx & (level - 1)
    since = lf
    until = jnp.zeros_like(lf)
    for i in range(1, level):
        since = since + jnp.where(pos >= i, pltpu.roll(lf, i, 0), 0.0)
        until = until + jnp.where(pos <= level - 1 - i, pltpu.roll(lf, chunk - i, 0), 0.0)
    return jnp.where(odd, since, until)


def _hgrn_prompt_kernel(q_ref, f_ref, i_ref, g_ref, bf_ref, lb_ref, ng_ref, o_ref, s_ref, st_ref,
                        *, nb, chunk, layer_idx):
    step = pl.program_id(0)

    @pl.when(step == 0)
    def _():
        st_ref[...] = jnp.zeros(st_ref.shape, F32)

    logits = [lb_ref[i:i + 1, :] for i in range(DEPTH)]
    mx = functools.reduce(jnp.maximum, logits)
    es = [jnp.exp(l - mx) for l in logits]
    tot = functools.reduce(lambda a, c: a + c, es)
    cum = []
    for e in es:
        cum.append(e / tot if not cum else cum[-1] + e / tot)
    lb = cum[layer_idx] - cum[0]

    row = lax.broadcasted_iota(jnp.int32, (chunk, chunk), 0)
    col = lax.broadcasted_iota(jnp.int32, (chunk, chunk), 1)
    tril = jnp.where(col <= row, 1.0, 0.0).astype(BF16)
    eye = row == col
    level_masks = []
    for level in HG_LEVELS:
        sh = int(math.log2(level))
        u, w = row >> sh, col >> sh
        level_masks.append(((u & 1) * 4096 + (u - w)) == 4097)
    t_idx = lax.broadcasted_iota(jnp.int32, (chunk, HG_DK), 0)
    last = chunk - 1

    for b in range(nb):
        fg_all = lb + (1.0 - lb) * jax.nn.sigmoid(f_ref[b] + bf_ref[...])
        lf_all = jnp.log(fg_all)
        b_all = _dot01_left(tril, lf_all)
        for h in range(HG_HEADS):
            sl = slice(h * HG_DK, (h + 1) * HG_DK)
            qx = q_ref[b, :, sl]
            q = qx * jax.nn.sigmoid(qx)
            k = 1.0 - fg_all[:, sl]
            lf = lf_all[:, sl]
            bi = b_all[:, sl]
            v = i_ref[b, :, sl]
            vb = v.astype(BF16)
            a = jnp.where(eye, _dot_nt(q.astype(BF16), k.astype(BF16)), 0.0)
            for level, mask in zip(HG_LEVELS, level_masks):
                e = jnp.exp(_level_log_decay(level, lf, bi, t_idx, chunk))
                a = a + jnp.where(mask, _dot_nt((q * e).astype(BF16), (k * e).astype(BF16)), 0.0)
            st = st_ref[b, h]
            o = _dot(a.astype(BF16), vb) + _dot_nt((q * jnp.exp(bi)).astype(BF16), st.astype(BF16))
            b_end = bi[last:last + 1, :]
            kd = k * jnp.exp(b_end - bi)
            st_ref[b, h] = st * jnp.exp(b_end) + _dot(v.T.astype(BF16), kd.astype(BF16))
            o_ref[b, :, sl] = _rms_gate(o, ng_ref[:, sl], g_ref[b, :, sl])

    @pl.when(step == pl.num_programs(0) - 1)
    def _():
        for b in range(nb):
            for h in range(HG_HEADS):
                s_ref[b, h] = st_ref[b, h].T


def _hgrn_prompt(p, b_f, lb_logits, norm_g, nb, t, layer_idx):
    chunk = HG_CHUNK
    kernel = functools.partial(_hgrn_prompt_kernel, nb=nb, chunk=chunk, layer_idx=layer_idx)
    const2 = lambda c: (0, 0)
    blk = lambda j: pl.BlockSpec((nb, chunk, HG_W), lambda c: (0, c, j))
    return pl.pallas_call(
        kernel,
        grid=(t // chunk,),
        in_specs=[blk(0), blk(1), blk(2), blk(3),
                  pl.BlockSpec((1, HG_W), const2), pl.BlockSpec((DEPTH, HG_W), const2),
                  pl.BlockSpec((1, HG_W), const2)],
        out_specs=[pl.BlockSpec((nb, chunk, HG_W), lambda c: (0, c, 0)),
                   pl.BlockSpec((nb, HG_HEADS, HG_DK, HG_DK), lambda c: (0, 0, 0, 0))],
        out_shape=[jax.ShapeDtypeStruct((nb, t, HG_W), F32),
                   jax.ShapeDtypeStruct((nb, HG_HEADS, HG_DK, HG_DK), F32)],
        scratch_shapes=[pltpu.VMEM((nb, HG_HEADS, HG_DK, HG_DK), F32)],
        compiler_params=_params(1),
        name="hgrn_prompt",
    )(p, p, p, p, b_f, lb_logits, norm_g)


def _hgrn_sample_kernel(p_ref, bf_ref, lb_ref, ng_ref, s_ref, o_ref, s_out, *, layer_idx):
    logits = [lb_ref[i:i + 1, :] for i in range(DEPTH)]
    mx = functools.reduce(jnp.maximum, logits)
    es = [jnp.exp(l - mx) for l in logits]
    tot = functools.reduce(lambda a, c: a + c, es)
    cum = []
    for e in es:
        cum.append(e / tot if not cum else cum[-1] + e / tot)
    lb = cum[layer_idx] - cum[0]
    eye = (lax.broadcasted_iota(jnp.int32, (HG_DK, HG_DK), 0)
           == lax.broadcasted_iota(jnp.int32, (HG_DK, HG_DK), 1))
    qx = p_ref[0, :, 0:HG_W]
    q_all = qx * jax.nn.sigmoid(qx)
    fg_all = lb + (1.0 - lb) * jax.nn.sigmoid(p_ref[0, :, HG_W:2 * HG_W] + bf_ref[...])
    for h in range(HG_HEADS):
        sl = slice(h * HG_DK, (h + 1) * HG_DK)
        q = q_all[:, sl]
        fg = fg_all[:, sl]
        k = 1.0 - fg
        v = p_ref[0, :, 2 * HG_W + h * HG_DK:2 * HG_W + (h + 1) * HG_DK]
        gate = p_ref[0, :, 3 * HG_W + h * HG_DK:3 * HG_W + (h + 1) * HG_DK]
        s_old = s_ref[0, 0, h]
        decay = jnp.exp(jnp.log(fg))
        a = jnp.sum(q * k, axis=1, keepdims=True)
        o = a * v + jnp.sum(s_old * _column(q * decay, eye), axis=0, keepdims=True)
        s_out[0, 0, h] = _column(decay, eye) * s_old + _column(k, eye) * v
        o_ref[0, :, sl] = _rms_gate(o, ng_ref[:, sl], gate)


def _hgrn_sample(p, b_f, lb_logits, norm_g, s0, layer_idx):
    n = p.shape[0]
    row3 = lambda i: (i, 0, 0)
    const2 = lambda i: (0, 0)
    state = pl.BlockSpec((1, 1, HG_HEADS, HG_DK, HG_DK), lambda i: (0, i, 0, 0, 0))
    o, s1 = pl.pallas_call(
        functools.partial(_hgrn_sample_kernel, layer_idx=layer_idx),
        grid=(n,),
        in_specs=[pl.BlockSpec((1, 1, p.shape[1]), row3), pl.BlockSpec((1, HG_W), const2),
                  pl.BlockSpec((DEPTH, HG_W), const2), pl.BlockSpec((1, HG_W), const2), state],
        out_specs=[pl.BlockSpec((1, 1, HG_W), row3), state],
        out_shape=[jax.ShapeDtypeStruct((n, 1, HG_W), F32), jax.ShapeDtypeStruct(s0.shape, F32)],
        compiler_params=_params(1),
        name="hgrn_sample",
    )(p[:, None, :], b_f, lb_logits, norm_g, s0)
    return o[:, 0, :], s1


def _band_kernel(*refs, n_heads, group, with_sinks, with_lse):
    refs = list(refs)
    sink_ref = refs.pop(0) if with_sinks else None
    q_ref, ko_ref, kp_ref, vo_ref, vp_ref, o_ref = refs[:6]
    lse_ref = refs[6] if with_lse else None
    blk = pl.program_id(2)
    qi = lax.broadcasted_iota(jnp.int32, (SPAN, 2 * SPAN), 0)
    ki = lax.broadcasted_iota(jnp.int32, (SPAN, 2 * SPAN), 1)
    first = jnp.where(blk > 0, 0, SPAN)
    valid = (ki >= qi) & (ki <= qi + SPAN) & (ki >= first)
    lo_q = lax.broadcasted_iota(jnp.int32, (SPAN, LANES), 1) < HEAD_DIM
    lo_kv = lax.broadcasted_iota(jnp.int32, (2 * SPAN, LANES), 1) < HEAD_DIM
    kcat = jnp.concatenate([kp_ref[...], ko_ref[...]], axis=0)
    vcat = jnp.concatenate([vp_ref[...], vo_ref[...]], axis=0)
    tiles = {}

    def kv_tile(name, src, kv_head, want_hi):
        key = (name, kv_head, want_hi)
        if key not in tiles:
            t = src[:, (kv_head // 2) * LANES:(kv_head // 2 + 1) * LANES]
            if (kv_head % 2 == 1) != want_hi:
                t = pltpu.roll(t, HEAD_DIM, 1)
            tiles[key] = t
        return tiles[key]

    def softmax_parts(s, head):
        s = jnp.where(valid, s, NEG)
        m = jnp.max(s, axis=1, keepdims=True)
        if with_sinks:
            m = jnp.maximum(m, sink_ref[head])
        p = jnp.exp(s - m)
        l = jnp.sum(p, axis=1, keepdims=True)
        if with_sinks:
            l = l + jnp.exp(sink_ref[head] - m)
        return p.astype(BF16), m, l

    scale = HEAD_DIM ** -0.5
    for j in range(n_heads // 2):
        sl = slice(j * LANES, (j + 1) * LANES)
        qpair = q_ref[:, sl] * scale
        h_lo, h_hi = 2 * j, 2 * j + 1
        q_lo = jnp.where(lo_q, qpair, 0.0).astype(BF16)
        q_hi = jnp.where(lo_q, 0.0, qpair).astype(BF16)
        s_lo = _dot_nt(q_lo, kv_tile("k", kcat, h_lo // group, False).astype(BF16))
        s_hi = _dot_nt(q_hi, kv_tile("k", kcat, h_hi // group, True).astype(BF16))
        p_lo, m_lo, l_lo = softmax_parts(s_lo, h_lo)
        p_hi, m_hi, l_hi = softmax_parts(s_hi, h_hi)
        v_lo = jnp.where(lo_kv, kv_tile("v", vcat, h_lo // group, False), 0.0).astype(BF16)
        v_hi = jnp.where(lo_kv, 0.0, kv_tile("v", vcat, h_hi // group, True)).astype(BF16)
        o = _dot(p_lo, v_lo) + _dot(p_hi, v_hi)
        o_ref[:, sl] = o / jnp.where(lo_q, l_lo, l_hi)
        if with_lse:
            lse_ref[:, sl] = jnp.where(lo_q, m_lo + jnp.log(l_lo), m_hi + jnp.log(l_hi))


def _dil_prompt_group(p, g, nb, t):
    _, dil = DIL_GROUPS[g]
    sub = t // dil
    n_cols = p.shape[-1] // (DIL_HEADS * HEAD_DIM)
    w = DIL_HEADS * HEAD_DIM
    pv = p.reshape(nb, sub, dil * p.shape[-1])
    own = lambda j: pl.BlockSpec((None, SPAN, w), lambda b, r, n: (b, n, r * n_cols + 3 * g + j))
    prev = lambda j: pl.BlockSpec((None, SPAN, w),
                                  lambda b, r, n: (b, jnp.maximum(n - 1, 0), r * n_cols + 3 * g + j))
    out = pl.BlockSpec((None, SPAN, w), lambda b, r, n: (b, n, r))
    shape = jax.ShapeDtypeStruct((nb, sub, dil * w), F32)
    o, lse = pl.pallas_call(
        functools.partial(_band_kernel, n_heads=DIL_HEADS, group=1, with_sinks=False, with_lse=True),
        grid=(nb, dil, sub // SPAN),
        in_specs=[own(0), own(1), prev(1), own(2), prev(2)],
        out_specs=[out, out],
        out_shape=[shape, shape],
        compiler_params=_params(3),
        name="dil_prompt_g%d" % g,
    )(pv, pv, pv, pv, pv)
    return o.reshape(nb * t, w), lse.reshape(nb * t, w)


def _swa_prompt(p, sinks, nb, t):
    kcol = SWA_Q // SWA_KV
    o = pl.pallas_call(
        functools.partial(_band_kernel, n_heads=SWA_HEADS, group=SWA_HEADS // SWA_KV_HEADS,
                          with_sinks=True, with_lse=False),
        grid=(nb, 1, t // SPAN),
        in_specs=[pl.BlockSpec(memory_space=pltpu.SMEM),
                  pl.BlockSpec((None, SPAN, SWA_Q), lambda b, r, n: (b, n, 0)),
                  pl.BlockSpec((None, SPAN, SWA_KV), lambda b, r, n: (b, n, kcol)),
                  pl.BlockSpec((None, SPAN, SWA_KV), lambda b, r, n: (b, jnp.maximum(n - 1, 0), kcol)),
                  pl.BlockSpec((None, SPAN, SWA_KV), lambda b, r, n: (b, n, kcol + 1)),
                  pl.BlockSpec((None, SPAN, SWA_KV), lambda b, r, n: (b, jnp.maximum(n - 1, 0), kcol + 1))],
        out_specs=pl.BlockSpec((None, SPAN, SWA_Q), lambda b, r, n: (b, n, 0)),
        out_shape=jax.ShapeDtypeStruct((nb, t, SWA_Q), F32),
        compiler_params=_params(3),
        name="swa_prompt",
    )(sinks, p, p, p, p, p)
    return o.reshape(nb * t, SWA_Q)


SAMPLE_BLOCK = 4


def _attend_cache(q, k_cache, v_cache, k_new, v_new, sink=None):
    s = jnp.sum(k_cache * q[:, None], axis=-1, keepdims=True)
    s_new = jnp.sum(k_new * q, axis=-1, keepdims=True)
    m = jnp.maximum(jnp.max(s, axis=1), s_new)
    if sink is not None:
        m = jnp.maximum(m, sink)
    p = jnp.exp(s - m[:, None])
    p_new = jnp.exp(s_new - m)
    l = jnp.sum(p, axis=1) + p_new
    if sink is not None:
        l = l + jnp.exp(sink - m)
    acc = jnp.sum(p * v_cache, axis=1) + p_new * v_new
    return acc, m, l


def _dil_sample_kernel(qkv_ref, c0_ref, c1_ref, c2_ref, y_ref):
    scale = HEAD_DIM ** -0.5
    accs, lses, ls = [], [], []
    for g, c_ref in enumerate((c0_ref, c1_ref, c2_ref)):
        cache = c_ref[0, :, :, 0]
        q = qkv_ref[:, 3 * g] * scale
        acc, m, l = _attend_cache(q, cache[:, :, 0], cache[:, :, 1], qkv_ref[:, 3 * g + 1], qkv_ref[:, 3 * g + 2])
        accs.append(acc)
        ls.append(l)
        lses.append(m + jnp.log(l))
    mx = jnp.maximum(jnp.maximum(lses[0], lses[1]), lses[2])
    es = [jnp.exp(x - mx) for x in lses]
    tot = es[0] + es[1] + es[2]
    y_ref[...] = sum(acc * (e / (tot * l)) for acc, e, l in zip(accs, es, ls))


def _dil_sample(qkv, caches):
    n = qkv.shape[0]
    nb = SAMPLE_BLOCK
    views, specs = [], []
    for (win, dil), c in zip(DIL_GROUPS, caches):
        assert c.shape[2] == win and win // dil == SPAN
        views.append(c.reshape(1, n, SPAN, dil, 2, DIL_HEADS, HEAD_DIM))
        specs.append(pl.BlockSpec((1, nb, SPAN, 1, 2, DIL_HEADS, HEAD_DIM),
                                  lambda i: (0, i, 0, 0, 0, 0, 0)))
    return pl.pallas_call(
        _dil_sample_kernel,
        grid=(n // nb,),
        in_specs=[pl.BlockSpec((nb, 9, DIL_HEADS, HEAD_DIM), lambda i: (i, 0, 0, 0))] + specs,
        out_specs=pl.BlockSpec((nb, DIL_HEADS, HEAD_DIM), lambda i: (i, 0, 0)),
        out_shape=jax.ShapeDtypeStruct((n, DIL_HEADS, HEAD_DIM), F32),
        compiler_params=_params(1),
        name="dil_sample",
    )(qkv, *views)


def _swa_sample_kernel(q_ref, kv_ref, c_ref, sink_ref, o_ref):
    scale = HEAD_DIM ** -0.5
    grp = SWA_HEADS // SWA_KV_HEADS
    cache = c_ref[0]
    for hk in range(SWA_KV_HEADS):
        q = q_ref[:, hk * grp:(hk + 1) * grp, :] * scale
        acc, _, l = _attend_cache(q, cache[:, :, 0, hk:hk + 1, :], cache[:, :, 1, hk:hk + 1, :],
                                  kv_ref[:, 0, hk:hk + 1, :], kv_ref[:, 1, hk:hk + 1, :],
                                  sink=sink_ref[hk * grp:(hk + 1) * grp, :][None])
        o_ref[:, hk * grp:(hk + 1) * grp, :] = acc / l


def _swa_sample(q, kv_new, cache, sinks):
    n = q.shape[0]
    nb = SAMPLE_BLOCK
    assert cache.shape[2] == SPAN
    return pl.pallas_call(
        _swa_sample_kernel,
        grid=(n // nb,),
        in_specs=[pl.BlockSpec((nb, SWA_HEADS, HEAD_DIM), lambda i: (i, 0, 0)),
                  pl.BlockSpec((nb, 2, SWA_KV_HEADS, HEAD_DIM), lambda i: (i, 0, 0, 0)),
                  pl.BlockSpec((1, nb, SPAN, 2, SWA_KV_HEADS, HEAD_DIM), lambda i: (0, i, 0, 0, 0, 0)),
                  pl.BlockSpec((SWA_HEADS, 1), lambda i: (0, 0))],
        out_specs=pl.BlockSpec((nb, SWA_HEADS, HEAD_DIM), lambda i: (i, 0, 0)),
        out_shape=jax.ShapeDtypeStruct((n, SWA_HEADS, HEAD_DIM), F32),
        compiler_params=_params(1),
        name="swa_sample",
    )(q, kv_new, cache, sinks)


ROLL_PIECES = 16


def _roll_kernel(cache_ref, new_ref, out_ref, sems, *, length):
    n = cache_ref.shape[1]
    per = n // ROLL_PIECES

    def piece(i):
        rows = pl.ds(i * per, per)
        return pltpu.make_async_copy(cache_ref.at[0, rows, pl.ds(1, length - 1)],
                                     out_ref.at[0, rows, pl.ds(0, length - 1)], sems.at[i])

    tail = pltpu.make_async_copy(new_ref, out_ref.at[0, :, length - 1], sems.at[ROLL_PIECES])
    for i in range(ROLL_PIECES):
        piece(i).start()
    tail.start()
    for i in range(ROLL_PIECES):
        piece(i).wait()
    tail.wait()


def _roll_cache(cache, kv_new):
    length = cache.shape[2]
    return pl.pallas_call(
        functools.partial(_roll_kernel, length=length),
        in_specs=[pl.BlockSpec(memory_space=pl.ANY), pl.BlockSpec(memory_space=pl.ANY)],
        out_specs=pl.BlockSpec(memory_space=pl.ANY),
        out_shape=jax.ShapeDtypeStruct(cache.shape, cache.dtype),
        scratch_shapes=[pltpu.SemaphoreType.DMA((ROLL_PIECES + 1,))],
        compiler_params=pltpu.CompilerParams(has_side_effects=True),
        name="roll_cache",
    )(cache, kv_new)


PROMPT_TM = 1024
LN_TM = 512


def _pad_cols(w, width):
    return jnp.pad(w, ((0, 0), (0, width - w.shape[1])))


def kernel(x_prompt, x_sample, state_mlstm_C, state_mlstm_n, state_mlstm_m, cache_dil_kv0, cache_dil_kv1, cache_dil_kv2, state_hgrn_S, cache_swa_kv, mlstm_w_in, mlstm_b_gates, mlstm_norm_g, mlstm_w_out, dil_w_in, dil_w_out, hgrn_w_in, hgrn_b_f, hgrn_lb_logits, hgrn_norm_g, hgrn_w_out, swa_w_in, swa_sinks, swa_w_out, ln1_g, ln1_b, ln2_g, ln2_b, mlp_w1, mlp_w2):
    nb, t, _ = x_prompt.shape
    ns = x_sample.shape[0]
    assert x_sample.shape[1] == 1 and DEPTH == 4
    hp = x_prompt.reshape(nb * t, D_MODEL)
    hs = x_sample.reshape(ns, D_MODEL)
    cos_p, sin_p = _rope_tables(jnp.arange(t, dtype=jnp.int32))
    cos_s, sin_s = _rope_tables(jnp.full((ns,), PAST_LEN, dtype=jnp.int32))

    def finish_layer(i, hp, hs, mix_p, mix_s, w_out, merged=None):
        g1, b1, g2, b2 = ln1_g[i][None], ln1_b[i][None], ln2_g[i][None], ln2_b[i][None]
        w_bf = w_out.astype(BF16)
        if merged is None:
            hp = _out_ln(mix_p, w_bf, hp, g1, b1, LN_TM)
        else:
            hp = _dil_out_ln(merged[0], merged[1], w_bf, hp, g1, b1, LN_TM)
        hs = _out_ln(mix_s, w_bf, hs, g1, b1, ns)
        w1, w2 = mlp_w1[i].astype(BF16), mlp_w2[i].astype(BF16)
        hp = _mlp_ln(hp, w1, w2, g2, b2, LN_TM)
        hs = _mlp_ln(hs, w1, w2, g2, b2, ns)
        return hp, hs

    w_in = mlstm_w_in[0]
    main = 2 * ML_QK + 2 * ML_VD
    w_main = w_in[:, :main].astype(BF16)
    w_gate = jnp.concatenate([_pad_cols(w_in[:, main:main + ML_HEADS], LANES),
                              _pad_cols(w_in[:, main + ML_HEADS:], LANES)], axis=1).astype(BF16)
    pp = _matmul(hp, w_main, PROMPT_TM, 512)
    gp = _matmul(hp, w_gate, PROMPT_TM, 2 * LANES)
    ps = _matmul(hs, w_main, ns, 512)
    gs = _matmul(hs, w_gate, ns, 2 * LANES)
    norm_g = mlstm_norm_g[0][None]
    mix_p, c_p, n_p, m_p = _mlstm_prompt(pp.reshape(nb, t, main), gp.reshape(nb, t, 2 * LANES),
                                         mlstm_b_gates[0], norm_g, nb, t)
    mix_s, c_s, n_s, m_s = _mlstm_sample(ps, gs, mlstm_b_gates[0], norm_g,
                                         state_mlstm_C, state_mlstm_n, state_mlstm_m)
    out_c_p = c_p[None]
    out_n_p = n_p[:, :, 0, :][None]
    out_m_p = m_p[:, :, 0, 0][None]
    hp, hs = finish_layer(0, hp, hs, mix_p.reshape(nb * t, ML_VD), mix_s, mlstm_w_out[0])

    w_in = dil_w_in[0].astype(BF16)
    n_tiles = w_in.shape[1] // 512
    flags = jnp.asarray([0 if j % 3 == 2 else 1 for j in range(n_tiles)], jnp.int32)
    ones = jnp.ones((1, w_in.shape[1]), F32)
    pp = _rope_matmul(hp, w_in, cos_p, sin_p, flags, ones, PROMPT_TM, 512)
    ps = _rope_matmul(hs, w_in, cos_s, sin_s, flags, ones, ns, 512)
    pp3 = pp.reshape(nb, t, -1)
    outs, lses, dil_kv_p = [], [], []
    for g, (win, _) in enumerate(DIL_GROUPS):
        o, lse = _dil_prompt_group(pp3, g, nb, t)
        outs.append(o)
        lses.append(lse)
        kv = pp3[:, t - win:, g * DIL_GW + DIL_HEADS * HEAD_DIM:(g + 1) * DIL_GW]
        dil_kv_p.append(kv.reshape(nb, win, 2, DIL_HEADS, HEAD_DIM)[None])
    qkv_s = ps.reshape(ns, 9, DIL_HEADS, HEAD_DIM)
    caches = (cache_dil_kv0, cache_dil_kv1, cache_dil_kv2)
    mix_s = _dil_sample(qkv_s, caches).reshape(ns, DIL_HEADS * HEAD_DIM)
    dil_kv_s = [_roll_cache(c, qkv_s[:, 3 * g + 1:3 * g + 3]) for g, c in enumerate(caches)]
    hp, hs = finish_layer(1, hp, hs, None, mix_s, dil_w_out[0], merged=(outs, lses))

    w_in = hgrn_w_in[0].astype(BF16)
    pp = _matmul(hp, w_in, PROMPT_TM, 512)
    ps = _matmul(hs, w_in, ns, 512)
    b_f, ng = hgrn_b_f[0][None], hgrn_norm_g[0][None]
    mix_p, s_p = _hgrn_prompt(pp.reshape(nb, t, 4 * HG_W), b_f, hgrn_lb_logits, ng, nb, t, 2)
    mix_s, s_s = _hgrn_sample(ps, b_f, hgrn_lb_logits, ng, state_hgrn_S, 2)
    hp, hs = finish_layer(2, hp, hs, mix_p.reshape(nb * t, HG_W), mix_s, hgrn_w_out[0])

    w_in = swa_w_in[0].astype(BF16)
    width = w_in.shape[1]
    flags = jnp.ones((width // 256,), jnp.int32)
    rope_cols = (jnp.arange(width) < SWA_Q + SWA_KV).astype(F32)[None, :]
    pp = _rope_matmul(hp, w_in, cos_p, sin_p, flags, rope_cols, PROMPT_TM, 256)
    ps = _rope_matmul(hs, w_in, cos_s, sin_s, flags, rope_cols, ns, 256)
    pp3 = pp.reshape(nb, t, width)
    mix_p = _swa_prompt(pp3, swa_sinks[0], nb, t)
    swa_kv_p = pp3[:, t - SPAN:, SWA_Q:].reshape(nb, SPAN, 2, SWA_KV_HEADS, HEAD_DIM)[None]
    q_s = ps[:, :SWA_Q].reshape(ns, SWA_HEADS, HEAD_DIM)
    kv_s = ps[:, SWA_Q:].reshape(ns, 2, SWA_KV_HEADS, HEAD_DIM)
    mix_s = _swa_sample(q_s, kv_s, cache_swa_kv, swa_sinks[0][:, None]).reshape(ns, SWA_Q)
    swa_kv_s = _roll_cache(cache_swa_kv, kv_s)
    hp, hs = finish_layer(3, hp, hs, mix_p, mix_s, swa_w_out[0])

    return (hp.reshape(nb, t, D_MODEL), hs.reshape(ns, 1, D_MODEL),
            out_c_p, c_s, out_n_p, n_s, out_m_p, m_s,
            dil_kv_p[0], dil_kv_s[0], dil_kv_p[1], dil_kv_s[1], dil_kv_p[2], dil_kv_s[2],
            s_p[None], s_s, swa_kv_p, swa_kv_s)
```

```python
import functools
import math

import numpy as np
import jax
import jax.numpy as jnp
from jax import lax
from jax.experimental import pallas as pl
from jax.experimental.pallas import tpu as pltpu

F32 = jnp.float32
BF16 = jnp.bfloat16

D_MODEL = 1024
DEPTH = 4
PAST_LEN = 8192
D_FF = 4 * D_MODEL
ALPHA = (2 * DEPTH) ** 0.25
LN_EPS = 1e-5
NORM_EPS = 1e-6
ROPE_THETA = 10000.0

ML_HEADS = 4
ML_DK = 128
ML_DV = 256
ML_QK = ML_HEADS * ML_DK
ML_VD = ML_HEADS * ML_DV
ML_CHUNK = 256

DIL_GROUPS = ((128, 1), (512, 4), (2048, 16))
DIL_HEADS = 8
HEAD_DIM = 64
DIL_GW = 3 * DIL_HEADS * HEAD_DIM
SPAN = 128

HG_HEADS = 8
HG_DK = 128
HG_CHUNK = 128
HG_W = HG_HEADS * HG_DK

SWA_HEADS = 16
SWA_KV_HEADS = 2
SWA_Q = SWA_HEADS * HEAD_DIM
SWA_KV = SWA_KV_HEADS * HEAD_DIM

LANES = 128
NEG = -1e30
VMEM_LIMIT = 48 * 1024 * 1024


def _params(n_axes, vmem=VMEM_LIMIT):
    return pltpu.CompilerParams(dimension_semantics=("arbitrary",) * n_axes, vmem_limit_bytes=vmem)


def _dot(a, b):
    return jnp.dot(a, b, preferred_element_type=F32)


def _dot_nt(a, b):
    return lax.dot_general(a, b, (((1,), (1,)), ((), ())), preferred_element_type=F32)


def _split3(x):
    hi = x.astype(BF16)
    r1 = x - hi.astype(F32)
    mid = r1.astype(BF16)
    lo = (r1 - mid.astype(F32)).astype(BF16)
    return hi, mid, lo


def _dot01_left(a01, x):
    hi, mid, lo = _split3(x)
    return _dot(a01, hi) + _dot(a01, mid) + _dot(a01, lo)


def _dot01_right(x, a01):
    hi, mid, lo = _split3(x)
    return _dot(hi, a01) + _dot(mid, a01) + _dot(lo, a01)


def _log_sigmoid(x):
    return jnp.minimum(x, 0.0) - jnp.log1p(jnp.exp(-jnp.abs(x)))


def _layer_norm(z, g, b):
    mu = jnp.mean(z, axis=-1, keepdims=True)
    zc = z - mu
    var = jnp.mean(zc * zc, axis=-1, keepdims=True)
    return zc * lax.rsqrt(var + LN_EPS) * g + b


def _rms_gate(h, g, gate):
    ms = jnp.mean(h * h, axis=-1, keepdims=True)
    return h * lax.rsqrt(ms + NORM_EPS) * g * jax.nn.sigmoid(gate)


PROJ_CHUNK = 512


def _proj_kernel(*refs, plan):
    if any(rope for _, _, rope in plan):
        x_ref, w_ref, cos_ref, sin_ref, o_ref = refs
    else:
        x_ref, w_ref, o_ref = refs
    xb = x_ref[...].astype(BF16)
    for start, width, rope in plan:
        acc = _dot(xb, w_ref[:, start:start + width])
        if rope:
            reps = width // LANES
            cos = jnp.concatenate([cos_ref[...]] * reps, axis=1)
            sin = jnp.concatenate([sin_ref[...]] * reps, axis=1)
            lane = lax.broadcasted_iota(jnp.int32, acc.shape, 1)
            first_half = (lane & (HEAD_DIM - 1)) < HEAD_DIM // 2
            partner = jnp.where(first_half, pltpu.roll(acc, width - HEAD_DIM // 2, 1),
                                pltpu.roll(acc, HEAD_DIM // 2, 1))
            acc = acc * cos + partner * sin
        o_ref[:, start:start + width] = acc


def _proj(x, w_bf, tm, rope_cols=(), tables=None):
    m, k = x.shape
    n = w_bf.shape[1]
    plan = []
    start = 0
    while start < n:
        rope = any(lo <= start < hi for lo, hi in rope_cols)
        limit = min([hi for lo, hi in rope_cols if lo <= start < hi] +
                    [lo for lo, hi in rope_cols if lo > start] + [n])
        width = min(PROJ_CHUNK, limit - start)
        plan.append((start, width, rope))
        start += width
    row = lambda i: (i, 0)
    in_specs = [pl.BlockSpec((tm, k), row),
                pl.BlockSpec((k, n), lambda i: (0, 0), pipeline_mode=pl.Buffered(1))]
    args = [x, w_bf]
    if rope_cols:
        t_blocks = tables[0].shape[0] // tm
        tab = pl.BlockSpec((tm, LANES), lambda i: (i % t_blocks, 0))
        in_specs += [tab, tab]
        args += list(tables)
    return pl.pallas_call(
        functools.partial(_proj_kernel, plan=tuple(plan)),
        grid=(m // tm,),
        in_specs=in_specs,
        out_specs=pl.BlockSpec((tm, n), row),
        out_shape=jax.ShapeDtypeStruct((m, n), F32),
        compiler_params=_params(1),
        name="proj",
    )(*args)


def _rope_tables(pos):
    half = HEAD_DIM // 2
    inv_freq = jnp.power(ROPE_THETA, -jnp.arange(half, dtype=F32) / half)
    ang = pos.astype(F32)[:, None] * inv_freq[None, :]
    cos = jnp.cos(ang)
    sin = jnp.sin(ang)
    cos = jnp.concatenate([cos, cos, cos, cos], axis=1)
    sin = jnp.concatenate([-sin, sin, -sin, sin], axis=1)
    return cos, sin


def _out_ln_kernel(o_ref, w_ref, x_ref, g_ref, b_ref, y_ref):
    z = ALPHA * x_ref[...] + _dot(o_ref[...].astype(BF16), w_ref[...])
    y_ref[...] = _layer_norm(z, g_ref[...], b_ref[...])


def _out_ln(o, w_bf, x, g, b, tm):
    m, kin = o.shape
    row = lambda i: (i, 0)
    const = lambda i: (0, 0)
    return pl.pallas_call(
        _out_ln_kernel,
        grid=(m // tm,),
        in_specs=[pl.BlockSpec((tm, kin), row), pl.BlockSpec((kin, D_MODEL), const),
                  pl.BlockSpec((tm, D_MODEL), row), pl.BlockSpec((1, D_MODEL), const),
                  pl.BlockSpec((1, D_MODEL), const)],
        out_specs=pl.BlockSpec((tm, D_MODEL), row),
        out_shape=jax.ShapeDtypeStruct((m, D_MODEL), F32),
        compiler_params=_params(1),
        name="out_ln",
    )(o, w_bf, x, g, b)


def _dil_out_ln_kernel(o0, o1, o2, l0, l1, l2, w_ref, x_ref, g_ref, b_ref, y_ref):
    a0, a1, a2 = l0[...], l1[...], l2[...]
    mx = jnp.maximum(jnp.maximum(a0, a1), a2)
    e0, e1, e2 = jnp.exp(a0 - mx), jnp.exp(a1 - mx), jnp.exp(a2 - mx)
    y = (e0 * o0[...] + e1 * o1[...] + e2 * o2[...]) / (e0 + e1 + e2)
    z = ALPHA * x_ref[...] + _dot(y.astype(BF16), w_ref[...])
    y_ref[...] = _layer_norm(z, g_ref[...], b_ref[...])


def _dil_out_ln(outs, lses, w_bf, x, g, b, tm):
    m, kin = outs[0].shape
    row = lambda i: (i, 0)
    const = lambda i: (0, 0)
    return pl.pallas_call(
        _dil_out_ln_kernel,
        grid=(m // tm,),
        in_specs=[pl.BlockSpec((tm, kin), row)] * 6 + [
            pl.BlockSpec((kin, D_MODEL), const), pl.BlockSpec((tm, D_MODEL), row),
            pl.BlockSpec((1, D_MODEL), const), pl.BlockSpec((1, D_MODEL), const)],
        out_specs=pl.BlockSpec((tm, D_MODEL), row),
        out_shape=jax.ShapeDtypeStruct((m, D_MODEL), F32),
        compiler_params=_params(1),
        name="dil_out_ln",
    )(*outs, *lses, w_bf, x, g, b)


FF_CHUNK = 1024


def _mlp_ln_kernel(x_ref, w1_ref, w2_ref, g_ref, b_ref, y_ref):
    x = x_ref[...]
    xb = x.astype(BF16)
    acc = jnp.zeros(x.shape, F32)
    for c in range(D_FF // FF_CHUNK):
        a = _dot(xb, w1_ref[:, c * FF_CHUNK:(c + 1) * FF_CHUNK])
        a = jnp.square(jnp.maximum(a, 0.0)).astype(BF16)
        acc = acc + _dot(a, w2_ref[c * FF_CHUNK:(c + 1) * FF_CHUNK, :])
    y_ref[...] = _layer_norm(ALPHA * x + acc, g_ref[...], b_ref[...])


def _mlp_ln(x, w1_bf, w2_bf, g, b, tm):
    m = x.shape[0]
    row = lambda i: (i, 0)
    const = lambda i: (0, 0)
    resident = pl.Buffered(1)
    return pl.pallas_call(
        _mlp_ln_kernel,
        grid=(m // tm,),
        in_specs=[pl.BlockSpec((tm, D_MODEL), row),
                  pl.BlockSpec((D_MODEL, D_FF), const, pipeline_mode=resident),
                  pl.BlockSpec((D_FF, D_MODEL), const, pipeline_mode=resident),
                  pl.BlockSpec((1, D_MODEL), const), pl.BlockSpec((1, D_MODEL), const)],
        out_specs=pl.BlockSpec((tm, D_MODEL), row),
        out_shape=jax.ShapeDtypeStruct((m, D_MODEL), F32),
        compiler_params=_params(1),
        name="mlp_ln",
    )(x, w1_bf, w2_bf, g, b)


def _mlstm_prompt_kernel(q_ref, k_ref, v_ref, og_ref, gc_ref, gr_ref, brow_ref, bcol_ref, g_ref,
                         h_ref, c_ref, n_ref, m_ref, *, nb, chunk):
    step = pl.program_id(0)

    @pl.when(step == 0)
    def _():
        c_ref[...] = jnp.zeros(c_ref.shape, F32)
        n_ref[...] = jnp.zeros(n_ref.shape, F32)
        m_ref[...] = jnp.zeros(m_ref.shape, F32)

    row = lax.broadcasted_iota(jnp.int32, (chunk, chunk), 0)
    col = lax.broadcasted_iota(jnp.int32, (chunk, chunk), 1)
    causal = col <= row
    tril = jnp.where(causal, 1.0, 0.0).astype(BF16)
    triu = jnp.where(row <= col, 1.0, 0.0).astype(BF16)
    scale = ML_DK ** -0.5
    last = chunk - 1

    for b in range(nb):
        gc = gc_ref[b] + brow_ref[...]
        ig_cols = gc[:, :LANES]
        b_cols = _dot01_left(tril, _log_sigmoid(gc[:, LANES:]))
        gr = gr_ref[b] + bcol_ref[...]
        b_rows = _dot01_right(_log_sigmoid(gr), triu)
        for h in range(ML_HEADS):
            bc = b_cols[:, h:h + 1]
            igc = ig_cols[:, h:h + 1]
            br = b_rows[ML_HEADS + h:ML_HEADS + h + 1, :]
            igr = gr[h:h + 1, :]
            m_prev = m_ref[b, h][:, 0:1]
            log_intra = jnp.where(causal, bc - br + igr, NEG)
            log_prev = bc + m_prev
            m_t = jnp.maximum(log_prev, jnp.max(log_intra, axis=1, keepdims=True))
            w_intra = jnp.exp(log_intra - m_t)
            w_prev = jnp.exp(log_prev - m_t)
            q = q_ref[b, :, h * ML_DK:(h + 1) * ML_DK]
            k = k_ref[b, :, h * ML_DK:(h + 1) * ML_DK] * scale
            v = v_ref[b, :, h * ML_DV:(h + 1) * ML_DV]
            qb, kb, vb = q.astype(BF16), k.astype(BF16), v.astype(BF16)
            a = _dot_nt(qb, kb) * w_intra
            c_old = c_ref[b, h]
            n_old = n_ref[b, h]
            num = _dot(a.astype(BF16), vb) + w_prev * _dot(qb, c_old.astype(BF16))
            den = jnp.sum(a, axis=1, keepdims=True) + w_prev * jnp.sum(q * n_old, axis=1, keepdims=True)
            hh = num / jnp.maximum(jnp.abs(den), jnp.exp(-m_t))
            m_new = m_t[last:last + 1, :]
            b_end = bc[last:last + 1, :]
            w_end = jnp.exp(b_end - bc + igc - m_new)
            decay = jnp.exp(b_end + m_prev - m_new)
            kw = k * w_end
            c_ref[b, h] = decay * c_old + _dot(kw.T.astype(BF16), vb)
            n_ref[b, h] = decay * n_old + jnp.sum(kw, axis=0, keepdims=True)
            m_ref[b, h] = jnp.broadcast_to(m_new, (1, LANES))
            sl = slice(h * ML_DV, (h + 1) * ML_DV)
            h_ref[b, :, sl] = _rms_gate(hh, g_ref[:, sl], og_ref[b, :, sl])


def _mlstm_prompt(p, gates, b_gates, norm_g, nb, t):
    chunk = ML_CHUNK
    g8 = jnp.concatenate([gates[..., :ML_HEADS], gates[..., LANES:LANES + ML_HEADS]], axis=-1)
    g_rows = jnp.transpose(g8, (0, 2, 1))
    zeros = jnp.zeros((LANES - ML_HEADS,), F32)
    b_row = jnp.concatenate([b_gates[:ML_HEADS], zeros, b_gates[ML_HEADS:], zeros])[None, :]
    b_col = b_gates[:, None]
    kernel = functools.partial(_mlstm_prompt_kernel, nb=nb, chunk=chunk)
    const2 = lambda c: (0, 0)
    return pl.pallas_call(
        kernel,
        grid=(t // chunk,),
        in_specs=[pl.BlockSpec((nb, chunk, ML_QK), lambda c: (0, c, 0)),
                  pl.BlockSpec((nb, chunk, ML_QK), lambda c: (0, c, 1)),
                  pl.BlockSpec((nb, chunk, ML_VD), lambda c: (0, c, 1)),
                  pl.BlockSpec((nb, chunk, ML_VD), lambda c: (0, c, 2)),
                  pl.BlockSpec((nb, chunk, 2 * LANES), lambda c: (0, c, 0)),
                  pl.BlockSpec((nb, 2 * ML_HEADS, chunk), lambda c: (0, 0, c)),
                  pl.BlockSpec((1, 2 * LANES), const2),
                  pl.BlockSpec((2 * ML_HEADS, 1), const2),
                  pl.BlockSpec((1, ML_VD), const2)],
        out_specs=[pl.BlockSpec((nb, chunk, ML_VD), lambda c: (0, c, 0)),
                   pl.BlockSpec((nb, ML_HEADS, ML_DK, ML_DV), lambda c: (0, 0, 0, 0)),
                   pl.BlockSpec((nb, ML_HEADS, 1, ML_DK), lambda c: (0, 0, 0, 0)),
                   pl.BlockSpec((nb, ML_HEADS, 1, LANES), lambda c: (0, 0, 0, 0))],
        out_shape=[jax.ShapeDtypeStruct((nb, t, ML_VD), F32),
                   jax.ShapeDtypeStruct((nb, ML_HEADS, ML_DK, ML_DV), F32),
                   jax.ShapeDtypeStruct((nb, ML_HEADS, 1, ML_DK), F32),
                   jax.ShapeDtypeStruct((nb, ML_HEADS, 1, LANES), F32)],
        compiler_params=_params(1),
        name="mlstm_prompt",
    )(p, p, p, p, gates, g_rows, b_row, b_col, norm_g)


def _column(row_vec, eye):
    return jnp.sum(jnp.where(eye, row_vec, 0.0), axis=1, keepdims=True)


def _mlstm_sample_kernel(p_ref, gate_ref, bias_ref, g_ref, c_ref, n_ref, m_ref,
                         h_ref, c_out, n_out, m_out):
    eye = (lax.broadcasted_iota(jnp.int32, (ML_DK, ML_DK), 0)
           == lax.broadcasted_iota(jnp.int32, (ML_DK, ML_DK), 1))
    gates = gate_ref[0] + bias_ref[...]
    ig = gates[:, :LANES]
    lf = _log_sigmoid(gates[:, LANES:])
    m_old = m_ref[0]
    log_prev = lf + m_old
    m_t = jnp.maximum(log_prev, ig)
    w_i_all = jnp.exp(ig - m_t)
    w_p_all = jnp.exp(log_prev - m_t)
    floor_all = jnp.exp(-m_t)
    m_out[0] = m_t
    scale = ML_DK ** -0.5
    for h in range(ML_HEADS):
        q = p_ref[0, :, h * ML_DK:(h + 1) * ML_DK]
        k = p_ref[0, :, ML_QK + h * ML_DK:ML_QK + (h + 1) * ML_DK] * scale
        v = p_ref[0, :, 2 * ML_QK + h * ML_DV:2 * ML_QK + (h + 1) * ML_DV]
        og = p_ref[0, :, 2 * ML_QK + ML_VD + h * ML_DV:2 * ML_QK + ML_VD + (h + 1) * ML_DV]
        w_i = w_i_all[:, h:h + 1]
        w_p = w_p_all[:, h:h + 1]
        c_old = c_ref[0, 0, h]
        n_old = n_ref[0, 0, h:h + 1, :]
        q_col = _column(q, eye)
        k_col = _column(k, eye)
        a = jnp.sum(q * k, axis=1, keepdims=True) * w_i
        num = a * v + w_p * jnp.sum(c_old * q_col, axis=0, keepdims=True)
        den = a + w_p * jnp.sum(q * n_old, axis=1, keepdims=True)
        hh = num / jnp.maximum(jnp.abs(den), floor_all[:, h:h + 1])
        c_out[0, 0, h] = w_p * c_old + (w_i * k_col) * v
        n_out[0, 0, h:h + 1, :] = w_p * n_old + w_i * k
        sl = slice(h * ML_DV, (h + 1) * ML_DV)
        h_ref[0, :, sl] = _rms_gate(hh, g_ref[:, sl], og)


def _mlstm_sample(p, gates, b_gates, norm_g, c0, n0, m0):
    n = p.shape[0]
    zeros = jnp.zeros((LANES - ML_HEADS,), F32)
    b_row = jnp.concatenate([b_gates[:ML_HEADS], zeros, b_gates[ML_HEADS:], zeros])[None, :]
    m_pad = jnp.pad(m0[0], ((0, 0), (0, LANES - ML_HEADS)))[:, None, :]
    row3 = lambda i: (i, 0, 0)
    const2 = lambda i: (0, 0)
    h, c1, n1, m1 = pl.pallas_call(
        _mlstm_sample_kernel,
        grid=(n,),
        in_specs=[pl.BlockSpec((1, 1, p.shape[1]), row3),
                  pl.BlockSpec((1, 1, 2 * LANES), row3),
                  pl.BlockSpec((1, 2 * LANES), const2),
                  pl.BlockSpec((1, ML_VD), const2),
                  pl.BlockSpec((1, 1, ML_HEADS, ML_DK, ML_DV), lambda i: (0, i, 0, 0, 0)),
                  pl.BlockSpec((1, 1, ML_HEADS, ML_DK), lambda i: (0, i, 0, 0)),
                  pl.BlockSpec((1, 1, LANES), row3)],
        out_specs=[pl.BlockSpec((1, 1, ML_VD), row3),
                   pl.BlockSpec((1, 1, ML_HEADS, ML_DK, ML_DV), lambda i: (0, i, 0, 0, 0)),
                   pl.BlockSpec((1, 1, ML_HEADS, ML_DK), lambda i: (0, i, 0, 0)),
                   pl.BlockSpec((1, 1, LANES), row3)],
        out_shape=[jax.ShapeDtypeStruct((n, 1, ML_VD), F32),
                   jax.ShapeDtypeStruct(c0.shape, F32),
                   jax.ShapeDtypeStruct(n0.shape, F32),
                   jax.ShapeDtypeStruct((n, 1, LANES), F32)],
        compiler_params=_params(1),
        name="mlstm_sample",
    )(p[:, None, :], gates[:, None, :], b_row, norm_g, c0, n0, m_pad)
    return h[:, 0, :], c1, n1, m1[:, 0, :ML_HEADS][None]


HG_LEVELS = (1, 2, 4, 8, 16, 32, 64)


def _bcast_rows(x, group, which):
    n, w = x.shape
    x3 = x.reshape(n // group, group, w)
    return jnp.broadcast_to(x3[:, which:which + 1, :], x3.shape).reshape(n, w)


def _level_log_decay(level, lf, b_incl, t_idx, chunk):
    odd = ((t_idx >> int(math.log2(level))) & 1) == 1
    if level >= 8:
        since = b_incl - _bcast_rows(b_incl - lf, level, 0)
        until = _bcast_rows(b_incl, level, level - 1) - b_incl
        return jnp.where(odd, since, until)
    pos = t_idx & (level - 1)
    since = lf
    until = jnp.zeros_like(lf)
    for i in range(1, level):
        since = since + jnp.where(pos >= i, pltpu.roll(lf, i, 0), 0.0)
        until = until + jnp.where(pos <= level - 1 - i, pltpu.roll(lf, chunk - i, 0), 0.0)
    return jnp.where(odd, since, until)


def _hgrn_prompt_kernel(q_ref, f_ref, i_ref, g_ref, bf_ref, lb_ref, ng_ref, o_ref, s_ref, st_ref,
                        *, nb, chunk, layer_idx):
    step = pl.program_id(0)

    @pl.when(step == 0)
    def _():
        st_ref[...] = jnp.zeros(st_ref.shape, F32)

    logits = [lb_ref[i:i + 1, :] for i in range(DEPTH)]
    mx = functools.reduce(jnp.maximum, logits)
    es = [jnp.exp(l - mx) for l in logits]
    tot = functools.reduce(lambda a, c: a + c, es)
    cum = []
    for e in es:
        cum.append(e / tot if not cum else cum[-1] + e / tot)
    lb = cum[layer_idx] - cum[0]

    row = lax.broadcasted_iota(jnp.int32, (chunk, chunk), 0)
    col = lax.broadcasted_iota(jnp.int32, (chunk, chunk), 1)
    tril = jnp.where(col <= row, 1.0, 0.0).astype(BF16)
    eye = row == col
    level_masks = []
    for level in HG_LEVELS:
        sh = int(math.log2(level))
        u, w = row >> sh, col >> sh
        level_masks.append(((u & 1) * 4096 + (u - w)) == 4097)
    t_idx = lax.broadcasted_iota(jnp.int32, (chunk, HG_DK), 0)
    last = chunk - 1

    for b in range(nb):
        fg_all = lb + (1.0 - lb) * jax.nn.sigmoid(f_ref[b] + bf_ref[...])
        lf_all = jnp.log(fg_all)
        b_all = _dot01_left(tril, lf_all)
        for h in range(HG_HEADS):
            sl = slice(h * HG_DK, (h + 1) * HG_DK)
            qx = q_ref[b, :, sl]
            q = qx * jax.nn.sigmoid(qx)
            k = 1.0 - fg_all[:, sl]
            lf = lf_all[:, sl]
            bi = b_all[:, sl]
            v = i_ref[b, :, sl]
            vb = v.astype(BF16)
            a = jnp.where(eye, _dot_nt(q.astype(BF16), k.astype(BF16)), 0.0)
            for level, mask in zip(HG_LEVELS, level_masks):
                e = jnp.exp(_level_log_decay(level, lf, bi, t_idx, chunk))
                a = a + jnp.where(mask, _dot_nt((q * e).astype(BF16), (k * e).astype(BF16)), 0.0)
            st = st_ref[b, h]
            o = _dot(a.astype(BF16), vb) + _dot_nt((q * jnp.exp(bi)).astype(BF16), st.astype(BF16))
            b_end = bi[last:last + 1, :]
            kd = k * jnp.exp(b_end - bi)
            st_ref[b, h] = st * jnp.exp(b_end) + _dot(v.T.astype(BF16), kd.astype(BF16))
            o_ref[b, :, sl] = _rms_gate(o, ng_ref[:, sl], g_ref[b, :, sl])

    @pl.when(step == pl.num_programs(0) - 1)
    def _():
        for b in range(nb):
            for h in range(HG_HEADS):
                s_ref[b, h] = st_ref[b, h].T


def _hgrn_prompt(p, b_f, lb_logits, norm_g, nb, t, layer_idx):
    chunk = HG_CHUNK
    kernel = functools.partial(_hgrn_prompt_kernel, nb=nb, chunk=chunk, layer_idx=layer_idx)
    const2 = lambda c: (0, 0)
    blk = lambda j: pl.BlockSpec((nb, chunk, HG_W), lambda c: (0, c, j))
    return pl.pallas_call(
        kernel,
        grid=(t // chunk,),
        in_specs=[blk(0), blk(1), blk(2), blk(3),
                  pl.BlockSpec((1, HG_W), const2), pl.BlockSpec((DEPTH, HG_W), const2),
                  pl.BlockSpec((1, HG_W), const2)],
        out_specs=[pl.BlockSpec((nb, chunk, HG_W), lambda c: (0, c, 0)),
                   pl.BlockSpec((nb, HG_HEADS, HG_DK, HG_DK), lambda c: (0, 0, 0, 0))],
        out_shape=[jax.ShapeDtypeStruct((nb, t, HG_W), F32),
                   jax.ShapeDtypeStruct((nb, HG_HEADS, HG_DK, HG_DK), F32)],
        scratch_shapes=[pltpu.VMEM((nb, HG_HEADS, HG_DK, HG_DK), F32)],
        compiler_params=_params(1),
        name="hgrn_prompt",
    )(p, p, p, p, b_f, lb_logits, norm_g)


def _hgrn_sample_kernel(p_ref, bf_ref, lb_ref, ng_ref, s_ref, o_ref, s_out, *, layer_idx):
    logits = [lb_ref[i:i + 1, :] for i in range(DEPTH)]
    mx = functools.reduce(jnp.maximum, logits)
    es = [jnp.exp(l - mx) for l in logits]
    tot = functools.reduce(lambda a, c: a + c, es)
    cum = []
    for e in es:
        cum.append(e / tot if not cum else cum[-1] + e / tot)
    lb = cum[layer_idx] - cum[0]
    eye = (lax.broadcasted_iota(jnp.int32, (HG_DK, HG_DK), 0)
           == lax.broadcasted_iota(jnp.int32, (HG_DK, HG_DK), 1))
    qx = p_ref[0, :, 0:HG_W]
    q_all = qx * jax.nn.sigmoid(qx)
    fg_all = lb + (1.0 - lb) * jax.nn.sigmoid(p_ref[0, :, HG_W:2 * HG_W] + bf_ref[...])
    for h in range(HG_HEADS):
        sl = slice(h * HG_DK, (h + 1) * HG_DK)
        q = q_all[:, sl]
        fg = fg_all[:, sl]
        k = 1.0 - fg
        v = p_ref[0, :, 2 * HG_W + h * HG_DK:2 * HG_W + (h + 1) * HG_DK]
        gate = p_ref[0, :, 3 * HG_W + h * HG_DK:3 * HG_W + (h + 1) * HG_DK]
        s_old = s_ref[0, 0, h]
        decay = jnp.exp(jnp.log(fg))
        a = jnp.sum(q * k, axis=1, keepdims=True)
        o = a * v + jnp.sum(s_old * _column(q * decay, eye), axis=0, keepdims=True)
        s_out[0, 0, h] = _column(decay, eye) * s_old + _column(k, eye) * v
        o_ref[0, :, sl] = _rms_gate(o, ng_ref[:, sl], gate)


def _hgrn_sample(p, b_f, lb_logits, norm_g, s0, layer_idx):
    n = p.shape[0]
    row3 = lambda i: (i, 0, 0)
    const2 = lambda i: (0, 0)
    state = pl.BlockSpec((1, 1, HG_HEADS, HG_DK, HG_DK), lambda i: (0, i, 0, 0, 0))
    o, s1 = pl.pallas_call(
        functools.partial(_hgrn_sample_kernel, layer_idx=layer_idx),
        grid=(n,),
        in_specs=[pl.BlockSpec((1, 1, p.shape[1]), row3), pl.BlockSpec((1, HG_W), const2),
                  pl.BlockSpec((DEPTH, HG_W), const2), pl.BlockSpec((1, HG_W), const2), state],
        out_specs=[pl.BlockSpec((1, 1, HG_W), row3), state],
        out_shape=[jax.ShapeDtypeStruct((n, 1, HG_W), F32), jax.ShapeDtypeStruct(s0.shape, F32)],
        compiler_params=_params(1),
        name="hgrn_sample",
    )(p[:, None, :], b_f, lb_logits, norm_g, s0)
    return o[:, 0, :], s1


def _band_kernel(*refs, n_heads, group, with_sinks, with_lse):
    refs = list(refs)
    sink_ref = refs.pop(0) if with_sinks else None
    q_ref, ko_ref, kp_ref, vo_ref, vp_ref, o_ref = refs[:6]
    lse_ref = refs[6] if with_lse else None
    blk = pl.program_id(2)
    qi = lax.broadcasted_iota(jnp.int32, (SPAN, 2 * SPAN), 0)
    ki = lax.broadcasted_iota(jnp.int32, (SPAN, 2 * SPAN), 1)
    first = jnp.where(blk > 0, 0, SPAN)
    valid = (ki >= qi) & (ki <= qi + SPAN) & (ki >= first)
    lo_q = lax.broadcasted_iota(jnp.int32, (SPAN, LANES), 1) < HEAD_DIM
    lo_kv = lax.broadcasted_iota(jnp.int32, (2 * SPAN, LANES), 1) < HEAD_DIM
    kcat = jnp.concatenate([kp_ref[...], ko_ref[...]], axis=0)
    vcat = jnp.concatenate([vp_ref[...], vo_ref[...]], axis=0)
    tiles = {}

    def kv_tile(name, src, kv_head, want_hi):
        key = (name, kv_head, want_hi)
        if key not in tiles:
            t = src[:, (kv_head // 2) * LANES:(kv_head // 2 + 1) * LANES]
            if (kv_head % 2 == 1) != want_hi:
                t = pltpu.roll(t, HEAD_DIM, 1)
            tiles[key] = t
        return tiles[key]

    def softmax_parts(s, head):
        s = jnp.where(valid, s, NEG)
        m = jnp.max(s, axis=1, keepdims=True)
        if with_sinks:
            m = jnp.maximum(m, sink_ref[head])
        p = jnp.exp(s - m)
        l = jnp.sum(p, axis=1, keepdims=True)
        if with_sinks:
            l = l + jnp.exp(sink_ref[head] - m)
        return p.astype(BF16), m, l

    scale = HEAD_DIM ** -0.5
    for j in range(n_heads // 2):
        sl = slice(j * LANES, (j + 1) * LANES)
        qpair = q_ref[:, sl] * scale
        h_lo, h_hi = 2 * j, 2 * j + 1
        q_lo = jnp.where(lo_q, qpair, 0.0).astype(BF16)
        q_hi = jnp.where(lo_q, 0.0, qpair).astype(BF16)
        s_lo = _dot_nt(q_lo, kv_tile("k", kcat, h_lo // group, False).astype(BF16))
        s_hi = _dot_nt(q_hi, kv_tile("k", kcat, h_hi // group, True).astype(BF16))
        p_lo, m_lo, l_lo = softmax_parts(s_lo, h_lo)
        p_hi, m_hi, l_hi = softmax_parts(s_hi, h_hi)
        v_lo = jnp.where(lo_kv, kv_tile("v", vcat, h_lo // group, False), 0.0).astype(BF16)
        v_hi = jnp.where(lo_kv, 0.0, kv_tile("v", vcat, h_hi // group, True)).astype(BF16)
        o = _dot(p_lo, v_lo) + _dot(p_hi, v_hi)
        o_ref[:, sl] = o / jnp.where(lo_q, l_lo, l_hi)
        if with_lse:
            lse_ref[:, sl] = jnp.where(lo_q, m_lo + jnp.log(l_lo), m_hi + jnp.log(l_hi))


def _dil_prompt_group(p, g, nb, t):
    _, dil = DIL_GROUPS[g]
    sub = t // dil
    n_cols = p.shape[-1] // (DIL_HEADS * HEAD_DIM)
    w = DIL_HEADS * HEAD_DIM
    pv = p.reshape(nb, sub, dil * p.shape[-1])
    own = lambda j: pl.BlockSpec((None, SPAN, w), lambda b, r, n: (b, n, r * n_cols + 3 * g + j))
    prev = lambda j: pl.BlockSpec((None, SPAN, w),
                                  lambda b, r, n: (b, jnp.maximum(n - 1, 0), r * n_cols + 3 * g + j))
    out = pl.BlockSpec((None, SPAN, w), lambda b, r, n: (b, n, r))
    shape = jax.ShapeDtypeStruct((nb, sub, dil * w), F32)
    o, lse = pl.pallas_call(
        functools.partial(_band_kernel, n_heads=DIL_HEADS, group=1, with_sinks=False, with_lse=True),
        grid=(nb, dil, sub // SPAN),
        in_specs=[own(0), own(1), prev(1), own(2), prev(2)],
        out_specs=[out, out],
        out_shape=[shape, shape],
        compiler_params=_params(3),
        name="dil_prompt_g%d" % g,
    )(pv, pv, pv, pv, pv)
    return o.reshape(nb * t, w), lse.reshape(nb * t, w)


def _swa_prompt(p, sinks, nb, t):
    kcol = SWA_Q // SWA_KV
    o = pl.pallas_call(
        functools.partial(_band_kernel, n_heads=SWA_HEADS, group=SWA_HEADS // SWA_KV_HEADS,
                          with_sinks=True, with_lse=False),
        grid=(nb, 1, t // SPAN),
        in_specs=[pl.BlockSpec(memory_space=pltpu.SMEM),
                  pl.BlockSpec((None, SPAN, SWA_Q), lambda b, r, n: (b, n, 0)),
                  pl.BlockSpec((None, SPAN, SWA_KV), lambda b, r, n: (b, n, kcol)),
                  pl.BlockSpec((None, SPAN, SWA_KV), lambda b, r, n: (b, jnp.maximum(n - 1, 0), kcol)),
                  pl.BlockSpec((None, SPAN, SWA_KV), lambda b, r, n: (b, n, kcol + 1)),
                  pl.BlockSpec((None, SPAN, SWA_KV), lambda b, r, n: (b, jnp.maximum(n - 1, 0), kcol + 1))],
        out_specs=pl.BlockSpec((None, SPAN, SWA_Q), lambda b, r, n: (b, n, 0)),
        out_shape=jax.ShapeDtypeStruct((nb, t, SWA_Q), F32),
        compiler_params=_params(3),
        name="swa_prompt",
    )(sinks, p, p, p, p, p)
    return o.reshape(nb * t, SWA_Q)


def _decode_roll_kernel(*refs, nb, hb, group, length, dil, with_sinks):
    refs = list(refs)
    sink_ref = refs.pop(0) if with_sinks else None
    q_ref, kn_ref, vn_ref, c_ref, o_ref, lse_ref, cout_ref = refs
    h0 = pl.program_id(1) * hb
    eye = (lax.broadcasted_iota(jnp.int32, (HEAD_DIM, HEAD_DIM), 0)
           == lax.broadcasted_iota(jnp.int32, (HEAD_DIM, HEAD_DIM), 1))
    pos = lax.broadcasted_iota(jnp.int32, (1, length), 1)
    attended = (pos & (dil - 1)) == 0
    is_last = lax.broadcasted_iota(jnp.int32, (HEAD_DIM, length), 1) == length - 1
    scale = HEAD_DIM ** -0.5
    for i in range(nb):
        for j in range(hb):
            kt = c_ref[0, i, 0, j]
            vt = c_ref[0, i, 1, j]
            kn = kn_ref[i, pl.ds(h0 + j, 1), :]
            vn = vn_ref[i, pl.ds(h0 + j, 1), :]
            rows = pl.ds((h0 + j) * group, group)
            q = q_ref[i, rows, :] * scale
            if group == 1:
                s = jnp.sum(kt * _column(q, eye), axis=0, keepdims=True)
            else:
                s = _dot(q.astype(BF16), kt.astype(BF16))
            s = jnp.where(attended, s, NEG)
            s_new = jnp.sum(q * kn, axis=1, keepdims=True)
            m = jnp.maximum(jnp.max(s, axis=1, keepdims=True), s_new)
            if with_sinks:
                sink = sink_ref[rows, :]
                m = jnp.maximum(m, sink)
            p = jnp.exp(s - m)
            p_new = jnp.exp(s_new - m)
            l = jnp.sum(p, axis=1, keepdims=True) + p_new
            if with_sinks:
                l = l + jnp.exp(sink - m)
            if group == 1:
                acc_col = jnp.sum(vt * p, axis=1, keepdims=True)
                acc = jnp.sum(jnp.where(eye, acc_col, 0.0), axis=0, keepdims=True)
            else:
                acc = _dot_nt(p.astype(BF16), vt.astype(BF16))
            o_ref[i, rows, :] = (acc + p_new * vn) / l
            lse_ref[i, rows, :] = jnp.broadcast_to(m + jnp.log(l), (group, HEAD_DIM))
            cout_ref[0, i, 0, j] = jnp.where(is_last, _column(kn, eye), pltpu.roll(kt, length - 1, 1))
            cout_ref[0, i, 1, j] = jnp.where(is_last, _column(vn, eye), pltpu.roll(vt, length - 1, 1))


def _decode_roll(q, k_new, v_new, cache, dil, nb, hb, sinks=None):
    _, n, length, _, kv_heads, d = cache.shape
    heads = q.shape[1]
    group = heads // kv_heads
    cache_t = jnp.transpose(cache, (0, 1, 3, 4, 5, 2))
    cblock = pl.BlockSpec((1, nb, 2, hb, d, length), lambda i, h: (0, i, 0, h, 0, 0))
    per_seq = lambda width: pl.BlockSpec((nb, width, d), lambda i, h: (i, 0, 0))
    in_specs = [per_seq(heads), per_seq(kv_heads), per_seq(kv_heads), cblock]
    args = [q, k_new, v_new, cache_t]
    if sinks is not None:
        in_specs = [pl.BlockSpec((heads, 1), lambda i, h: (0, 0))] + in_specs
        args = [sinks] + args
    o, lse, new_t = pl.pallas_call(
        functools.partial(_decode_roll_kernel, nb=nb, hb=hb, group=group, length=length, dil=dil,
                          with_sinks=sinks is not None),
        grid=(n // nb, kv_heads // hb),
        in_specs=in_specs,
        out_specs=[per_seq(heads), per_seq(heads), cblock],
        out_shape=[jax.ShapeDtypeStruct(q.shape, F32), jax.ShapeDtypeStruct(q.shape, F32),
                   jax.ShapeDtypeStruct(cache_t.shape, F32)],
        compiler_params=_params(2),
        name="decode_roll",
    )(*args)
    return o, lse, jnp.transpose(new_t, (0, 1, 5, 2, 3, 4))


PROMPT_TM = 512
LN_TM = 512
DIL_DECODE_BLOCKS = ((4, 8), (1, 8), (1, 2))
SWA_DECODE_BLOCK = (16, 2)


def _pad_cols(w, width):
    return jnp.pad(w, ((0, 0), (0, width - w.shape[1])))


def kernel(x_prompt, x_sample, state_mlstm_C, state_mlstm_n, state_mlstm_m, cache_dil_kv0, cache_dil_kv1, cache_dil_kv2, state_hgrn_S, cache_swa_kv, mlstm_w_in, mlstm_b_gates, mlstm_norm_g, mlstm_w_out, dil_w_in, dil_w_out, hgrn_w_in, hgrn_b_f, hgrn_lb_logits, hgrn_norm_g, hgrn_w_out, swa_w_in, swa_sinks, swa_w_out, ln1_g, ln1_b, ln2_g, ln2_b, mlp_w1, mlp_w2):
    nb, t, _ = x_prompt.shape
    ns = x_sample.shape[0]
    assert x_sample.shape[1] == 1 and DEPTH == 4
    hp = x_prompt.reshape(nb * t, D_MODEL)
    hs = x_sample.reshape(ns, D_MODEL)
    cos_p, sin_p = _rope_tables(jnp.arange(t, dtype=jnp.int32))
    cos_s, sin_s = _rope_tables(jnp.full((ns,), PAST_LEN, dtype=jnp.int32))

    def finish_layer(i, hp, hs, mix_p, mix_s, w_out, merged=None):
        g1, b1, g2, b2 = ln1_g[i][None], ln1_b[i][None], ln2_g[i][None], ln2_b[i][None]
        w_bf = w_out.astype(BF16)
        if merged is None:
            hp = _out_ln(mix_p, w_bf, hp, g1, b1, LN_TM)
            hs = _out_ln(mix_s, w_bf, hs, g1, b1, ns)
        else:
            hp = _dil_out_ln(mix_p[0], mix_p[1], w_bf, hp, g1, b1, LN_TM)
            hs = _dil_out_ln(mix_s[0], mix_s[1], w_bf, hs, g1, b1, ns)
        w1, w2 = mlp_w1[i].astype(BF16), mlp_w2[i].astype(BF16)
        hp = _mlp_ln(hp, w1, w2, g2, b2, LN_TM)
        hs = _mlp_ln(hs, w1, w2, g2, b2, ns)
        return hp, hs

    w_in = mlstm_w_in[0]
    main = 2 * ML_QK + 2 * ML_VD
    w_main = w_in[:, :main].astype(BF16)
    w_gate = jnp.concatenate([_pad_cols(w_in[:, main:main + ML_HEADS], LANES),
                              _pad_cols(w_in[:, main + ML_HEADS:], LANES)], axis=1).astype(BF16)
    pp = _proj(hp, w_main, PROMPT_TM)
    gp = _proj(hp, w_gate, PROMPT_TM)
    ps = _proj(hs, w_main, ns)
    gs = _proj(hs, w_gate, ns)
    norm_g = mlstm_norm_g[0][None]
    mix_p, c_p, n_p, m_p = _mlstm_prompt(pp.reshape(nb, t, main), gp.reshape(nb, t, 2 * LANES),
                                         mlstm_b_gates[0], norm_g, nb, t)
    mix_s, c_s, n_s, m_s = _mlstm_sample(ps, gs, mlstm_b_gates[0], norm_g,
                                         state_mlstm_C, state_mlstm_n, state_mlstm_m)
    out_c_p = c_p[None]
    out_n_p = n_p[:, :, 0, :][None]
    out_m_p = m_p[:, :, 0, 0][None]
    hp, hs = finish_layer(0, hp, hs, mix_p.reshape(nb * t, ML_VD), mix_s, mlstm_w_out[0])

    w_in = dil_w_in[0].astype(BF16)
    qk_cols = tuple((g * DIL_GW, g * DIL_GW + 2 * DIL_HEADS * HEAD_DIM) for g in range(len(DIL_GROUPS)))
    pp = _proj(hp, w_in, PROMPT_TM, qk_cols, (cos_p, sin_p))
    ps = _proj(hs, w_in, ns, qk_cols, (cos_s, sin_s))
    pp3 = pp.reshape(nb, t, -1)
    outs, lses, dil_kv_p = [], [], []
    for g, (win, _) in enumerate(DIL_GROUPS):
        o, lse = _dil_prompt_group(pp3, g, nb, t)
        outs.append(o)
        lses.append(lse)
        kv = pp3[:, t - win:, g * DIL_GW + DIL_HEADS * HEAD_DIM:(g + 1) * DIL_GW]
        dil_kv_p.append(kv.reshape(nb, win, 2, DIL_HEADS, HEAD_DIM)[None])
    qkv_s = ps.reshape(ns, 9, DIL_HEADS, HEAD_DIM)
    outs_s, lses_s, dil_kv_s = [], [], []
    for g, cache in enumerate((cache_dil_kv0, cache_dil_kv1, cache_dil_kv2)):
        win, dil = DIL_GROUPS[g]
        assert cache.shape[2] == win and win // dil == SPAN
        blk_n, blk_h = DIL_DECODE_BLOCKS[g]
        o, lse, rolled = _decode_roll(qkv_s[:, 3 * g], qkv_s[:, 3 * g + 1], qkv_s[:, 3 * g + 2],
                                      cache, dil, blk_n, blk_h)
        outs_s.append(o.reshape(ns, DIL_HEADS * HEAD_DIM))
        lses_s.append(lse.reshape(ns, DIL_HEADS * HEAD_DIM))
        dil_kv_s.append(rolled)
    hp, hs = finish_layer(1, hp, hs, (outs, lses), (outs_s, lses_s), dil_w_out[0], merged=True)

    w_in = hgrn_w_in[0].astype(BF16)
    pp = _proj(hp, w_in, PROMPT_TM)
    ps = _proj(hs, w_in, ns)
    b_f, ng = hgrn_b_f[0][None], hgrn_norm_g[0][None]
    mix_p, s_p = _hgrn_prompt(pp.reshape(nb, t, 4 * HG_W), b_f, hgrn_lb_logits, ng, nb, t, 2)
    mix_s, s_s = _hgrn_sample(ps, b_f, hgrn_lb_logits, ng, state_hgrn_S, 2)
    hp, hs = finish_layer(2, hp, hs, mix_p.reshape(nb * t, HG_W), mix_s, hgrn_w_out[0])

    w_in = swa_w_in[0].astype(BF16)
    width = w_in.shape[1]
    qk_cols = ((0, SWA_Q + SWA_KV),)
    pp = _proj(hp, w_in, PROMPT_TM, qk_cols, (cos_p, sin_p))
    ps = _proj(hs, w_in, ns, qk_cols, (cos_s, sin_s))
    pp3 = pp.reshape(nb, t, width)
    mix_p = _swa_prompt(pp3, swa_sinks[0], nb, t)
    swa_kv_p = pp3[:, t - SPAN:, SWA_Q:].reshape(nb, SPAN, 2, SWA_KV_HEADS, HEAD_DIM)[None]
    q_s = ps[:, :SWA_Q].reshape(ns, SWA_HEADS, HEAD_DIM)
    kv_s = ps[:, SWA_Q:].reshape(ns, 2, SWA_KV_HEADS, HEAD_DIM)
    assert cache_swa_kv.shape[2] == SPAN
    mix_s, _, swa_kv_s = _decode_roll(q_s, kv_s[:, 0], kv_s[:, 1], cache_swa_kv, 1, *SWA_DECODE_BLOCK,
                                      sinks=swa_sinks[0][:, None])
    mix_s = mix_s.reshape(ns, SWA_Q)
    hp, hs = finish_layer(3, hp, hs, mix_p, mix_s, swa_w_out[0])

    return (hp.reshape(nb, t, D_MODEL), hs.reshape(ns, 1, D_MODEL),
            out_c_p, c_s, out_n_p, n_s, out_m_p, m_s,
            dil_kv_p[0], dil_kv_s[0], dil_kv_p[1], dil_kv_s[1], dil_kv_p[2], dil_kv_s[2],
            s_p[None], s_s, swa_kv_p, swa_kv_s)
```

```python
import functools
import math

import numpy as np
import jax
import jax.numpy as jnp
from jax import lax
from jax.experimental import pallas as pl
from jax.experimental.pallas import tpu as pltpu

F32 = jnp.float32
BF16 = jnp.bfloat16

D_MODEL = 1024
DEPTH = 4
PAST_LEN = 8192
D_FF = 4 * D_MODEL
ALPHA = (2 * DEPTH) ** 0.25
LN_EPS = 1e-5
NORM_EPS = 1e-6
ROPE_THETA = 10000.0

ML_HEADS = 4
ML_DK = 128
ML_DV = 256
ML_QK = ML_HEADS * ML_DK
ML_VD = ML_HEADS * ML_DV
ML_CHUNK = 256

DIL_GROUPS = ((128, 1), (512, 4), (2048, 16))
DIL_HEADS = 8
HEAD_DIM = 64
DIL_GW = 3 * DIL_HEADS * HEAD_DIM
SPAN = 128

HG_HEADS = 8
HG_DK = 128
HG_CHUNK = 128
HG_W = HG_HEADS * HG_DK

SWA_HEADS = 16
SWA_KV_HEADS = 2
SWA_Q = SWA_HEADS * HEAD_DIM
SWA_KV = SWA_KV_HEADS * HEAD_DIM

LANES = 128
NEG = -1e30
VMEM_LIMIT = 48 * 1024 * 1024


def _params(n_axes, vmem=VMEM_LIMIT):
    return pltpu.CompilerParams(dimension_semantics=("arbitrary",) * n_axes, vmem_limit_bytes=vmem)


def _dot(a, b):
    return jnp.dot(a, b, preferred_element_type=F32)


def _dot_nt(a, b):
    return lax.dot_general(a, b, (((1,), (1,)), ((), ())), preferred_element_type=F32)


def _split3(x):
    hi = x.astype(BF16)
    r1 = x - hi.astype(F32)
    mid = r1.astype(BF16)
    lo = (r1 - mid.astype(F32)).astype(BF16)
    return hi, mid, lo


def _dot01_left(a01, x):
    hi, mid, lo = _split3(x)
    return _dot(a01, hi) + _dot(a01, mid) + _dot(a01, lo)


def _dot01_right(x, a01):
    hi, mid, lo = _split3(x)
    return _dot(hi, a01) + _dot(mid, a01) + _dot(lo, a01)


def _log_sigmoid(x):
    return jnp.minimum(x, 0.0) - jnp.log1p(jnp.exp(-jnp.abs(x)))


def _layer_norm(z, g, b):
    mu = jnp.mean(z, axis=-1, keepdims=True)
    zc = z - mu
    var = jnp.mean(zc * zc, axis=-1, keepdims=True)
    return zc * lax.rsqrt(var + LN_EPS) * g + b


def _rms_gate(h, g, gate):
    ms = jnp.mean(h * h, axis=-1, keepdims=True)
    return h * lax.rsqrt(ms + NORM_EPS) * g * jax.nn.sigmoid(gate)


PROJ_CHUNK = 512
RESIDUES = 16


def _row_spec(width, tm, residue_tiles=None):
    if residue_tiles is None:
        return pl.BlockSpec((tm, width), lambda i: (i, 0))
    return pl.BlockSpec((None, RESIDUES, tm // RESIDUES, width),
                        lambda i: (i // residue_tiles, 0, i % residue_tiles, 0))


def _rows(ref):
    v = ref[...]
    return v.reshape(-1, v.shape[-1])


def _proj_kernel(*refs, plan):
    if any(rope for _, _, rope in plan):
        x_ref, w_ref, cos_ref, sin_ref, o_ref = refs
    else:
        x_ref, w_ref, o_ref = refs
    xb = _rows(x_ref).astype(BF16)
    for start, width, rope in plan:
        acc = _dot(xb, w_ref[:, start:start + width])
        if rope:
            reps = width // LANES
            cos = jnp.concatenate([_rows(cos_ref)] * reps, axis=1)
            sin = jnp.concatenate([_rows(sin_ref)] * reps, axis=1)
            lane = lax.broadcasted_iota(jnp.int32, acc.shape, 1)
            first_half = (lane & (HEAD_DIM - 1)) < HEAD_DIM // 2
            partner = jnp.where(first_half, pltpu.roll(acc, width - HEAD_DIM // 2, 1),
                                pltpu.roll(acc, HEAD_DIM // 2, 1))
            acc = acc * cos + partner * sin
        o_ref[..., start:start + width] = acc.reshape(o_ref.shape[:-1] + (width,))


def _proj(x, w_bf, tm, rope_cols=(), tables=None, by_residue=False):
    k, n = w_bf.shape
    plan = []
    start = 0
    while start < n:
        rope = any(lo <= start < hi for lo, hi in rope_cols)
        limit = min([hi for lo, hi in rope_cols if lo <= start < hi] +
                    [lo for lo, hi in rope_cols if lo > start] + [n])
        width = min(PROJ_CHUNK, limit - start)
        plan.append((start, width, rope))
        start += width
    if by_residue:
        seqs, _, per_residue, _ = x.shape
        tiles = per_residue * RESIDUES // tm
        steps = seqs * tiles
        tab = pl.BlockSpec((RESIDUES, tm // RESIDUES, LANES), lambda i: (0, i % tiles, 0))
    else:
        tiles = None
        steps = x.shape[0] // tm
        if rope_cols:
            t_blocks = tables[0].shape[0] // tm
            tab = pl.BlockSpec((tm, LANES), lambda i: (i % t_blocks, 0))
    in_specs = [_row_spec(k, tm, tiles),
                pl.BlockSpec((k, n), lambda i: (0, 0), pipeline_mode=pl.Buffered(1))]
    args = [x, w_bf]
    if rope_cols:
        in_specs += [tab, tab]
        args += list(tables)
    return pl.pallas_call(
        functools.partial(_proj_kernel, plan=tuple(plan)),
        grid=(steps,),
        in_specs=in_specs,
        out_specs=_row_spec(n, tm, tiles),
        out_shape=jax.ShapeDtypeStruct(x.shape[:-1] + (n,), F32),
        compiler_params=_params(1),
        name="proj",
    )(*args)


def _rope_tables(pos):
    half = HEAD_DIM // 2
    inv_freq = jnp.power(ROPE_THETA, -jnp.arange(half, dtype=F32) / half)
    ang = pos.astype(F32)[:, None] * inv_freq[None, :]
    cos = jnp.cos(ang)
    sin = jnp.sin(ang)
    cos = jnp.concatenate([cos, cos, cos, cos], axis=1)
    sin = jnp.concatenate([-sin, sin, -sin, sin], axis=1)
    return cos, sin


def _dil_out_ln_kernel(o0, o1, o2, l0, l1, l2, w_ref, x_ref, g_ref, b_ref, y_ref):
    a0, a1, a2 = _rows(l0), _rows(l1), _rows(l2)
    mx = jnp.maximum(jnp.maximum(a0, a1), a2)
    e0, e1, e2 = jnp.exp(a0 - mx), jnp.exp(a1 - mx), jnp.exp(a2 - mx)
    y = (e0 * _rows(o0) + e1 * _rows(o1) + e2 * _rows(o2)) / (e0 + e1 + e2)
    z = ALPHA * _rows(x_ref) + _dot(y.astype(BF16), w_ref[...])
    y_ref[...] = _layer_norm(z, g_ref[...], b_ref[...]).reshape(y_ref.shape)


def _dil_out_ln(outs, lses, w_bf, x, g, b, tm, by_residue=False):
    kin = outs[0].shape[-1]
    if by_residue:
        tiles = x.shape[2] * RESIDUES // tm
        steps = x.shape[0] * tiles
    else:
        tiles = None
        steps = x.shape[0] // tm
    const = lambda i: (0, 0)
    return pl.pallas_call(
        _dil_out_ln_kernel,
        grid=(steps,),
        in_specs=[_row_spec(kin, tm, tiles)] * 6 + [
            pl.BlockSpec((kin, D_MODEL), const), _row_spec(D_MODEL, tm, tiles),
            pl.BlockSpec((1, D_MODEL), const), pl.BlockSpec((1, D_MODEL), const)],
        out_specs=_row_spec(D_MODEL, tm, tiles),
        out_shape=jax.ShapeDtypeStruct(x.shape, F32),
        compiler_params=_params(1),
        name="dil_out_ln",
    )(*outs, *lses, w_bf, x, g, b)


FF_CHUNK = 1024


def _tail_kernel(*refs, with_mix, shuffle):
    refs = list(refs)
    scratch = refs.pop() if shuffle else None
    y_ref = refs.pop()
    if with_mix:
        mix_ref, wo_ref, x_ref, g1_ref, b1_ref = refs[:5]
        refs = refs[5:]
        z = ALPHA * _rows(x_ref) + _dot(_rows(mix_ref).astype(BF16), wo_ref[...])
        x = _layer_norm(z, g1_ref[...], b1_ref[...])
    else:
        x = _rows(refs.pop(0))
    w1_ref, w2_ref, g2_ref, b2_ref = refs
    xb = x.astype(BF16)
    acc = jnp.zeros(x.shape, F32)
    for c in range(D_FF // FF_CHUNK):
        a = _dot(xb, w1_ref[:, c * FF_CHUNK:(c + 1) * FF_CHUNK])
        a = jnp.square(jnp.maximum(a, 0.0)).astype(BF16)
        acc = acc + _dot(a, w2_ref[c * FF_CHUNK:(c + 1) * FF_CHUNK, :])
    y = _layer_norm(ALPHA * x + acc, g2_ref[...], b2_ref[...])
    per = y.shape[0] // RESIDUES
    lane_tiles = [slice(c * LANES, (c + 1) * LANES) for c in range(D_MODEL // LANES)]
    if shuffle == "to_residue":
        for c, cols in enumerate(lane_tiles):
            scratch[c] = y[:, cols]
        for r in range(RESIDUES):
            for c, cols in enumerate(lane_tiles):
                y_ref[r, :, cols] = scratch[c, pl.ds(r, per, stride=RESIDUES), :]
    elif shuffle == "to_natural":
        for r in range(RESIDUES):
            for c, cols in enumerate(lane_tiles):
                scratch[c, pl.ds(r, per, stride=RESIDUES), :] = y[r * per:(r + 1) * per, cols]
        for c, cols in enumerate(lane_tiles):
            y_ref[:, cols] = scratch[c]
    else:
        y_ref[...] = y


def _layer_tail(x, w1_bf, w2_bf, g2, b2, tm, mix=None, w_out_bf=None, g1=None, b1=None,
                shuffle=None, seq_len=None):
    const = lambda i: (0, 0)
    resident = pl.Buffered(1)
    in_tiles = out_tiles = None
    out_shape = x.shape
    if shuffle == "to_natural":
        in_tiles = x.shape[2] * RESIDUES // tm
        steps = x.shape[0] * in_tiles
        out_shape = (x.shape[0] * x.shape[1] * x.shape[2], D_MODEL)
    else:
        steps = x.shape[0] // tm
        if shuffle == "to_residue":
            out_tiles = seq_len // tm
            out_shape = (x.shape[0] // seq_len, RESIDUES, seq_len // RESIDUES, D_MODEL)
    in_specs, args = [], []
    if mix is not None:
        kin = mix.shape[-1]
        in_specs += [_row_spec(kin, tm, in_tiles),
                     pl.BlockSpec((kin, D_MODEL), const, pipeline_mode=resident)]
        args += [mix, w_out_bf]
    in_specs.append(_row_spec(D_MODEL, tm, in_tiles))
    args.append(x)
    if mix is not None:
        in_specs += [pl.BlockSpec((1, D_MODEL), const)] * 2
        args += [g1, b1]
    in_specs += [pl.BlockSpec((D_MODEL, D_FF), const, pipeline_mode=resident),
                 pl.BlockSpec((D_FF, D_MODEL), const, pipeline_mode=resident),
                 pl.BlockSpec((1, D_MODEL), const), pl.BlockSpec((1, D_MODEL), const)]
    args += [w1_bf, w2_bf, g2, b2]
    return pl.pallas_call(
        functools.partial(_tail_kernel, with_mix=mix is not None, shuffle=shuffle),
        grid=(steps,),
        in_specs=in_specs,
        out_specs=_row_spec(D_MODEL, tm, out_tiles),
        out_shape=jax.ShapeDtypeStruct(out_shape, F32),
        scratch_shapes=[pltpu.VMEM((D_MODEL // LANES, tm, LANES), F32)] if shuffle else [],
        compiler_params=_params(1),
        name="layer_tail",
    )(*args)


def _mlstm_prompt_kernel(q_ref, k_ref, v_ref, og_ref, gc_ref, gr_ref, brow_ref, bcol_ref, g_ref,
                         h_ref, c_ref, n_ref, m_ref, *, nb, chunk):
    step = pl.program_id(0)

    @pl.when(step == 0)
    def _():
        c_ref[...] = jnp.zeros(c_ref.shape, F32)
        n_ref[...] = jnp.zeros(n_ref.shape, F32)
        m_ref[...] = jnp.zeros(m_ref.shape, F32)

    row = lax.broadcasted_iota(jnp.int32, (chunk, chunk), 0)
    col = lax.broadcasted_iota(jnp.int32, (chunk, chunk), 1)
    causal = col <= row
    tril = jnp.where(causal, 1.0, 0.0).astype(BF16)
    triu = jnp.where(row <= col, 1.0, 0.0).astype(BF16)
    scale = ML_DK ** -0.5
    last = chunk - 1

    for b in range(nb):
        gc = gc_ref[b] + brow_ref[...]
        ig_cols = gc[:, :LANES]
        b_cols = _dot01_left(tril, _log_sigmoid(gc[:, LANES:]))
        gr = gr_ref[b] + bcol_ref[...]
        b_rows = _dot01_right(_log_sigmoid(gr), triu)
        for h in range(ML_HEADS):
            bc = b_cols[:, h:h + 1]
            igc = ig_cols[:, h:h + 1]
            br = b_rows[ML_HEADS + h:ML_HEADS + h + 1, :]
            igr = gr[h:h + 1, :]
            m_prev = m_ref[b, h][:, 0:1]
            log_intra = jnp.where(causal, bc - br + igr, NEG)
            log_prev = bc + m_prev
            m_t = jnp.maximum(log_prev, jnp.max(log_intra, axis=1, keepdims=True))
            w_intra = jnp.exp(log_intra - m_t)
            w_prev = jnp.exp(log_prev - m_t)
            q = q_ref[b, :, h * ML_DK:(h + 1) * ML_DK]
            k = k_ref[b, :, h * ML_DK:(h + 1) * ML_DK] * scale
            v = v_ref[b, :, h * ML_DV:(h + 1) * ML_DV]
            qb, kb, vb = q.astype(BF16), k.astype(BF16), v.astype(BF16)
            a = _dot_nt(qb, kb) * w_intra
            c_old = c_ref[b, h]
            n_old = n_ref[b, h]
            num = _dot(a.astype(BF16), vb) + w_prev * _dot(qb, c_old.astype(BF16))
            den = jnp.sum(a, axis=1, keepdims=True) + w_prev * jnp.sum(q * n_old, axis=1, keepdims=True)
            hh = num / jnp.maximum(jnp.abs(den), jnp.exp(-m_t))
            m_new = m_t[last:last + 1, :]
            b_end = bc[last:last + 1, :]
            w_end = jnp.exp(b_end - bc + igc - m_new)
            decay = jnp.exp(b_end + m_prev - m_new)
            kw = k * w_end
            c_ref[b, h] = decay * c_old + _dot(kw.T.astype(BF16), vb)
            n_ref[b, h] = decay * n_old + jnp.sum(kw, axis=0, keepdims=True)
            m_ref[b, h] = jnp.broadcast_to(m_new, (1, LANES))
            sl = slice(h * ML_DV, (h + 1) * ML_DV)
            h_ref[b, :, sl] = _rms_gate(hh, g_ref[:, sl], og_ref[b, :, sl])


def _mlstm_prompt(p, gates, b_gates, norm_g, nb, t):
    chunk = ML_CHUNK
    g8 = jnp.concatenate([gates[..., :ML_HEADS], gates[..., LANES:LANES + ML_HEADS]], axis=-1)
    g_rows = jnp.transpose(g8, (0, 2, 1))
    zeros = jnp.zeros((LANES - ML_HEADS,), F32)
    b_row = jnp.concatenate([b_gates[:ML_HEADS], zeros, b_gates[ML_HEADS:], zeros])[None, :]
    b_col = b_gates[:, None]
    kernel = functools.partial(_mlstm_prompt_kernel, nb=nb, chunk=chunk)
    const2 = lambda c: (0, 0)
    return pl.pallas_call(
        kernel,
        grid=(t // chunk,),
        in_specs=[pl.BlockSpec((nb, chunk, ML_QK), lambda c: (0, c, 0)),
                  pl.BlockSpec((nb, chunk, ML_QK), lambda c: (0, c, 1)),
                  pl.BlockSpec((nb, chunk, ML_VD), lambda c: (0, c, 1)),
                  pl.BlockSpec((nb, chunk, ML_VD), lambda c: (0, c, 2)),
                  pl.BlockSpec((nb, chunk, 2 * LANES), lambda c: (0, c, 0)),
                  pl.BlockSpec((nb, 2 * ML_HEADS, chunk), lambda c: (0, 0, c)),
                  pl.BlockSpec((1, 2 * LANES), const2),
                  pl.BlockSpec((2 * ML_HEADS, 1), const2),
                  pl.BlockSpec((1, ML_VD), const2)],
        out_specs=[pl.BlockSpec((nb, chunk, ML_VD), lambda c: (0, c, 0)),
                   pl.BlockSpec((nb, ML_HEADS, ML_DK, ML_DV), lambda c: (0, 0, 0, 0)),
                   pl.BlockSpec((nb, ML_HEADS, 1, ML_DK), lambda c: (0, 0, 0, 0)),
                   pl.BlockSpec((nb, ML_HEADS, 1, LANES), lambda c: (0, 0, 0, 0))],
        out_shape=[jax.ShapeDtypeStruct((nb, t, ML_VD), F32),
                   jax.ShapeDtypeStruct((nb, ML_HEADS, ML_DK, ML_DV), F32),
                   jax.ShapeDtypeStruct((nb, ML_HEADS, 1, ML_DK), F32),
                   jax.ShapeDtypeStruct((nb, ML_HEADS, 1, LANES), F32)],
        compiler_params=_params(1),
        name="mlstm_prompt",
    )(p, p, p, p, gates, g_rows, b_row, b_col, norm_g)


STATE_BLOCK = 4


def _column(row_vec, eye):
    return jnp.sum(jnp.where(eye, row_vec, 0.0), axis=1, keepdims=True)


def _mlstm_sample_kernel(p_ref, gate_ref, bias_ref, g_ref, c_ref, n_ref, m_ref,
                         h_ref, c_out, n_out, m_out):
    eye = (lax.broadcasted_iota(jnp.int32, (ML_DK, ML_DK), 0)
           == lax.broadcasted_iota(jnp.int32, (ML_DK, ML_DK), 1))
    for s in range(p_ref.shape[0]):
        _mlstm_sample_one(s, eye, p_ref, gate_ref, bias_ref, g_ref, c_ref, n_ref, m_ref,
                          h_ref, c_out, n_out, m_out)


def _mlstm_sample_one(s, eye, p_ref, gate_ref, bias_ref, g_ref, c_ref, n_ref, m_ref,
                      h_ref, c_out, n_out, m_out):
    gates = gate_ref[s] + bias_ref[...]
    ig = gates[:, :LANES]
    lf = _log_sigmoid(gates[:, LANES:])
    m_old = m_ref[s]
    log_prev = lf + m_old
    m_t = jnp.maximum(log_prev, ig)
    w_i_all = jnp.exp(ig - m_t)
    w_p_all = jnp.exp(log_prev - m_t)
    floor_all = jnp.exp(-m_t)
    m_out[s] = m_t
    scale = ML_DK ** -0.5
    for h in range(ML_HEADS):
        q = p_ref[s, :, h * ML_DK:(h + 1) * ML_DK]
        k = p_ref[s, :, ML_QK + h * ML_DK:ML_QK + (h + 1) * ML_DK] * scale
        v = p_ref[s, :, 2 * ML_QK + h * ML_DV:2 * ML_QK + (h + 1) * ML_DV]
        og = p_ref[s, :, 2 * ML_QK + ML_VD + h * ML_DV:2 * ML_QK + ML_VD + (h + 1) * ML_DV]
        w_i = w_i_all[:, h:h + 1]
        w_p = w_p_all[:, h:h + 1]
        c_old = c_ref[0, s, h]
        n_old = n_ref[0, s, h:h + 1, :]
        q_col = _column(q, eye)
        k_col = _column(k, eye)
        a = jnp.sum(q * k, axis=1, keepdims=True) * w_i
        num = a * v + w_p * jnp.sum(c_old * q_col, axis=0, keepdims=True)
        den = a + w_p * jnp.sum(q * n_old, axis=1, keepdims=True)
        hh = num / jnp.maximum(jnp.abs(den), floor_all[:, h:h + 1])
        c_out[0, s, h] = w_p * c_old + (w_i * k_col) * v
        n_out[0, s, h:h + 1, :] = w_p * n_old + w_i * k
        sl = slice(h * ML_DV, (h + 1) * ML_DV)
        h_ref[s, :, sl] = _rms_gate(hh, g_ref[:, sl], og)


def _mlstm_sample(p, gates, b_gates, norm_g, c0, n0, m0):
    n = p.shape[0]
    zeros = jnp.zeros((LANES - ML_HEADS,), F32)
    b_row = jnp.concatenate([b_gates[:ML_HEADS], zeros, b_gates[ML_HEADS:], zeros])[None, :]
    m_pad = jnp.pad(m0[0], ((0, 0), (0, LANES - ML_HEADS)))[:, None, :]
    row3 = lambda i: (i, 0, 0)
    const2 = lambda i: (0, 0)
    sb = STATE_BLOCK
    h, c1, n1, m1 = pl.pallas_call(
        _mlstm_sample_kernel,
        grid=(n // sb,),
        in_specs=[pl.BlockSpec((sb, 1, p.shape[1]), row3),
                  pl.BlockSpec((sb, 1, 2 * LANES), row3),
                  pl.BlockSpec((1, 2 * LANES), const2),
                  pl.BlockSpec((1, ML_VD), const2),
                  pl.BlockSpec((1, sb, ML_HEADS, ML_DK, ML_DV), lambda i: (0, i, 0, 0, 0)),
                  pl.BlockSpec((1, sb, ML_HEADS, ML_DK), lambda i: (0, i, 0, 0)),
                  pl.BlockSpec((sb, 1, LANES), row3)],
        out_specs=[pl.BlockSpec((sb, 1, ML_VD), row3),
                   pl.BlockSpec((1, sb, ML_HEADS, ML_DK, ML_DV), lambda i: (0, i, 0, 0, 0)),
                   pl.BlockSpec((1, sb, ML_HEADS, ML_DK), lambda i: (0, i, 0, 0)),
                   pl.BlockSpec((sb, 1, LANES), row3)],
        out_shape=[jax.ShapeDtypeStruct((n, 1, ML_VD), F32),
                   jax.ShapeDtypeStruct(c0.shape, F32),
                   jax.ShapeDtypeStruct(n0.shape, F32),
                   jax.ShapeDtypeStruct((n, 1, LANES), F32)],
        compiler_params=_params(1),
        name="mlstm_sample",
    )(p[:, None, :], gates[:, None, :], b_row, norm_g, c0, n0, m_pad)
    return h[:, 0, :], c1, n1, m1[:, 0, :ML_HEADS][None]


HG_LEVELS = (1, 2, 4, 8, 16, 32, 64)


def _bcast_rows(x, group, which):
    n, w = x.shape
    x3 = x.reshape(n // group, group, w)
    return jnp.broadcast_to(x3[:, which:which + 1, :], x3.shape).reshape(n, w)


def _level_log_decay(level, lf, b_incl, t_idx, chunk):
    odd = ((t_idx >> int(math.log2(level))) & 1) == 1
    if level >= 8:
        since = b_incl - _bcast_rows(b_incl - lf, level, 0)
        until = _bcast_rows(b_incl, level, level - 1) - b_incl
        return jnp.where(odd, since, until)
    pos = t_idx & (level - 1)
    since = lf
    until = jnp.zeros_like(lf)
    for i in range(1, level):
        since = since + jnp.where(pos >= i, pltpu.roll(lf, i, 0), 0.0)
        until = until + jnp.where(pos <= level - 1 - i, pltpu.roll(lf, chunk - i, 0), 0.0)
    return jnp.where(odd, since, until)


def _hgrn_prompt_kernel(q_ref, f_ref, i_ref, g_ref, bf_ref, lb_ref, ng_ref, o_ref, s_ref, st_ref,
                        *, nb, chunk, layer_idx):
    step = pl.program_id(0)

    @pl.when(step == 0)
    def _():
        st_ref[...] = jnp.zeros(st_ref.shape, F32)

    logits = [lb_ref[i:i + 1, :] for i in range(DEPTH)]
    mx = functools.reduce(jnp.maximum, logits)
    es = [jnp.exp(l - mx) for l in logits]
    tot = functools.reduce(lambda a, c: a + c, es)
    cum = []
    for e in es:
        cum.append(e / tot if not cum else cum[-1] + e / tot)
    lb = cum[layer_idx] - cum[0]

    row = lax.broadcasted_iota(jnp.int32, (chunk, chunk), 0)
    col = lax.broadcasted_iota(jnp.int32, (chunk, chunk), 1)
    tril = jnp.where(col <= row, 1.0, 0.0).astype(BF16)
    eye = row == col
    level_masks = []
    for level in HG_LEVELS:
        sh = int(math.log2(level))
        u, w = row >> sh, col >> sh
        level_masks.append(((u & 1) * 4096 + (u - w)) == 4097)
    t_idx = lax.broadcasted_iota(jnp.int32, (chunk, HG_DK), 0)
    last = chunk - 1

    for b in range(nb):
        fg_all = lb + (1.0 - lb) * jax.nn.sigmoid(f_ref[b] + bf_ref[...])
        lf_all = jnp.log(fg_all)
        b_all = _dot01_left(tril, lf_all)
        for h in range(HG_HEADS):
            sl = slice(h * HG_DK, (h + 1) * HG_DK)
            qx = q_ref[b, :, sl]
            q = qx * jax.nn.sigmoid(qx)
            k = 1.0 - fg_all[:, sl]
            lf = lf_all[:, sl]
            bi = b_all[:, sl]
            v = i_ref[b, :, sl]
            vb = v.astype(BF16)
            a = jnp.where(eye, _dot_nt(q.astype(BF16), k.astype(BF16)), 0.0)
            for level, mask in zip(HG_LEVELS, level_masks):
                e = jnp.exp(_level_log_decay(level, lf, bi, t_idx, chunk))
                a = a + jnp.where(mask, _dot_nt((q * e).astype(BF16), (k * e).astype(BF16)), 0.0)
            st = st_ref[b, h]
            o = _dot(a.astype(BF16), vb) + _dot_nt((q * jnp.exp(bi)).astype(BF16), st.astype(BF16))
            b_end = bi[last:last + 1, :]
            kd = k * jnp.exp(b_end - bi)
            st_ref[b, h] = st * jnp.exp(b_end) + _dot(v.T.astype(BF16), kd.astype(BF16))
            o_ref[b, :, sl] = _rms_gate(o, ng_ref[:, sl], g_ref[b, :, sl])

    @pl.when(step == pl.num_programs(0) - 1)
    def _():
        for b in range(nb):
            for h in range(HG_HEADS):
                s_ref[b, h] = st_ref[b, h].T


def _hgrn_prompt(p, b_f, lb_logits, norm_g, nb, t, layer_idx):
    chunk = HG_CHUNK
    kernel = functools.partial(_hgrn_prompt_kernel, nb=nb, chunk=chunk, layer_idx=layer_idx)
    const2 = lambda c: (0, 0)
    blk = lambda j: pl.BlockSpec((nb, chunk, HG_W), lambda c: (0, c, j))
    return pl.pallas_call(
        kernel,
        grid=(t // chunk,),
        in_specs=[blk(0), blk(1), blk(2), blk(3),
                  pl.BlockSpec((1, HG_W), const2), pl.BlockSpec((DEPTH, HG_W), const2),
                  pl.BlockSpec((1, HG_W), const2)],
        out_specs=[pl.BlockSpec((nb, chunk, HG_W), lambda c: (0, c, 0)),
                   pl.BlockSpec((nb, HG_HEADS, HG_DK, HG_DK), lambda c: (0, 0, 0, 0))],
        out_shape=[jax.ShapeDtypeStruct((nb, t, HG_W), F32),
                   jax.ShapeDtypeStruct((nb, HG_HEADS, HG_DK, HG_DK), F32)],
        scratch_shapes=[pltpu.VMEM((nb, HG_HEADS, HG_DK, HG_DK), F32)],
        compiler_params=_params(1),
        name="hgrn_prompt",
    )(p, p, p, p, b_f, lb_logits, norm_g)


def _hgrn_sample_kernel(p_ref, bf_ref, lb_ref, ng_ref, s_ref, o_ref, s_out, *, layer_idx):
    logits = [lb_ref[i:i + 1, :] for i in range(DEPTH)]
    mx = functools.reduce(jnp.maximum, logits)
    es = [jnp.exp(l - mx) for l in logits]
    tot = functools.reduce(lambda a, c: a + c, es)
    cum = []
    for e in es:
        cum.append(e / tot if not cum else cum[-1] + e / tot)
    lb = cum[layer_idx] - cum[0]
    eye = (lax.broadcasted_iota(jnp.int32, (HG_DK, HG_DK), 0)
           == lax.broadcasted_iota(jnp.int32, (HG_DK, HG_DK), 1))
    for s in range(p_ref.shape[0]):
        qx = p_ref[s, :, 0:HG_W]
        q_all = qx * jax.nn.sigmoid(qx)
        fg_all = lb + (1.0 - lb) * jax.nn.sigmoid(p_ref[s, :, HG_W:2 * HG_W] + bf_ref[...])
        for h in range(HG_HEADS):
            sl = slice(h * HG_DK, (h + 1) * HG_DK)
            q = q_all[:, sl]
            fg = fg_all[:, sl]
            k = 1.0 - fg
            v = p_ref[s, :, 2 * HG_W + h * HG_DK:2 * HG_W + (h + 1) * HG_DK]
            gate = p_ref[s, :, 3 * HG_W + h * HG_DK:3 * HG_W + (h + 1) * HG_DK]
            s_old = s_ref[0, s, h]
            decay = jnp.exp(jnp.log(fg))
            a = jnp.sum(q * k, axis=1, keepdims=True)
            o = a * v + jnp.sum(s_old * _column(q * decay, eye), axis=0, keepdims=True)
            s_out[0, s, h] = _column(decay, eye) * s_old + _column(k, eye) * v
            o_ref[s, :, sl] = _rms_gate(o, ng_ref[:, sl], gate)


def _hgrn_sample(p, b_f, lb_logits, norm_g, s0, layer_idx):
    n = p.shape[0]
    row3 = lambda i: (i, 0, 0)
    const2 = lambda i: (0, 0)
    sb = STATE_BLOCK
    state = pl.BlockSpec((1, sb, HG_HEADS, HG_DK, HG_DK), lambda i: (0, i, 0, 0, 0))
    o, s1 = pl.pallas_call(
        functools.partial(_hgrn_sample_kernel, layer_idx=layer_idx),
        grid=(n // sb,),
        in_specs=[pl.BlockSpec((sb, 1, p.shape[1]), row3), pl.BlockSpec((1, HG_W), const2),
                  pl.BlockSpec((DEPTH, HG_W), const2), pl.BlockSpec((1, HG_W), const2), state],
        out_specs=[pl.BlockSpec((sb, 1, HG_W), row3), state],
        out_shape=[jax.ShapeDtypeStruct((n, 1, HG_W), F32), jax.ShapeDtypeStruct(s0.shape, F32)],
        compiler_params=_params(1),
        name="hgrn_sample",
    )(p[:, None, :], b_f, lb_logits, norm_g, s0)
    return o[:, 0, :], s1


def _band_kernel(*refs, n_heads, group, fold, n_blocks, with_sinks, with_lse):
    refs = list(refs)
    sink_ref = refs.pop(0) if with_sinks else None
    q_ref, ko_ref, kp_ref, vo_ref, vp_ref, o_ref = refs[:6]
    lse_ref = refs[6] if with_lse else None
    step = pl.program_id(2)
    per = SPAN // fold
    shift = int(math.log2(per))

    def local_pos(idx):
        return ((idx & (per - 1)) * fold) + (idx >> shift)

    qi = lax.broadcasted_iota(jnp.int32, (SPAN, 2 * SPAN), 0)
    ki = lax.broadcasted_iota(jnp.int32, (SPAN, 2 * SPAN), 1)
    qpos = SPAN + local_pos(qi)
    kpos = (ki & SPAN) + local_pos(ki & (SPAN - 1))
    band = (kpos <= qpos) & (kpos >= qpos - SPAN)
    first = jnp.where(step > 0, 0, SPAN)
    band_first = band & (kpos >= first)
    lo_q = lax.broadcasted_iota(jnp.int32, (SPAN, LANES), 1) < HEAD_DIM
    lo_kv = lax.broadcasted_iota(jnp.int32, (2 * SPAN, LANES), 1) < HEAD_DIM
    scale = HEAD_DIM ** -0.5

    def block_rows(ref, blk):
        v = ref[:, blk * per:(blk + 1) * per, :]
        return v.reshape(SPAN, v.shape[-1])

    for blk in range(n_blocks):
        valid = band_first if blk == 0 else band
        k_prev = _rows(kp_ref) if blk == 0 else block_rows(ko_ref, blk - 1)
        v_prev = _rows(vp_ref) if blk == 0 else block_rows(vo_ref, blk - 1)
        kcat = jnp.concatenate([k_prev, block_rows(ko_ref, blk)], axis=0)
        vcat = jnp.concatenate([v_prev, block_rows(vo_ref, blk)], axis=0)
        q_blk = block_rows(q_ref, blk)
        tiles = {}

        def kv_tile(name, src, kv_head, want_hi):
            key = (name, kv_head, want_hi)
            if key not in tiles:
                t = src[:, (kv_head // 2) * LANES:(kv_head // 2 + 1) * LANES]
                if (kv_head % 2 == 1) != want_hi:
                    t = pltpu.roll(t, HEAD_DIM, 1)
                tiles[key] = t
            return tiles[key]

        def softmax_parts(s, head):
            s = jnp.where(valid, s, NEG)
            m = jnp.max(s, axis=1, keepdims=True)
            if with_sinks:
                m = jnp.maximum(m, sink_ref[head])
            p = jnp.exp(s - m)
            l = jnp.sum(p, axis=1, keepdims=True)
            if with_sinks:
                l = l + jnp.exp(sink_ref[head] - m)
            return p.astype(BF16), m, l

        for j in range(n_heads // 2):
            sl = slice(j * LANES, (j + 1) * LANES)
            qpair = q_blk[:, sl] * scale
            h_lo, h_hi = 2 * j, 2 * j + 1
            q_lo = jnp.where(lo_q, qpair, 0.0).astype(BF16)
            q_hi = jnp.where(lo_q, 0.0, qpair).astype(BF16)
            s_lo = _dot_nt(q_lo, kv_tile("k", kcat, h_lo // group, False).astype(BF16))
            s_hi = _dot_nt(q_hi, kv_tile("k", kcat, h_hi // group, True).astype(BF16))
            p_lo, m_lo, l_lo = softmax_parts(s_lo, h_lo)
            p_hi, m_hi, l_hi = softmax_parts(s_hi, h_hi)
            v_lo = jnp.where(lo_kv, kv_tile("v", vcat, h_lo // group, False), 0.0).astype(BF16)
            v_hi = jnp.where(lo_kv, 0.0, kv_tile("v", vcat, h_hi // group, True)).astype(BF16)
            o = (_dot(p_lo, v_lo) + _dot(p_hi, v_hi)) / jnp.where(lo_q, l_lo, l_hi)
            o_ref[:, blk * per:(blk + 1) * per, sl] = o.reshape(fold, per, LANES)
            if with_lse:
                lse = jnp.where(lo_q, m_lo + jnp.log(l_lo), m_hi + jnp.log(l_hi))
                lse_ref[:, blk * per:(blk + 1) * per, sl] = lse.reshape(fold, per, LANES)


BAND_BLOCKS = 4


def _band_call(name, arrays, sinks, fold, classes, rows, q_col, k_col, v_col, q_width, kv_width,
               n_heads, group, with_lse):
    seqs = arrays.shape[0]
    per = SPAN // fold
    steps = rows // (per * BAND_BLOCKS)
    own = lambda width, col: pl.BlockSpec((None, fold, None, per * BAND_BLOCKS, width),
                                          lambda b, c, n: (b, 0, c, n, col))
    prev = lambda width, col: pl.BlockSpec(
        (None, fold, None, per, width),
        lambda b, c, n: (b, 0, c, jnp.maximum(n * BAND_BLOCKS - 1, 0), col))
    out = pl.BlockSpec((None, fold, None, per * BAND_BLOCKS, q_width), lambda b, c, n: (b, 0, c, n, 0))
    shape = jax.ShapeDtypeStruct((seqs, fold, classes, rows, q_width), F32)
    in_specs = [own(q_width, q_col), own(kv_width, k_col), prev(kv_width, k_col),
                own(kv_width, v_col), prev(kv_width, v_col)]
    args = [arrays] * 5
    if sinks is not None:
        in_specs = [pl.BlockSpec(memory_space=pltpu.SMEM)] + in_specs
        args = [sinks] + args
    return pl.pallas_call(
        functools.partial(_band_kernel, n_heads=n_heads, group=group, fold=fold, n_blocks=BAND_BLOCKS,
                          with_sinks=sinks is not None, with_lse=with_lse),
        grid=(seqs, classes, steps),
        in_specs=in_specs,
        out_specs=[out, out] if with_lse else out,
        out_shape=[shape, shape] if with_lse else shape,
        compiler_params=_params(3),
        name=name,
    )(*args)


def _dil_prompt_group(p, g):
    _, dil = DIL_GROUPS[g]
    seqs, _, rows, width = p.shape
    fold = RESIDUES // dil
    w = DIL_HEADS * HEAD_DIM
    o, lse = _band_call("dil_prompt_g%d" % g, p.reshape(seqs, fold, dil, rows, width), None, fold, dil, rows,
                        3 * g, 3 * g + 1, 3 * g + 2, w, w, DIL_HEADS, 1, True)
    return o.reshape(seqs, RESIDUES, rows, w), lse.reshape(seqs, RESIDUES, rows, w)


def _swa_prompt(p, sinks, nb, t):
    kcol = SWA_Q // SWA_KV
    o = _band_call("swa_prompt", p.reshape(nb, 1, 1, t, p.shape[-1]), sinks, 1, 1, t,
                   0, kcol, kcol + 1, SWA_Q, SWA_KV, SWA_HEADS, SWA_HEADS // SWA_KV_HEADS, False)
    return o.reshape(nb * t, SWA_Q)


def _decode_roll_kernel(*refs, nb, hb, group, length, dil, with_sinks):
    refs = list(refs)
    sink_ref = refs.pop(0) if with_sinks else None
    q_ref, kn_ref, vn_ref, c_ref, o_ref, lse_ref, cout_ref = refs
    h0 = pl.program_id(1) * hb
    eye = (lax.broadcasted_iota(jnp.int32, (HEAD_DIM, HEAD_DIM), 0)
           == lax.broadcasted_iota(jnp.int32, (HEAD_DIM, HEAD_DIM), 1))
    pos = lax.broadcasted_iota(jnp.int32, (1, length), 1)
    attended = (pos & (dil - 1)) == 0
    is_last = lax.broadcasted_iota(jnp.int32, (HEAD_DIM, length), 1) == length - 1
    scale = HEAD_DIM ** -0.5
    for i in range(nb if group == 1 else 0):
        kt = c_ref[0, i, 0]
        vt = c_ref[0, i, 1]
        row = lambda ref: jnp.stack([ref[i, pl.ds(h0 + j, 1), :] for j in range(hb)])
        q, kn, vn = row(q_ref) * scale, row(kn_ref), row(vn_ref)
        column = lambda r: jnp.sum(jnp.where(eye[None], r, 0.0), axis=2, keepdims=True)
        s = jnp.sum(kt * column(q), axis=1, keepdims=True)
        s = jnp.where(attended[None], s, NEG)
        s_new = jnp.sum(q * kn, axis=2, keepdims=True)
        m = jnp.maximum(jnp.max(s, axis=2, keepdims=True), s_new)
        p = jnp.exp(s - m)
        p_new = jnp.exp(s_new - m)
        l = jnp.sum(p, axis=2, keepdims=True) + p_new
        acc_col = jnp.sum(vt * p, axis=2, keepdims=True)
        acc = jnp.sum(jnp.where(eye[None], acc_col, 0.0), axis=1, keepdims=True)
        out = (acc + p_new * vn) / l
        lse = jnp.broadcast_to(m + jnp.log(l), out.shape)
        for j in range(hb):
            o_ref[i, pl.ds(h0 + j, 1), :] = out[j]
            lse_ref[i, pl.ds(h0 + j, 1), :] = lse[j]
        cout_ref[0, i, 0] = jnp.where(is_last[None], column(kn), pltpu.roll(kt, length - 1, 2))
        cout_ref[0, i, 1] = jnp.where(is_last[None], column(vn), pltpu.roll(vt, length - 1, 2))
    for i in range(nb if group > 1 else 0):
        for j in range(hb):
            kt = c_ref[0, i, 0, j]
            vt = c_ref[0, i, 1, j]
            kn = kn_ref[i, pl.ds(h0 + j, 1), :]
            vn = vn_ref[i, pl.ds(h0 + j, 1), :]
            rows = pl.ds((h0 + j) * group, group)
            q = q_ref[i, rows, :] * scale
            s = _dot(q.astype(BF16), kt.astype(BF16))
            s = jnp.where(attended, s, NEG)
            s_new = jnp.sum(q * kn, axis=1, keepdims=True)
            m = jnp.maximum(jnp.max(s, axis=1, keepdims=True), s_new)
            if with_sinks:
                sink = sink_ref[rows, :]
                m = jnp.maximum(m, sink)
            p = jnp.exp(s - m)
            p_new = jnp.exp(s_new - m)
            l = jnp.sum(p, axis=1, keepdims=True) + p_new
            if with_sinks:
                l = l + jnp.exp(sink - m)
            acc = _dot_nt(p.astype(BF16), vt.astype(BF16))
            o_ref[i, rows, :] = (acc + p_new * vn) / l
            lse_ref[i, rows, :] = jnp.broadcast_to(m + jnp.log(l), (group, HEAD_DIM))
            cout_ref[0, i, 0, j] = jnp.where(is_last, _column(kn, eye), pltpu.roll(kt, length - 1, 1))
            cout_ref[0, i, 1, j] = jnp.where(is_last, _column(vn, eye), pltpu.roll(vt, length - 1, 1))


def _decode_roll(q, k_new, v_new, cache, dil, nb, hb, sinks=None):
    _, n, length, _, kv_heads, d = cache.shape
    heads = q.shape[1]
    group = heads // kv_heads
    cache_t = jnp.transpose(cache, (0, 1, 3, 4, 5, 2))
    cblock = pl.BlockSpec((1, nb, 2, hb, d, length), lambda i, h: (0, i, 0, h, 0, 0))
    per_seq = lambda width: pl.BlockSpec((nb, width, d), lambda i, h: (i, 0, 0))
    in_specs = [per_seq(heads), per_seq(kv_heads), per_seq(kv_heads), cblock]
    args = [q, k_new, v_new, cache_t]
    if sinks is not None:
        in_specs = [pl.BlockSpec((heads, 1), lambda i, h: (0, 0))] + in_specs
        args = [sinks] + args
    o, lse, new_t = pl.pallas_call(
        functools.partial(_decode_roll_kernel, nb=nb, hb=hb, group=group, length=length, dil=dil,
                          with_sinks=sinks is not None),
        grid=(n // nb, kv_heads // hb),
        in_specs=in_specs,
        out_specs=[per_seq(heads), per_seq(heads), cblock],
        out_shape=[jax.ShapeDtypeStruct(q.shape, F32), jax.ShapeDtypeStruct(q.shape, F32),
                   jax.ShapeDtypeStruct(cache_t.shape, F32)],
        compiler_params=_params(2),
        name="decode_roll",
    )(*args)
    return o, lse, jnp.transpose(new_t, (0, 1, 5, 2, 3, 4))


PROMPT_TM = 512
LN_TM = 512
DIL_DECODE_BLOCKS = ((4, 8), (1, 8), (1, 2))
SWA_DECODE_BLOCK = (16, 2)


def _pad_cols(w, width):
    return jnp.pad(w, ((0, 0), (0, width - w.shape[1])))


def kernel(x_prompt, x_sample, state_mlstm_C, state_mlstm_n, state_mlstm_m, cache_dil_kv0, cache_dil_kv1, cache_dil_kv2, state_hgrn_S, cache_swa_kv, mlstm_w_in, mlstm_b_gates, mlstm_norm_g, mlstm_w_out, dil_w_in, dil_w_out, hgrn_w_in, hgrn_b_f, hgrn_lb_logits, hgrn_norm_g, hgrn_w_out, swa_w_in, swa_sinks, swa_w_out, ln1_g, ln1_b, ln2_g, ln2_b, mlp_w1, mlp_w2):
    nb, t, _ = x_prompt.shape
    ns = x_sample.shape[0]
    assert x_sample.shape[1] == 1 and DEPTH == 4
    hp = x_prompt.reshape(nb * t, D_MODEL)
    hs = x_sample.reshape(ns, D_MODEL)
    cos_p, sin_p = _rope_tables(jnp.arange(t, dtype=jnp.int32))
    cos_s, sin_s = _rope_tables(jnp.full((ns,), PAST_LEN, dtype=jnp.int32))

    def finish_layer(i, hp, hs, mix_p, mix_s, w_out):
        g1, b1, g2, b2 = ln1_g[i][None], ln1_b[i][None], ln2_g[i][None], ln2_b[i][None]
        w_bf = w_out.astype(BF16)
        w1, w2 = mlp_w1[i].astype(BF16), mlp_w2[i].astype(BF16)
        if i == 1:
            hp = _dil_out_ln(mix_p[0], mix_p[1], w_bf, hp, g1, b1, LN_TM, by_residue=True)
            hp = _layer_tail(hp, w1, w2, g2, b2, LN_TM, shuffle="to_natural")
            hs = _dil_out_ln(mix_s[0], mix_s[1], w_bf, hs, g1, b1, ns)
            hs = _layer_tail(hs, w1, w2, g2, b2, ns)
        else:
            hp = _layer_tail(hp, w1, w2, g2, b2, LN_TM, mix_p, w_bf, g1, b1,
                             shuffle="to_residue" if i == 0 else None, seq_len=t)
            hs = _layer_tail(hs, w1, w2, g2, b2, ns, mix_s, w_bf, g1, b1)
        return hp, hs

    w_in = mlstm_w_in[0]
    main = 2 * ML_QK + 2 * ML_VD
    w_main = w_in[:, :main].astype(BF16)
    w_gate = jnp.concatenate([_pad_cols(w_in[:, main:main + ML_HEADS], LANES),
                              _pad_cols(w_in[:, main + ML_HEADS:], LANES)], axis=1).astype(BF16)
    pp = _proj(hp, w_main, PROMPT_TM)
    gp = _proj(hp, w_gate, PROMPT_TM)
    ps = _proj(hs, w_main, ns)
    gs = _proj(hs, w_gate, ns)
    norm_g = mlstm_norm_g[0][None]
    mix_p, c_p, n_p, m_p = _mlstm_prompt(pp.reshape(nb, t, main), gp.reshape(nb, t, 2 * LANES),
                                         mlstm_b_gates[0], norm_g, nb, t)
    mix_s, c_s, n_s, m_s = _mlstm_sample(ps, gs, mlstm_b_gates[0], norm_g,
                                         state_mlstm_C, state_mlstm_n, state_mlstm_m)
    out_c_p = c_p[None]
    out_n_p = n_p[:, :, 0, :][None]
    out_m_p = m_p[:, :, 0, 0][None]
    hp, hs = finish_layer(0, hp, hs, mix_p.reshape(nb * t, ML_VD), mix_s, mlstm_w_out[0])

    w_in = dil_w_in[0].astype(BF16)
    qk_cols = tuple((g * DIL_GW, g * DIL_GW + 2 * DIL_HEADS * HEAD_DIM) for g in range(len(DIL_GROUPS)))
    by_residue = lambda tab: jnp.transpose(tab.reshape(t // RESIDUES, RESIDUES, LANES), (1, 0, 2))
    pp = _proj(hp, w_in, PROMPT_TM, qk_cols, (by_residue(cos_p), by_residue(sin_p)), by_residue=True)
    ps = _proj(hs, w_in, ns, qk_cols, (cos_s, sin_s))
    outs, lses, dil_kv_p = [], [], []
    for g, (win, _) in enumerate(DIL_GROUPS):
        o, lse = _dil_prompt_group(pp, g)
        outs.append(o)
        lses.append(lse)
        kv = pp[:, :, (t - win) // RESIDUES:, g * DIL_GW + DIL_HEADS * HEAD_DIM:(g + 1) * DIL_GW]
        kv = jnp.transpose(kv, (0, 2, 1, 3))
        dil_kv_p.append(kv.reshape(nb, win, 2, DIL_HEADS, HEAD_DIM)[None])
    qkv_s = ps.reshape(ns, 9, DIL_HEADS, HEAD_DIM)
    outs_s, lses_s, dil_kv_s = [], [], []
    for g, cache in enumerate((cache_dil_kv0, cache_dil_kv1, cache_dil_kv2)):
        win, dil = DIL_GROUPS[g]
        assert cache.shape[2] == win and win // dil == SPAN
        blk_n, blk_h = DIL_DECODE_BLOCKS[g]
        o, lse, rolled = _decode_roll(qkv_s[:, 3 * g], qkv_s[:, 3 * g + 1], qkv_s[:, 3 * g + 2],
                                      cache, dil, blk_n, blk_h)
        outs_s.append(o.reshape(ns, DIL_HEADS * HEAD_DIM))
        lses_s.append(lse.reshape(ns, DIL_HEADS * HEAD_DIM))
        dil_kv_s.append(rolled)
    hp, hs = finish_layer(1, hp, hs, (outs, lses), (outs_s, lses_s), dil_w_out[0])

    w_in = hgrn_w_in[0].astype(BF16)
    pp = _proj(hp, w_in, PROMPT_TM)
    ps = _proj(hs, w_in, ns)
    b_f, ng = hgrn_b_f[0][None], hgrn_norm_g[0][None]
    mix_p, s_p = _hgrn_prompt(pp.reshape(nb, t, 4 * HG_W), b_f, hgrn_lb_logits, ng, nb, t, 2)
    mix_s, s_s = _hgrn_sample(ps, b_f, hgrn_lb_logits, ng, state_hgrn_S, 2)
    hp, hs = finish_layer(2, hp, hs, mix_p.reshape(nb * t, HG_W), mix_s, hgrn_w_out[0])

    w_in = swa_w_in[0].astype(BF16)
    width = w_in.shape[1]
    qk_cols = ((0, SWA_Q + SWA_KV),)
    pp = _proj(hp, w_in, PROMPT_TM, qk_cols, (cos_p, sin_p))
    ps = _proj(hs, w_in, ns, qk_cols, (cos_s, sin_s))
    pp3 = pp.reshape(nb, t, width)
    mix_p = _swa_prompt(pp3, swa_sinks[0], nb, t)
    swa_kv_p = pp3[:, t - SPAN:, SWA_Q:].reshape(nb, SPAN, 2, SWA_KV_HEADS, HEAD_DIM)[None]
    q_s = ps[:, :SWA_Q].reshape(ns, SWA_HEADS, HEAD_DIM)
    kv_s = ps[:, SWA_Q:].reshape(ns, 2, SWA_KV_HEADS, HEAD_DIM)
    assert cache_swa_kv.shape[2] == SPAN
    mix_s, _, swa_kv_s = _decode_roll(q_s, kv_s[:, 0], kv_s[:, 1], cache_swa_kv, 1, *SWA_DECODE_BLOCK,
                                      sinks=swa_sinks[0][:, None])
    mix_s = mix_s.reshape(ns, SWA_Q)
    hp, hs = finish_layer(3, hp, hs, mix_p, mix_s, swa_w_out[0])

    return (hp.reshape(nb, t, D_MODEL), hs.reshape(ns, 1, D_MODEL),
            out_c_p, c_s, out_n_p, n_s, out_m_p, m_s,
            dil_kv_p[0], dil_kv_s[0], dil_kv_p[1], dil_kv_s[1], dil_kv_p[2], dil_kv_s[2],
            s_p[None], s_s, swa_kv_p, swa_kv_s)
```

```python
import functools
import math

import numpy as np
import jax
import jax.numpy as jnp
from jax import lax
from jax.experimental import pallas as pl
from jax.experimental.pallas import tpu as pltpu

F32 = jnp.float32
BF16 = jnp.bfloat16

D_MODEL = 1024
DEPTH = 4
PAST_LEN = 8192
D_FF = 4 * D_MODEL
ALPHA = (2 * DEPTH) ** 0.25
LN_EPS = 1e-5
NORM_EPS = 1e-6
ROPE_THETA = 10000.0

ML_HEADS = 4
ML_DK = 128
ML_DV = 256
ML_QK = ML_HEADS * ML_DK
ML_VD = ML_HEADS * ML_DV
ML_CHUNK = 256

DIL_GROUPS = ((128, 1), (512, 4), (2048, 16))
DIL_HEADS = 8
HEAD_DIM = 64
DIL_GW = 3 * DIL_HEADS * HEAD_DIM
SPAN = 128

HG_HEADS = 8
HG_DK = 128
HG_CHUNK = 128
HG_W = HG_HEADS * HG_DK

SWA_HEADS = 16
SWA_KV_HEADS = 2
SWA_Q = SWA_HEADS * HEAD_DIM
SWA_KV = SWA_KV_HEADS * HEAD_DIM

LANES = 128
NEG = -1e30
VMEM_LIMIT = 48 * 1024 * 1024


def _params(n_axes, vmem=VMEM_LIMIT):
    return pltpu.CompilerParams(dimension_semantics=("arbitrary",) * n_axes, vmem_limit_bytes=vmem)


def _dot(a, b):
    return jnp.dot(a, b, preferred_element_type=F32)


def _dot_nt(a, b):
    return lax.dot_general(a, b, (((1,), (1,)), ((), ())), preferred_element_type=F32)


def _split3(x):
    hi = x.astype(BF16)
    r1 = x - hi.astype(F32)
    mid = r1.astype(BF16)
    lo = (r1 - mid.astype(F32)).astype(BF16)
    return hi, mid, lo


def _dot01_left(a01, x):
    hi, mid, lo = _split3(x)
    return _dot(a01, hi) + _dot(a01, mid) + _dot(a01, lo)


def _dot01_right(x, a01):
    hi, mid, lo = _split3(x)
    return _dot(hi, a01) + _dot(mid, a01) + _dot(lo, a01)


def _sigmoid(x):
    return 0.5 * jnp.tanh(0.5 * x) + 0.5


def _log_sigmoid(x):
    return jnp.minimum(x, 0.0) - jnp.log1p(jnp.exp(-jnp.abs(x)))


def _layer_norm(z, g, b):
    mu = jnp.mean(z, axis=-1, keepdims=True)
    zc = z - mu
    var = jnp.mean(zc * zc, axis=-1, keepdims=True)
    return zc * lax.rsqrt(var + LN_EPS) * g + b


def _rms_gate(h, g, gate):
    ms = jnp.mean(h * h, axis=-1, keepdims=True)
    return h * lax.rsqrt(ms + NORM_EPS) * g * _sigmoid(gate)


PROJ_CHUNK = 512
RESIDUES = 16


def _row_spec(width, tm, residue_tiles=None):
    if residue_tiles is None:
        return pl.BlockSpec((tm, width), lambda i: (i, 0))
    return pl.BlockSpec((None, RESIDUES, tm // RESIDUES, width),
                        lambda i: (i // residue_tiles, 0, i % residue_tiles, 0))


def _rows(ref):
    v = ref[...]
    return v.reshape(-1, v.shape[-1])


def _proj_kernel(*refs, plan):
    if any(rope for _, _, rope in plan):
        x_ref, w_ref, cos_ref, sin_ref, o_ref = refs
    else:
        x_ref, w_ref, o_ref = refs
    xb = _rows(x_ref).astype(BF16)
    for start, width, rope in plan:
        acc = _dot(xb, w_ref[:, start:start + width])
        if rope:
            reps = width // LANES
            cos = jnp.concatenate([_rows(cos_ref)] * reps, axis=1)
            sin = jnp.concatenate([_rows(sin_ref)] * reps, axis=1)
            lane = lax.broadcasted_iota(jnp.int32, acc.shape, 1)
            first_half = (lane & (HEAD_DIM - 1)) < HEAD_DIM // 2
            partner = jnp.where(first_half, pltpu.roll(acc, width - HEAD_DIM // 2, 1),
                                pltpu.roll(acc, HEAD_DIM // 2, 1))
            acc = acc * cos + partner * sin
        o_ref[..., start:start + width] = acc.reshape(o_ref.shape[:-1] + (width,))


def _proj(x, w_bf, tm, rope_cols=(), tables=None, by_residue=False):
    k, n = w_bf.shape
    plan = []
    start = 0
    while start < n:
        rope = any(lo <= start < hi for lo, hi in rope_cols)
        limit = min([hi for lo, hi in rope_cols if lo <= start < hi] +
                    [lo for lo, hi in rope_cols if lo > start] + [n])
        width = min(PROJ_CHUNK, limit - start)
        plan.append((start, width, rope))
        start += width
    if by_residue:
        seqs, _, per_residue, _ = x.shape
        tiles = per_residue * RESIDUES // tm
        steps = seqs * tiles
        tab = pl.BlockSpec((RESIDUES, tm // RESIDUES, LANES), lambda i: (0, i % tiles, 0))
    else:
        tiles = None
        steps = x.shape[0] // tm
        if rope_cols:
            t_blocks = tables[0].shape[0] // tm
            tab = pl.BlockSpec((tm, LANES), lambda i: (i % t_blocks, 0))
    in_specs = [_row_spec(k, tm, tiles),
                pl.BlockSpec((k, n), lambda i: (0, 0), pipeline_mode=pl.Buffered(1))]
    args = [x, w_bf]
    if rope_cols:
        in_specs += [tab, tab]
        args += list(tables)
    return pl.pallas_call(
        functools.partial(_proj_kernel, plan=tuple(plan)),
        grid=(steps,),
        in_specs=in_specs,
        out_specs=_row_spec(n, tm, tiles),
        out_shape=jax.ShapeDtypeStruct(x.shape[:-1] + (n,), F32),
        compiler_params=_params(1),
        name="proj",
    )(*args)


def _rope_tables(pos):
    half = HEAD_DIM // 2
    inv_freq = jnp.power(ROPE_THETA, -jnp.arange(half, dtype=F32) / half)
    ang = pos.astype(F32)[:, None] * inv_freq[None, :]
    cos = jnp.cos(ang)
    sin = jnp.sin(ang)
    cos = jnp.concatenate([cos, cos, cos, cos], axis=1)
    sin = jnp.concatenate([-sin, sin, -sin, sin], axis=1)
    return cos, sin


def _dil_out_ln_kernel(o0, o1, o2, l0, l1, l2, w_ref, x_ref, g_ref, b_ref, y_ref):
    a0, a1, a2 = _rows(l0), _rows(l1), _rows(l2)
    mx = jnp.maximum(jnp.maximum(a0, a1), a2)
    e0, e1, e2 = jnp.exp(a0 - mx), jnp.exp(a1 - mx), jnp.exp(a2 - mx)
    y = (e0 * _rows(o0) + e1 * _rows(o1) + e2 * _rows(o2)) / (e0 + e1 + e2)
    z = ALPHA * _rows(x_ref) + _dot(y.astype(BF16), w_ref[...])
    y_ref[...] = _layer_norm(z, g_ref[...], b_ref[...]).reshape(y_ref.shape)


def _dil_out_ln(outs, lses, w_bf, x, g, b, tm, by_residue=False):
    kin = outs[0].shape[-1]
    if by_residue:
        tiles = x.shape[2] * RESIDUES // tm
        steps = x.shape[0] * tiles
    else:
        tiles = None
        steps = x.shape[0] // tm
    const = lambda i: (0, 0)
    return pl.pallas_call(
        _dil_out_ln_kernel,
        grid=(steps,),
        in_specs=[_row_spec(kin, tm, tiles)] * 6 + [
            pl.BlockSpec((kin, D_MODEL), const), _row_spec(D_MODEL, tm, tiles),
            pl.BlockSpec((1, D_MODEL), const), pl.BlockSpec((1, D_MODEL), const)],
        out_specs=_row_spec(D_MODEL, tm, tiles),
        out_shape=jax.ShapeDtypeStruct(x.shape, F32),
        compiler_params=_params(1),
        name="dil_out_ln",
    )(*outs, *lses, w_bf, x, g, b)


FF_CHUNK = 1024


def _tail_kernel(*refs, with_mix, shuffle):
    refs = list(refs)
    scratch = refs.pop() if shuffle else None
    y_ref = refs.pop()
    if with_mix:
        mix_ref, wo_ref, x_ref, g1_ref, b1_ref = refs[:5]
        refs = refs[5:]
        z = ALPHA * _rows(x_ref) + _dot(_rows(mix_ref).astype(BF16), wo_ref[...])
        x = _layer_norm(z, g1_ref[...], b1_ref[...])
    else:
        x = _rows(refs.pop(0))
    w1_ref, w2_ref, g2_ref, b2_ref = refs
    xb = x.astype(BF16)
    acc = jnp.zeros(x.shape, F32)
    for c in range(D_FF // FF_CHUNK):
        a = _dot(xb, w1_ref[:, c * FF_CHUNK:(c + 1) * FF_CHUNK])
        a = jnp.square(jnp.maximum(a, 0.0)).astype(BF16)
        acc = acc + _dot(a, w2_ref[c * FF_CHUNK:(c + 1) * FF_CHUNK, :])
    y = _layer_norm(ALPHA * x + acc, g2_ref[...], b2_ref[...])
    per = y.shape[0] // RESIDUES
    lane_tiles = [slice(c * LANES, (c + 1) * LANES) for c in range(D_MODEL // LANES)]
    if shuffle == "to_residue":
        for c, cols in enumerate(lane_tiles):
            scratch[c] = y[:, cols]
        for r in range(RESIDUES):
            for c, cols in enumerate(lane_tiles):
                y_ref[r, :, cols] = scratch[c, pl.ds(r, per, stride=RESIDUES), :]
    elif shuffle == "to_natural":
        for r in range(RESIDUES):
            for c, cols in enumerate(lane_tiles):
                scratch[c, pl.ds(r, per, stride=RESIDUES), :] = y[r * per:(r + 1) * per, cols]
        for c, cols in enumerate(lane_tiles):
            y_ref[:, cols] = scratch[c]
    else:
        y_ref[...] = y


def _layer_tail(x, w1_bf, w2_bf, g2, b2, tm, mix=None, w_out_bf=None, g1=None, b1=None,
                shuffle=None, seq_len=None):
    const = lambda i: (0, 0)
    resident = pl.Buffered(1)
    in_tiles = out_tiles = None
    out_shape = x.shape
    if shuffle == "to_natural":
        in_tiles = x.shape[2] * RESIDUES // tm
        steps = x.shape[0] * in_tiles
        out_shape = (x.shape[0] * x.shape[1] * x.shape[2], D_MODEL)
    else:
        steps = x.shape[0] // tm
        if shuffle == "to_residue":
            out_tiles = seq_len // tm
            out_shape = (x.shape[0] // seq_len, RESIDUES, seq_len // RESIDUES, D_MODEL)
    in_specs, args = [], []
    if mix is not None:
        kin = mix.shape[-1]
        in_specs += [_row_spec(kin, tm, in_tiles),
                     pl.BlockSpec((kin, D_MODEL), const, pipeline_mode=resident)]
        args += [mix, w_out_bf]
    in_specs.append(_row_spec(D_MODEL, tm, in_tiles))
    args.append(x)
    if mix is not None:
        in_specs += [pl.BlockSpec((1, D_MODEL), const)] * 2
        args += [g1, b1]
    in_specs += [pl.BlockSpec((D_MODEL, D_FF), const, pipeline_mode=resident),
                 pl.BlockSpec((D_FF, D_MODEL), const, pipeline_mode=resident),
                 pl.BlockSpec((1, D_MODEL), const), pl.BlockSpec((1, D_MODEL), const)]
    args += [w1_bf, w2_bf, g2, b2]
    return pl.pallas_call(
        functools.partial(_tail_kernel, with_mix=mix is not None, shuffle=shuffle),
        grid=(steps,),
        in_specs=in_specs,
        out_specs=_row_spec(D_MODEL, tm, out_tiles),
        out_shape=jax.ShapeDtypeStruct(out_shape, F32),
        scratch_shapes=[pltpu.VMEM((D_MODEL // LANES, tm, LANES), F32)] if shuffle else [],
        compiler_params=_params(1),
        name="layer_tail",
    )(*args)


def _mlstm_prompt_kernel(q_ref, k_ref, v_ref, og_ref, gc_ref, gr_ref, brow_ref, bcol_ref, g_ref,
                         h_ref, c_ref, n_ref, m_ref, *, nb, chunk):
    step = pl.program_id(0)

    @pl.when(step == 0)
    def _():
        c_ref[...] = jnp.zeros(c_ref.shape, F32)
        n_ref[...] = jnp.zeros(n_ref.shape, F32)
        m_ref[...] = jnp.zeros(m_ref.shape, F32)

    row = lax.broadcasted_iota(jnp.int32, (chunk, chunk), 0)
    col = lax.broadcasted_iota(jnp.int32, (chunk, chunk), 1)
    causal = col <= row
    tril = jnp.where(causal, 1.0, 0.0).astype(BF16)
    triu = jnp.where(row <= col, 1.0, 0.0).astype(BF16)
    scale = ML_DK ** -0.5
    last = chunk - 1

    for b in range(nb):
        gc = gc_ref[b] + brow_ref[...]
        ig_cols = gc[:, :LANES]
        b_cols = _dot01_left(tril, _log_sigmoid(gc[:, LANES:]))
        gr = gr_ref[b] + bcol_ref[...]
        b_rows = _dot01_right(_log_sigmoid(gr), triu)
        for h in range(ML_HEADS):
            bc = b_cols[:, h:h + 1]
            igc = ig_cols[:, h:h + 1]
            br = b_rows[ML_HEADS + h:ML_HEADS + h + 1, :]
            igr = gr[h:h + 1, :]
            m_prev = m_ref[b, h][:, 0:1]
            log_intra = jnp.where(causal, bc - br + igr, NEG)
            log_prev = bc + m_prev
            m_t = jnp.maximum(log_prev, jnp.max(log_intra, axis=1, keepdims=True))
            w_intra = jnp.exp(log_intra - m_t)
            w_prev = jnp.exp(log_prev - m_t)
            q = q_ref[b, :, h * ML_DK:(h + 1) * ML_DK]
            k = k_ref[b, :, h * ML_DK:(h + 1) * ML_DK] * scale
            v = v_ref[b, :, h * ML_DV:(h + 1) * ML_DV]
            qb, kb, vb = q.astype(BF16), k.astype(BF16), v.astype(BF16)
            a = _dot_nt(qb, kb) * w_intra
            c_old = c_ref[b, h]
            n_old = n_ref[b, h]
            num = _dot(a.astype(BF16), vb) + w_prev * _dot(qb, c_old.astype(BF16))
            den = jnp.sum(a, axis=1, keepdims=True) + w_prev * jnp.sum(q * n_old, axis=1, keepdims=True)
            hh = num / jnp.maximum(jnp.abs(den), jnp.exp(-m_t))
            m_new = m_t[last:last + 1, :]
            b_end = bc[last:last + 1, :]
            w_end = jnp.exp(b_end - bc + igc - m_new)
            decay = jnp.exp(b_end + m_prev - m_new)
            kw = k * w_end
            c_ref[b, h] = decay * c_old + _dot(kw.T.astype(BF16), vb)
            n_ref[b, h] = decay * n_old + jnp.sum(kw, axis=0, keepdims=True)
            m_ref[b, h] = jnp.broadcast_to(m_new, (1, LANES))
            sl = slice(h * ML_DV, (h + 1) * ML_DV)
            h_ref[b, :, sl] = _rms_gate(hh, g_ref[:, sl], og_ref[b, :, sl])


def _mlstm_prompt(p, gates, b_gates, norm_g, nb, t):
    chunk = ML_CHUNK
    g8 = jnp.concatenate([gates[..., :ML_HEADS], gates[..., LANES:LANES + ML_HEADS]], axis=-1)
    g_rows = jnp.transpose(g8, (0, 2, 1))
    zeros = jnp.zeros((LANES - ML_HEADS,), F32)
    b_row = jnp.concatenate([b_gates[:ML_HEADS], zeros, b_gates[ML_HEADS:], zeros])[None, :]
    b_col = b_gates[:, None]
    kernel = functools.partial(_mlstm_prompt_kernel, nb=nb, chunk=chunk)
    const2 = lambda c: (0, 0)
    return pl.pallas_call(
        kernel,
        grid=(t // chunk,),
        in_specs=[pl.BlockSpec((nb, chunk, ML_QK), lambda c: (0, c, 0)),
                  pl.BlockSpec((nb, chunk, ML_QK), lambda c: (0, c, 1)),
                  pl.BlockSpec((nb, chunk, ML_VD), lambda c: (0, c, 1)),
                  pl.BlockSpec((nb, chunk, ML_VD), lambda c: (0, c, 2)),
                  pl.BlockSpec((nb, chunk, 2 * LANES), lambda c: (0, c, 0)),
                  pl.BlockSpec((nb, 2 * ML_HEADS, chunk), lambda c: (0, 0, c)),
                  pl.BlockSpec((1, 2 * LANES), const2),
                  pl.BlockSpec((2 * ML_HEADS, 1), const2),
                  pl.BlockSpec((1, ML_VD), const2)],
        out_specs=[pl.BlockSpec((nb, chunk, ML_VD), lambda c: (0, c, 0)),
                   pl.BlockSpec((nb, ML_HEADS, ML_DK, ML_DV), lambda c: (0, 0, 0, 0)),
                   pl.BlockSpec((nb, ML_HEADS, 1, ML_DK), lambda c: (0, 0, 0, 0)),
                   pl.BlockSpec((nb, ML_HEADS, 1, LANES), lambda c: (0, 0, 0, 0))],
        out_shape=[jax.ShapeDtypeStruct((nb, t, ML_VD), F32),
                   jax.ShapeDtypeStruct((nb, ML_HEADS, ML_DK, ML_DV), F32),
                   jax.ShapeDtypeStruct((nb, ML_HEADS, 1, ML_DK), F32),
                   jax.ShapeDtypeStruct((nb, ML_HEADS, 1, LANES), F32)],
        compiler_params=_params(1),
        name="mlstm_prompt",
    )(p, p, p, p, gates, g_rows, b_row, b_col, norm_g)


STATE_BLOCK = 8


def _column(row_vec, eye):
    return jnp.sum(jnp.where(eye, row_vec, 0.0), axis=1, keepdims=True)


def _mlstm_sample_kernel(p_ref, gate_ref, bias_ref, g_ref, c_ref, n_ref, m_ref,
                         h_ref, c_out, n_out, m_out):
    eye = (lax.broadcasted_iota(jnp.int32, (ML_DK, ML_DK), 0)
           == lax.broadcasted_iota(jnp.int32, (ML_DK, ML_DK), 1))
    for s in range(p_ref.shape[0]):
        _mlstm_sample_one(s, eye, p_ref, gate_ref, bias_ref, g_ref, c_ref, n_ref, m_ref,
                          h_ref, c_out, n_out, m_out)


def _mlstm_sample_one(s, eye, p_ref, gate_ref, bias_ref, g_ref, c_ref, n_ref, m_ref,
                      h_ref, c_out, n_out, m_out):
    gates = gate_ref[s] + bias_ref[...]
    ig = gates[:, :LANES]
    lf = _log_sigmoid(gates[:, LANES:])
    m_old = m_ref[s]
    log_prev = lf + m_old
    m_t = jnp.maximum(log_prev, ig)
    w_i_all = jnp.exp(ig - m_t)
    w_p_all = jnp.exp(log_prev - m_t)
    floor_all = jnp.exp(-m_t)
    m_out[s] = m_t
    scale = ML_DK ** -0.5
    for h in range(ML_HEADS):
        q = p_ref[s, :, h * ML_DK:(h + 1) * ML_DK]
        k = p_ref[s, :, ML_QK + h * ML_DK:ML_QK + (h + 1) * ML_DK] * scale
        v = p_ref[s, :, 2 * ML_QK + h * ML_DV:2 * ML_QK + (h + 1) * ML_DV]
        og = p_ref[s, :, 2 * ML_QK + ML_VD + h * ML_DV:2 * ML_QK + ML_VD + (h + 1) * ML_DV]
        w_i = w_i_all[:, h:h + 1]
        w_p = w_p_all[:, h:h + 1]
        c_old = c_ref[0, s, h]
        n_old = n_ref[0, s, h:h + 1, :]
        q_col = _column(q, eye)
        k_col = _column(k, eye)
        a = jnp.sum(q * k, axis=1, keepdims=True) * w_i
        num = a * v + w_p * jnp.sum(c_old * q_col, axis=0, keepdims=True)
        den = a + w_p * jnp.sum(q * n_old, axis=1, keepdims=True)
        hh = num / jnp.maximum(jnp.abs(den), floor_all[:, h:h + 1])
        c_out[0, s, h] = w_p * c_old + (w_i * k_col) * v
        n_out[0, s, h:h + 1, :] = w_p * n_old + w_i * k
        sl = slice(h * ML_DV, (h + 1) * ML_DV)
        h_ref[s, :, sl] = _rms_gate(hh, g_ref[:, sl], og)


def _mlstm_sample(p, gates, b_gates, norm_g, c0, n0, m0):
    n = p.shape[0]
    zeros = jnp.zeros((LANES - ML_HEADS,), F32)
    b_row = jnp.concatenate([b_gates[:ML_HEADS], zeros, b_gates[ML_HEADS:], zeros])[None, :]
    m_pad = jnp.pad(m0[0], ((0, 0), (0, LANES - ML_HEADS)))[:, None, :]
    row3 = lambda i: (i, 0, 0)
    const2 = lambda i: (0, 0)
    sb = STATE_BLOCK
    h, c1, n1, m1 = pl.pallas_call(
        _mlstm_sample_kernel,
        grid=(n // sb,),
        in_specs=[pl.BlockSpec((sb, 1, p.shape[1]), row3),
                  pl.BlockSpec((sb, 1, 2 * LANES), row3),
                  pl.BlockSpec((1, 2 * LANES), const2),
                  pl.BlockSpec((1, ML_VD), const2),
                  pl.BlockSpec((1, sb, ML_HEADS, ML_DK, ML_DV), lambda i: (0, i, 0, 0, 0)),
                  pl.BlockSpec((1, sb, ML_HEADS, ML_DK), lambda i: (0, i, 0, 0)),
                  pl.BlockSpec((sb, 1, LANES), row3)],
        out_specs=[pl.BlockSpec((sb, 1, ML_VD), row3),
                   pl.BlockSpec((1, sb, ML_HEADS, ML_DK, ML_DV), lambda i: (0, i, 0, 0, 0)),
                   pl.BlockSpec((1, sb, ML_HEADS, ML_DK), lambda i: (0, i, 0, 0)),
                   pl.BlockSpec((sb, 1, LANES), row3)],
        out_shape=[jax.ShapeDtypeStruct((n, 1, ML_VD), F32),
                   jax.ShapeDtypeStruct(c0.shape, F32),
                   jax.ShapeDtypeStruct(n0.shape, F32),
                   jax.ShapeDtypeStruct((n, 1, LANES), F32)],
        compiler_params=_params(1),
        name="mlstm_sample",
    )(p[:, None, :], gates[:, None, :], b_row, norm_g, c0, n0, m_pad)
    return h[:, 0, :], c1, n1, m1[:, 0, :ML_HEADS][None]


HG_LEVELS = (1, 2, 4, 8, 16, 32, 64)


def _bcast_rows(x, group, which):
    n, w = x.shape
    x3 = x.reshape(n // group, group, w)
    return jnp.broadcast_to(x3[:, which:which + 1, :], x3.shape).reshape(n, w)


def _level_row_masks(level, t_idx):
    pos = t_idx & (level - 1)
    return dict(odd=((t_idx >> int(math.log2(level))) & 1) == 1,
                at_least=[pos >= i for i in range(level)],
                at_most=[pos <= i for i in range(level)])


def _level_log_decay(level, lf, b_incl, masks, shifted):
    if level >= 8:
        since = b_incl - _bcast_rows(b_incl - lf, level, 0)
        until = _bcast_rows(b_incl, level, level - 1) - b_incl
        return jnp.where(masks["odd"], since, until)
    since = lf
    until = jnp.zeros_like(lf)
    for i in range(1, level):
        since = since + jnp.where(masks["at_least"][i], shifted(i), 0.0)
        until = until + jnp.where(masks["at_most"][level - 1 - i], shifted(-i), 0.0)
    return jnp.where(masks["odd"], since, until)


def _hgrn_prompt_kernel(q_ref, f_ref, i_ref, g_ref, bf_ref, lb_ref, ng_ref, o_ref, s_ref, st_ref,
                        *, nb, chunk, layer_idx):
    step = pl.program_id(0)

    @pl.when(step == 0)
    def _():
        st_ref[...] = jnp.zeros(st_ref.shape, F32)

    logits = [lb_ref[i:i + 1, :] for i in range(DEPTH)]
    mx = functools.reduce(jnp.maximum, logits)
    es = [jnp.exp(l - mx) for l in logits]
    tot = functools.reduce(lambda a, c: a + c, es)
    cum = []
    for e in es:
        cum.append(e / tot if not cum else cum[-1] + e / tot)
    lb = cum[layer_idx] - cum[0]

    row = lax.broadcasted_iota(jnp.int32, (chunk, chunk), 0)
    col = lax.broadcasted_iota(jnp.int32, (chunk, chunk), 1)
    tril = jnp.where(col <= row, 1.0, 0.0).astype(BF16)
    eye = row == col
    level_masks = []
    for level in HG_LEVELS:
        sh = int(math.log2(level))
        u, w = row >> sh, col >> sh
        level_masks.append(((u & 1) * 4096 + (u - w)) == 4097)
    t_idx = lax.broadcasted_iota(jnp.int32, (chunk, HG_DK), 0)
    row_masks = [_level_row_masks(level, t_idx) for level in HG_LEVELS]
    last = chunk - 1

    for b in range(nb):
        fg_all = lb + (1.0 - lb) * _sigmoid(f_ref[b] + bf_ref[...])
        lf_all = jnp.log(fg_all)
        b_all = _dot01_left(tril, lf_all)
        for h in range(HG_HEADS):
            sl = slice(h * HG_DK, (h + 1) * HG_DK)
            qx = q_ref[b, :, sl]
            q = qx * _sigmoid(qx)
            k = 1.0 - fg_all[:, sl]
            lf = lf_all[:, sl]
            bi = b_all[:, sl]
            v = i_ref[b, :, sl]
            vb = v.astype(BF16)
            a = jnp.where(eye, _dot_nt(q.astype(BF16), k.astype(BF16)), 0.0)
            rolled = {}

            def shifted(i, lf=lf, rolled=rolled):
                if i not in rolled:
                    rolled[i] = pltpu.roll(lf, i % chunk, 0)
                return rolled[i]

            for level, mask, rows in zip(HG_LEVELS, level_masks, row_masks):
                e = jnp.exp(_level_log_decay(level, lf, bi, rows, shifted))
                a = a + jnp.where(mask, _dot_nt((q * e).astype(BF16), (k * e).astype(BF16)), 0.0)
            st = st_ref[b, h]
            o = _dot(a.astype(BF16), vb) + _dot_nt((q * jnp.exp(bi)).astype(BF16), st.astype(BF16))
            b_end = bi[last:last + 1, :]
            kd = k * jnp.exp(b_end - bi)
            st_ref[b, h] = st * jnp.exp(b_end) + _dot(v.T.astype(BF16), kd.astype(BF16))
            o_ref[b, :, sl] = _rms_gate(o, ng_ref[:, sl], g_ref[b, :, sl])

    @pl.when(step == pl.num_programs(0) - 1)
    def _():
        for b in range(nb):
            for h in range(HG_HEADS):
                s_ref[b, h] = st_ref[b, h].T


def _hgrn_prompt(p, b_f, lb_logits, norm_g, nb, t, layer_idx):
    chunk = HG_CHUNK
    kernel = functools.partial(_hgrn_prompt_kernel, nb=nb, chunk=chunk, layer_idx=layer_idx)
    const2 = lambda c: (0, 0)
    blk = lambda j: pl.BlockSpec((nb, chunk, HG_W), lambda c: (0, c, j))
    return pl.pallas_call(
        kernel,
        grid=(t // chunk,),
        in_specs=[blk(0), blk(1), blk(2), blk(3),
                  pl.BlockSpec((1, HG_W), const2), pl.BlockSpec((DEPTH, HG_W), const2),
                  pl.BlockSpec((1, HG_W), const2)],
        out_specs=[pl.BlockSpec((nb, chunk, HG_W), lambda c: (0, c, 0)),
                   pl.BlockSpec((nb, HG_HEADS, HG_DK, HG_DK), lambda c: (0, 0, 0, 0))],
        out_shape=[jax.ShapeDtypeStruct((nb, t, HG_W), F32),
                   jax.ShapeDtypeStruct((nb, HG_HEADS, HG_DK, HG_DK), F32)],
        scratch_shapes=[pltpu.VMEM((nb, HG_HEADS, HG_DK, HG_DK), F32)],
        compiler_params=_params(1),
        name="hgrn_prompt",
    )(p, p, p, p, b_f, lb_logits, norm_g)


def _hgrn_sample_kernel(p_ref, bf_ref, lb_ref, ng_ref, s_ref, o_ref, s_out, *, layer_idx):
    logits = [lb_ref[i:i + 1, :] for i in range(DEPTH)]
    mx = functools.reduce(jnp.maximum, logits)
    es = [jnp.exp(l - mx) for l in logits]
    tot = functools.reduce(lambda a, c: a + c, es)
    cum = []
    for e in es:
        cum.append(e / tot if not cum else cum[-1] + e / tot)
    lb = cum[layer_idx] - cum[0]
    eye = (lax.broadcasted_iota(jnp.int32, (HG_DK, HG_DK), 0)
           == lax.broadcasted_iota(jnp.int32, (HG_DK, HG_DK), 1))
    for s in range(p_ref.shape[0]):
        qx = p_ref[s, :, 0:HG_W]
        q_all = qx * _sigmoid(qx)
        fg_all = lb + (1.0 - lb) * _sigmoid(p_ref[s, :, HG_W:2 * HG_W] + bf_ref[...])
        for h in range(HG_HEADS):
            sl = slice(h * HG_DK, (h + 1) * HG_DK)
            q = q_all[:, sl]
            fg = fg_all[:, sl]
            k = 1.0 - fg
            v = p_ref[s, :, 2 * HG_W + h * HG_DK:2 * HG_W + (h + 1) * HG_DK]
            gate = p_ref[s, :, 3 * HG_W + h * HG_DK:3 * HG_W + (h + 1) * HG_DK]
            s_old = s_ref[0, s, h]
            decay = jnp.exp(jnp.log(fg))
            a = jnp.sum(q * k, axis=1, keepdims=True)
            o = a * v + jnp.sum(s_old * _column(q * decay, eye), axis=0, keepdims=True)
            s_out[0, s, h] = _column(decay, eye) * s_old + _column(k, eye) * v
            o_ref[s, :, sl] = _rms_gate(o, ng_ref[:, sl], gate)


def _hgrn_sample(p, b_f, lb_logits, norm_g, s0, layer_idx):
    n = p.shape[0]
    row3 = lambda i: (i, 0, 0)
    const2 = lambda i: (0, 0)
    sb = STATE_BLOCK
    state = pl.BlockSpec((1, sb, HG_HEADS, HG_DK, HG_DK), lambda i: (0, i, 0, 0, 0))
    o, s1 = pl.pallas_call(
        functools.partial(_hgrn_sample_kernel, layer_idx=layer_idx),
        grid=(n // sb,),
        in_specs=[pl.BlockSpec((sb, 1, p.shape[1]), row3), pl.BlockSpec((1, HG_W), const2),
                  pl.BlockSpec((DEPTH, HG_W), const2), pl.BlockSpec((1, HG_W), const2), state],
        out_specs=[pl.BlockSpec((sb, 1, HG_W), row3), state],
        out_shape=[jax.ShapeDtypeStruct((n, 1, HG_W), F32), jax.ShapeDtypeStruct(s0.shape, F32)],
        compiler_params=_params(1),
        name="hgrn_sample",
    )(p[:, None, :], b_f, lb_logits, norm_g, s0)
    return o[:, 0, :], s1


def _band_kernel(*refs, n_heads, group, fold, n_blocks, with_sinks, with_lse):
    refs = list(refs)
    sink_ref = refs.pop(0) if with_sinks else None
    q_ref, ko_ref, kp_ref, vo_ref, vp_ref, o_ref = refs[:6]
    lse_ref = refs[6] if with_lse else None
    step = pl.program_id(2)
    per = SPAN // fold
    shift = int(math.log2(per))

    def local_pos(idx):
        return ((idx & (per - 1)) * fold) + (idx >> shift)

    qi = lax.broadcasted_iota(jnp.int32, (SPAN, 2 * SPAN), 0)
    ki = lax.broadcasted_iota(jnp.int32, (SPAN, 2 * SPAN), 1)
    qpos = SPAN + local_pos(qi)
    kpos = (ki & SPAN) + local_pos(ki & (SPAN - 1))
    band = (kpos <= qpos) & (kpos >= qpos - SPAN)
    first = jnp.where(step > 0, 0, SPAN)
    band_first = band & (kpos >= first)
    lo_q = lax.broadcasted_iota(jnp.int32, (SPAN, LANES), 1) < HEAD_DIM
    lo_kv = lax.broadcasted_iota(jnp.int32, (2 * SPAN, LANES), 1) < HEAD_DIM
    scale = HEAD_DIM ** -0.5

    def block_rows(ref, blk):
        v = ref[:, blk * per:(blk + 1) * per, :]
        return v.reshape(SPAN, v.shape[-1])

    for blk in range(n_blocks):
        valid = band_first if blk == 0 else band
        k_prev = _rows(kp_ref) if blk == 0 else block_rows(ko_ref, blk - 1)
        v_prev = _rows(vp_ref) if blk == 0 else block_rows(vo_ref, blk - 1)
        kcat = jnp.concatenate([k_prev, block_rows(ko_ref, blk)], axis=0)
        vcat = jnp.concatenate([v_prev, block_rows(vo_ref, blk)], axis=0)
        q_blk = block_rows(q_ref, blk)
        tiles = {}

        def kv_tile(name, src, kv_head, want_hi):
            key = (name, kv_head, want_hi)
            if key not in tiles:
                t = src[:, (kv_head // 2) * LANES:(kv_head // 2 + 1) * LANES]
                if (kv_head % 2 == 1) != want_hi:
                    t = pltpu.roll(t, HEAD_DIM, 1)
                tiles[key] = t
            return tiles[key]

        def softmax_parts(s, head):
            s = jnp.where(valid, s, NEG)
            m = jnp.max(s, axis=1, keepdims=True)
            if with_sinks:
                m = jnp.maximum(m, sink_ref[head])
            p = jnp.exp(s - m)
            l = jnp.sum(p, axis=1, keepdims=True)
            if with_sinks:
                l = l + jnp.exp(sink_ref[head] - m)
            return p.astype(BF16), m, l

        for j in range(n_heads // 2):
            sl = slice(j * LANES, (j + 1) * LANES)
            qpair = q_blk[:, sl] * scale
            h_lo, h_hi = 2 * j, 2 * j + 1
            q_lo = jnp.where(lo_q, qpair, 0.0).astype(BF16)
            q_hi = jnp.where(lo_q, 0.0, qpair).astype(BF16)
            s_lo = _dot_nt(q_lo, kv_tile("k", kcat, h_lo // group, False).astype(BF16))
            s_hi = _dot_nt(q_hi, kv_tile("k", kcat, h_hi // group, True).astype(BF16))
            p_lo, m_lo, l_lo = softmax_parts(s_lo, h_lo)
            p_hi, m_hi, l_hi = softmax_parts(s_hi, h_hi)
            v_lo = jnp.where(lo_kv, kv_tile("v", vcat, h_lo // group, False), 0.0).astype(BF16)
            v_hi = jnp.where(lo_kv, 0.0, kv_tile("v", vcat, h_hi // group, True)).astype(BF16)
            o = (_dot(p_lo, v_lo) + _dot(p_hi, v_hi)) / jnp.where(lo_q, l_lo, l_hi)
            o_ref[:, blk * per:(blk + 1) * per, sl] = o.reshape(fold, per, LANES)
            if with_lse:
                lse = jnp.where(lo_q, m_lo + jnp.log(l_lo), m_hi + jnp.log(l_hi))
                lse_ref[:, blk * per:(blk + 1) * per, sl] = lse.reshape(fold, per, LANES)


BAND_BLOCKS = 4


def _band_call(name, arrays, sinks, fold, classes, rows, q_col, k_col, v_col, q_width, kv_width,
               n_heads, group, with_lse):
    seqs = arrays.shape[0]
    per = SPAN // fold
    steps = rows // (per * BAND_BLOCKS)
    own = lambda width, col: pl.BlockSpec((None, fold, None, per * BAND_BLOCKS, width),
                                          lambda b, c, n: (b, 0, c, n, col))
    prev = lambda width, col: pl.BlockSpec(
        (None, fold, None, per, width),
        lambda b, c, n: (b, 0, c, jnp.maximum(n * BAND_BLOCKS - 1, 0), col))
    out = pl.BlockSpec((None, fold, None, per * BAND_BLOCKS, q_width), lambda b, c, n: (b, 0, c, n, 0))
    shape = jax.ShapeDtypeStruct((seqs, fold, classes, rows, q_width), F32)
    in_specs = [own(q_width, q_col), own(kv_width, k_col), prev(kv_width, k_col),
                own(kv_width, v_col), prev(kv_width, v_col)]
    args = [arrays] * 5
    if sinks is not None:
        in_specs = [pl.BlockSpec(memory_space=pltpu.SMEM)] + in_specs
        args = [sinks] + args
    return pl.pallas_call(
        functools.partial(_band_kernel, n_heads=n_heads, group=group, fold=fold, n_blocks=BAND_BLOCKS,
                          with_sinks=sinks is not None, with_lse=with_lse),
        grid=(seqs, classes, steps),
        in_specs=in_specs,
        out_specs=[out, out] if with_lse else out,
        out_shape=[shape, shape] if with_lse else shape,
        compiler_params=_params(3),
        name=name,
    )(*args)


def _dil_prompt_group(p, g):
    _, dil = DIL_GROUPS[g]
    seqs, _, rows, width = p.shape
    fold = RESIDUES // dil
    w = DIL_HEADS * HEAD_DIM
    o, lse = _band_call("dil_prompt_g%d" % g, p.reshape(seqs, fold, dil, rows, width), None, fold, dil, rows,
                        3 * g, 3 * g + 1, 3 * g + 2, w, w, DIL_HEADS, 1, True)
    return o.reshape(seqs, RESIDUES, rows, w), lse.reshape(seqs, RESIDUES, rows, w)


def _swa_prompt(p, sinks, nb, t):
    kcol = SWA_Q // SWA_KV
    o = _band_call("swa_prompt", p.reshape(nb, 1, 1, t, p.shape[-1]), sinks, 1, 1, t,
                   0, kcol, kcol + 1, SWA_Q, SWA_KV, SWA_HEADS, SWA_HEADS // SWA_KV_HEADS, False)
    return o.reshape(nb * t, SWA_Q)


def _decode_roll_kernel(*refs, nb, hb, group, length, dil, with_sinks):
    refs = list(refs)
    sink_ref = refs.pop(0) if with_sinks else None
    q_ref, kn_ref, vn_ref, c_ref, o_ref, lse_ref, cout_ref = refs
    h0 = pl.program_id(1) * hb
    eye = (lax.broadcasted_iota(jnp.int32, (HEAD_DIM, HEAD_DIM), 0)
           == lax.broadcasted_iota(jnp.int32, (HEAD_DIM, HEAD_DIM), 1))
    pos = lax.broadcasted_iota(jnp.int32, (1, length), 1)
    attended = (pos & (dil - 1)) == 0
    is_last = lax.broadcasted_iota(jnp.int32, (HEAD_DIM, length), 1) == length - 1
    scale = HEAD_DIM ** -0.5
    for i in range(nb if group == 1 else 0):
        kt = c_ref[0, i, 0]
        vt = c_ref[0, i, 1]
        row = lambda ref: jnp.stack([ref[i, pl.ds(h0 + j, 1), :] for j in range(hb)])
        q, kn, vn = row(q_ref) * scale, row(kn_ref), row(vn_ref)
        column = lambda r: jnp.sum(jnp.where(eye[None], r, 0.0), axis=2, keepdims=True)
        s = jnp.sum(kt * column(q), axis=1, keepdims=True)
        s = jnp.where(attended[None], s, NEG)
        s_new = jnp.sum(q * kn, axis=2, keepdims=True)
        m = jnp.maximum(jnp.max(s, axis=2, keepdims=True), s_new)
        p = jnp.exp(s - m)
        p_new = jnp.exp(s_new - m)
        l = jnp.sum(p, axis=2, keepdims=True) + p_new
        acc_col = jnp.sum(vt * p, axis=2, keepdims=True)
        acc = jnp.sum(jnp.where(eye[None], acc_col, 0.0), axis=1, keepdims=True)
        out = (acc + p_new * vn) / l
        lse = jnp.broadcast_to(m + jnp.log(l), out.shape)
        for j in range(hb):
            o_ref[i, pl.ds(h0 + j, 1), :] = out[j]
            lse_ref[i, pl.ds(h0 + j, 1), :] = lse[j]
        cout_ref[0, i, 0] = jnp.where(is_last[None], column(kn), pltpu.roll(kt, length - 1, 2))
        cout_ref[0, i, 1] = jnp.where(is_last[None], column(vn), pltpu.roll(vt, length - 1, 2))
    for i in range(nb if group > 1 else 0):
        for j in range(hb):
            kt = c_ref[0, i, 0, j]
            vt = c_ref[0, i, 1, j]
            kn = kn_ref[i, pl.ds(h0 + j, 1), :]
            vn = vn_ref[i, pl.ds(h0 + j, 1), :]
            rows = pl.ds((h0 + j) * group, group)
            q = q_ref[i, rows, :] * scale
            s = _dot(q.astype(BF16), kt.astype(BF16))
            s = jnp.where(attended, s, NEG)
            s_new = jnp.sum(q * kn, axis=1, keepdims=True)
            m = jnp.maximum(jnp.max(s, axis=1, keepdims=True), s_new)
            if with_sinks:
                sink = sink_ref[rows, :]
                m = jnp.maximum(m, sink)
            p = jnp.exp(s - m)
            p_new = jnp.exp(s_new - m)
            l = jnp.sum(p, axis=1, keepdims=True) + p_new
            if with_sinks:
                l = l + jnp.exp(sink - m)
            acc = _dot_nt(p.astype(BF16), vt.astype(BF16))
            o_ref[i, rows, :] = (acc + p_new * vn) / l
            lse_ref[i, rows, :] = jnp.broadcast_to(m + jnp.log(l), (group, HEAD_DIM))
            cout_ref[0, i, 0, j] = jnp.where(is_last, _column(kn, eye), pltpu.roll(kt, length - 1, 1))
            cout_ref[0, i, 1, j] = jnp.where(is_last, _column(vn, eye), pltpu.roll(vt, length - 1, 1))


def _decode_roll(q, k_new, v_new, cache, dil, nb, hb, sinks=None):
    _, n, length, _, kv_heads, d = cache.shape
    heads = q.shape[1]
    group = heads // kv_heads
    cache_t = jnp.transpose(cache, (0, 1, 3, 4, 5, 2))
    cblock = pl.BlockSpec((1, nb, 2, hb, d, length), lambda i, h: (0, i, 0, h, 0, 0))
    per_seq = lambda width: pl.BlockSpec((nb, width, d), lambda i, h: (i, 0, 0))
    in_specs = [per_seq(heads), per_seq(kv_heads), per_seq(kv_heads), cblock]
    args = [q, k_new, v_new, cache_t]
    if sinks is not None:
        in_specs = [pl.BlockSpec((heads, 1), lambda i, h: (0, 0))] + in_specs
        args = [sinks] + args
    o, lse, new_t = pl.pallas_call(
        functools.partial(_decode_roll_kernel, nb=nb, hb=hb, group=group, length=length, dil=dil,
                          with_sinks=sinks is not None),
        grid=(n // nb, kv_heads // hb),
        in_specs=in_specs,
        out_specs=[per_seq(heads), per_seq(heads), cblock],
        out_shape=[jax.ShapeDtypeStruct(q.shape, F32), jax.ShapeDtypeStruct(q.shape, F32),
                   jax.ShapeDtypeStruct(cache_t.shape, F32)],
        compiler_params=_params(2),
        name="decode_roll",
    )(*args)
    return o, lse, jnp.transpose(new_t, (0, 1, 5, 2, 3, 4))


PROMPT_TM = 512
LN_TM = 512
DIL_DECODE_BLOCKS = ((8, 8), (2, 8), (1, 4))
SWA_DECODE_BLOCK = (16, 2)


def _pad_cols(w, width):
    return jnp.pad(w, ((0, 0), (0, width - w.shape[1])))


def kernel(x_prompt, x_sample, state_mlstm_C, state_mlstm_n, state_mlstm_m, cache_dil_kv0, cache_dil_kv1, cache_dil_kv2, state_hgrn_S, cache_swa_kv, mlstm_w_in, mlstm_b_gates, mlstm_norm_g, mlstm_w_out, dil_w_in, dil_w_out, hgrn_w_in, hgrn_b_f, hgrn_lb_logits, hgrn_norm_g, hgrn_w_out, swa_w_in, swa_sinks, swa_w_out, ln1_g, ln1_b, ln2_g, ln2_b, mlp_w1, mlp_w2):
    nb, t, _ = x_prompt.shape
    ns = x_sample.shape[0]
    assert x_sample.shape[1] == 1 and DEPTH == 4
    hp = x_prompt.reshape(nb * t, D_MODEL)
    hs = x_sample.reshape(ns, D_MODEL)
    cos_p, sin_p = _rope_tables(jnp.arange(t, dtype=jnp.int32))
    cos_s, sin_s = _rope_tables(jnp.full((ns,), PAST_LEN, dtype=jnp.int32))

    def finish_layer(i, hp, hs, mix_p, mix_s, w_out):
        g1, b1, g2, b2 = ln1_g[i][None], ln1_b[i][None], ln2_g[i][None], ln2_b[i][None]
        w_bf = w_out.astype(BF16)
        w1, w2 = mlp_w1[i].astype(BF16), mlp_w2[i].astype(BF16)
        if i == 1:
            hp = _dil_out_ln(mix_p[0], mix_p[1], w_bf, hp, g1, b1, LN_TM, by_residue=True)
            hp = _layer_tail(hp, w1, w2, g2, b2, LN_TM, shuffle="to_natural")
            hs = _dil_out_ln(mix_s[0], mix_s[1], w_bf, hs, g1, b1, ns)
            hs = _layer_tail(hs, w1, w2, g2, b2, ns)
        else:
            hp = _layer_tail(hp, w1, w2, g2, b2, LN_TM, mix_p, w_bf, g1, b1,
                             shuffle="to_residue" if i == 0 else None, seq_len=t)
            hs = _layer_tail(hs, w1, w2, g2, b2, ns, mix_s, w_bf, g1, b1)
        return hp, hs

    w_in = mlstm_w_in[0]
    main = 2 * ML_QK + 2 * ML_VD
    w_main = w_in[:, :main].astype(BF16)
    w_gate = jnp.concatenate([_pad_cols(w_in[:, main:main + ML_HEADS], LANES),
                              _pad_cols(w_in[:, main + ML_HEADS:], LANES)], axis=1).astype(BF16)
    pp = _proj(hp, w_main, PROMPT_TM)
    gp = _proj(hp, w_gate, PROMPT_TM)
    ps = _proj(hs, w_main, ns)
    gs = _proj(hs, w_gate, ns)
    norm_g = mlstm_norm_g[0][None]
    mix_p, c_p, n_p, m_p = _mlstm_prompt(pp.reshape(nb, t, main), gp.reshape(nb, t, 2 * LANES),
                                         mlstm_b_gates[0], norm_g, nb, t)
    mix_s, c_s, n_s, m_s = _mlstm_sample(ps, gs, mlstm_b_gates[0], norm_g,
                                         state_mlstm_C, state_mlstm_n, state_mlstm_m)
    out_c_p = c_p[None]
    out_n_p = n_p[:, :, 0, :][None]
    out_m_p = m_p[:, :, 0, 0][None]
    hp, hs = finish_layer(0, hp, hs, mix_p.reshape(nb * t, ML_VD), mix_s, mlstm_w_out[0])

    w_in = dil_w_in[0].astype(BF16)
    qk_cols = tuple((g * DIL_GW, g * DIL_GW + 2 * DIL_HEADS * HEAD_DIM) for g in range(len(DIL_GROUPS)))
    by_residue = lambda tab: jnp.transpose(tab.reshape(t // RESIDUES, RESIDUES, LANES), (1, 0, 2))
    pp = _proj(hp, w_in, PROMPT_TM, qk_cols, (by_residue(cos_p), by_residue(sin_p)), by_residue=True)
    ps = _proj(hs, w_in, ns, qk_cols, (cos_s, sin_s))
    outs, lses, dil_kv_p = [], [], []
    for g, (win, _) in enumerate(DIL_GROUPS):
        o, lse = _dil_prompt_group(pp, g)
        outs.append(o)
        lses.append(lse)
        kv = pp[:, :, (t - win) // RESIDUES:, g * DIL_GW + DIL_HEADS * HEAD_DIM:(g + 1) * DIL_GW]
        kv = jnp.transpose(kv, (0, 2, 1, 3))
        dil_kv_p.append(kv.reshape(nb, win, 2, DIL_HEADS, HEAD_DIM)[None])
    qkv_s = ps.reshape(ns, 9, DIL_HEADS, HEAD_DIM)
    outs_s, lses_s, dil_kv_s = [], [], []
    for g, cache in enumerate((cache_dil_kv0, cache_dil_kv1, cache_dil_kv2)):
        win, dil = DIL_GROUPS[g]
        assert cache.shape[2] == win and win // dil == SPAN
        blk_n, blk_h = DIL_DECODE_BLOCKS[g]
        o, lse, rolled = _decode_roll(qkv_s[:, 3 * g], qkv_s[:, 3 * g + 1], qkv_s[:, 3 * g + 2],
                                      cache, dil, blk_n, blk_h)
        outs_s.append(o.reshape(ns, DIL_HEADS * HEAD_DIM))
        lses_s.append(lse.reshape(ns, DIL_HEADS * HEAD_DIM))
        dil_kv_s.append(rolled)
    hp, hs = finish_layer(1, hp, hs, (outs, lses), (outs_s, lses_s), dil_w_out[0])

    w_in = hgrn_w_in[0].astype(BF16)
    pp = _proj(hp, w_in, PROMPT_TM)
    ps = _proj(hs, w_in, ns)
    b_f, ng = hgrn_b_f[0][None], hgrn_norm_g[0][None]
    mix_p, s_p = _hgrn_prompt(pp.reshape(nb, t, 4 * HG_W), b_f, hgrn_lb_logits, ng, nb, t, 2)
    mix_s, s_s = _hgrn_sample(ps, b_f, hgrn_lb_logits, ng, state_hgrn_S, 2)
    hp, hs = finish_layer(2, hp, hs, mix_p.reshape(nb * t, HG_W), mix_s, hgrn_w_out[0])

    w_in = swa_w_in[0].astype(BF16)
    width = w_in.shape[1]
    qk_cols = ((0, SWA_Q + SWA_KV),)
    pp = _proj(hp, w_in, PROMPT_TM, qk_cols, (cos_p, sin_p))
    ps = _proj(hs, w_in, ns, qk_cols, (cos_s, sin_s))
    pp3 = pp.reshape(nb, t, width)
    mix_p = _swa_prompt(pp3, swa_sinks[0], nb, t)
    swa_kv_p = pp3[:, t - SPAN:, SWA_Q:].reshape(nb, SPAN, 2, SWA_KV_HEADS, HEAD_DIM)[None]
    q_s = ps[:, :SWA_Q].reshape(ns, SWA_HEADS, HEAD_DIM)
    kv_s = ps[:, SWA_Q:].reshape(ns, 2, SWA_KV_HEADS, HEAD_DIM)
    assert cache_swa_kv.shape[2] == SPAN
    mix_s, _, swa_kv_s = _decode_roll(q_s, kv_s[:, 0], kv_s[:, 1], cache_swa_kv, 1, *SWA_DECODE_BLOCK,
                                      sinks=swa_sinks[0][:, None])
    mix_s = mix_s.reshape(ns, SWA_Q)
    hp, hs = finish_layer(3, hp, hs, mix_p, mix_s, swa_w_out[0])

    return (hp.reshape(nb, t, D_MODEL), hs.reshape(ns, 1, D_MODEL),
            out_c_p, c_s, out_n_p, n_s, out_m_p, m_s,
            dil_kv_p[0], dil_kv_s[0], dil_kv_p[1], dil_kv_s[1], dil_kv_p[2], dil_kv_s[2],
            s_p[None], s_s, swa_kv_p, swa_kv_s)
```

```python
import functools
import math

import numpy as np
import jax
import jax.numpy as jnp
from jax import lax
from jax.experimental import pallas as pl
from jax.experimental.pallas import tpu as pltpu

F32 = jnp.float32
BF16 = jnp.bfloat16

D_MODEL = 1024
DEPTH = 4
PAST_LEN = 8192
D_FF = 4 * D_MODEL
ALPHA = (2 * DEPTH) ** 0.25
LN_EPS = 1e-5
NORM_EPS = 1e-6
ROPE_THETA = 10000.0

ML_HEADS = 4
ML_DK = 128
ML_DV = 256
ML_QK = ML_HEADS * ML_DK
ML_VD = ML_HEADS * ML_DV
ML_CHUNK = 256

DIL_GROUPS = ((128, 1), (512, 4), (2048, 16))
DIL_HEADS = 8
HEAD_DIM = 64
DIL_GW = 3 * DIL_HEADS * HEAD_DIM
SPAN = 128

HG_HEADS = 8
HG_DK = 128
HG_CHUNK = 128
HG_W = HG_HEADS * HG_DK

SWA_HEADS = 16
SWA_KV_HEADS = 2
SWA_Q = SWA_HEADS * HEAD_DIM
SWA_KV = SWA_KV_HEADS * HEAD_DIM

LANES = 128
NEG = -1e30
VMEM_LIMIT = 48 * 1024 * 1024


def _params(n_axes, vmem=VMEM_LIMIT):
    return pltpu.CompilerParams(dimension_semantics=("arbitrary",) * n_axes, vmem_limit_bytes=vmem)


def _dot(a, b):
    return jnp.dot(a, b, preferred_element_type=F32)


def _dot_nt(a, b):
    return lax.dot_general(a, b, (((1,), (1,)), ((), ())), preferred_element_type=F32)


def _split3(x):
    hi = x.astype(BF16)
    r1 = x - hi.astype(F32)
    mid = r1.astype(BF16)
    lo = (r1 - mid.astype(F32)).astype(BF16)
    return hi, mid, lo


def _dot01_left(a01, x):
    hi, mid, lo = _split3(x)
    return _dot(a01, hi) + _dot(a01, mid) + _dot(a01, lo)


def _dot01_right(x, a01):
    hi, mid, lo = _split3(x)
    return _dot(hi, a01) + _dot(mid, a01) + _dot(lo, a01)


def _sigmoid(x):
    return 0.5 * jnp.tanh(0.5 * x) + 0.5


def _log_sigmoid(x):
    return jnp.minimum(x, 0.0) - jnp.log1p(jnp.exp(-jnp.abs(x)))


def _layer_norm(z, g, b):
    mu = jnp.mean(z, axis=-1, keepdims=True)
    zc = z - mu
    var = jnp.mean(zc * zc, axis=-1, keepdims=True)
    return zc * lax.rsqrt(var + LN_EPS) * g + b


def _rms_gate(h, g, gate):
    ms = jnp.mean(h * h, axis=-1, keepdims=True)
    return h * lax.rsqrt(ms + NORM_EPS) * g * _sigmoid(gate)


PROJ_CHUNK = 512
RESIDUES = 16


def _row_spec(width, tm, residue_tiles=None):
    if residue_tiles is None:
        return pl.BlockSpec((tm, width), lambda i: (i, 0))
    return pl.BlockSpec((None, RESIDUES, tm // RESIDUES, width),
                        lambda i: (i // residue_tiles, 0, i % residue_tiles, 0))


def _rows(ref):
    v = ref[...]
    return v.reshape(-1, v.shape[-1])


def _proj_kernel(*refs, plan):
    if any(rope for _, _, rope in plan):
        x_ref, w_ref, cos_ref, sin_ref, o_ref = refs
    else:
        x_ref, w_ref, o_ref = refs
    xb = _rows(x_ref).astype(BF16)
    for start, width, rope in plan:
        acc = _dot(xb, w_ref[:, start:start + width])
        if rope:
            reps = width // LANES
            cos = jnp.concatenate([_rows(cos_ref)] * reps, axis=1)
            sin = jnp.concatenate([_rows(sin_ref)] * reps, axis=1)
            lane = lax.broadcasted_iota(jnp.int32, acc.shape, 1)
            first_half = (lane & (HEAD_DIM - 1)) < HEAD_DIM // 2
            partner = jnp.where(first_half, pltpu.roll(acc, width - HEAD_DIM // 2, 1),
                                pltpu.roll(acc, HEAD_DIM // 2, 1))
            acc = acc * cos + partner * sin
        o_ref[..., start:start + width] = acc.reshape(o_ref.shape[:-1] + (width,))


def _proj(x, w_bf, tm, rope_cols=(), tables=None, by_residue=False):
    k, n = w_bf.shape
    plan = []
    start = 0
    while start < n:
        rope = any(lo <= start < hi for lo, hi in rope_cols)
        limit = min([hi for lo, hi in rope_cols if lo <= start < hi] +
                    [lo for lo, hi in rope_cols if lo > start] + [n])
        width = min(PROJ_CHUNK, limit - start)
        plan.append((start, width, rope))
        start += width
    if by_residue:
        seqs, _, per_residue, _ = x.shape
        tiles = per_residue * RESIDUES // tm
        steps = seqs * tiles
        tab = pl.BlockSpec((RESIDUES, tm // RESIDUES, LANES), lambda i: (0, i % tiles, 0))
    else:
        tiles = None
        steps = x.shape[0] // tm
        if rope_cols:
            t_blocks = tables[0].shape[0] // tm
            tab = pl.BlockSpec((tm, LANES), lambda i: (i % t_blocks, 0))
    in_specs = [_row_spec(k, tm, tiles),
                pl.BlockSpec((k, n), lambda i: (0, 0), pipeline_mode=pl.Buffered(1))]
    args = [x, w_bf]
    if rope_cols:
        in_specs += [tab, tab]
        args += list(tables)
    return pl.pallas_call(
        functools.partial(_proj_kernel, plan=tuple(plan)),
        grid=(steps,),
        in_specs=in_specs,
        out_specs=_row_spec(n, tm, tiles),
        out_shape=jax.ShapeDtypeStruct(x.shape[:-1] + (n,), F32),
        compiler_params=_params(1),
        name="proj",
    )(*args)


def _rope_tables(pos):
    half = HEAD_DIM // 2
    inv_freq = jnp.power(ROPE_THETA, -jnp.arange(half, dtype=F32) / half)
    ang = pos.astype(F32)[:, None] * inv_freq[None, :]
    cos = jnp.cos(ang)
    sin = jnp.sin(ang)
    cos = jnp.concatenate([cos, cos, cos, cos], axis=1)
    sin = jnp.concatenate([-sin, sin, -sin, sin], axis=1)
    return cos, sin


def _dil_out_ln_kernel(o0, o1, o2, l0, l1, l2, w_ref, x_ref, g_ref, b_ref, y_ref):
    a0, a1, a2 = _rows(l0), _rows(l1), _rows(l2)
    mx = jnp.maximum(jnp.maximum(a0, a1), a2)
    e0, e1, e2 = jnp.exp(a0 - mx), jnp.exp(a1 - mx), jnp.exp(a2 - mx)
    y = (e0 * _rows(o0) + e1 * _rows(o1) + e2 * _rows(o2)) / (e0 + e1 + e2)
    z = ALPHA * _rows(x_ref) + _dot(y.astype(BF16), w_ref[...])
    y_ref[...] = _layer_norm(z, g_ref[...], b_ref[...]).reshape(y_ref.shape)


def _dil_out_ln(outs, lses, w_bf, x, g, b, tm, by_residue=False):
    kin = outs[0].shape[-1]
    if by_residue:
        tiles = x.shape[2] * RESIDUES // tm
        steps = x.shape[0] * tiles
    else:
        tiles = None
        steps = x.shape[0] // tm
    const = lambda i: (0, 0)
    return pl.pallas_call(
        _dil_out_ln_kernel,
        grid=(steps,),
        in_specs=[_row_spec(kin, tm, tiles)] * 6 + [
            pl.BlockSpec((kin, D_MODEL), const), _row_spec(D_MODEL, tm, tiles),
            pl.BlockSpec((1, D_MODEL), const), pl.BlockSpec((1, D_MODEL), const)],
        out_specs=_row_spec(D_MODEL, tm, tiles),
        out_shape=jax.ShapeDtypeStruct(x.shape, F32),
        compiler_params=_params(1),
        name="dil_out_ln",
    )(*outs, *lses, w_bf, x, g, b)


FF_CHUNK = 1024


def _tail_kernel(*refs, with_mix, shuffle, side):
    refs = list(refs)
    scratch = refs.pop() if shuffle else None
    if side:
        cout_ref, lse_ref, o_ref = refs.pop(), refs.pop(), refs.pop()
    y_ref = refs.pop()
    if side:
        if side["chained"]:
            refs.pop()
        c_ref, vn_ref, kn_ref, q_ref = refs.pop(), refs.pop(), refs.pop(), refs.pop()
        block = side["first"] + pl.program_id(0)
        _decode_block((block % side["hsteps"]) * side["hb"], None, q_ref, kn_ref, vn_ref, c_ref,
                      o_ref, lse_ref, cout_ref, nb=1, hb=side["hb"], group=1,
                      length=side["length"], dil=side["dil"], with_sinks=False)
    if with_mix:
        mix_ref, wo_ref, x_ref, g1_ref, b1_ref = refs[:5]
        refs = refs[5:]
        z = ALPHA * _rows(x_ref) + _dot(_rows(mix_ref).astype(BF16), wo_ref[...])
        x = _layer_norm(z, g1_ref[...], b1_ref[...])
    else:
        x = _rows(refs.pop(0))
    w1_ref, w2_ref, g2_ref, b2_ref = refs
    xb = x.astype(BF16)
    acc = jnp.zeros(x.shape, F32)
    for c in range(D_FF // FF_CHUNK):
        a = _dot(xb, w1_ref[:, c * FF_CHUNK:(c + 1) * FF_CHUNK])
        a = jnp.square(jnp.maximum(a, 0.0)).astype(BF16)
        acc = acc + _dot(a, w2_ref[c * FF_CHUNK:(c + 1) * FF_CHUNK, :])
    y = _layer_norm(ALPHA * x + acc, g2_ref[...], b2_ref[...])
    per = y.shape[0] // RESIDUES
    lane_tiles = [slice(c * LANES, (c + 1) * LANES) for c in range(D_MODEL // LANES)]
    if shuffle == "to_residue":
        for c, cols in enumerate(lane_tiles):
            scratch[c] = y[:, cols]
        for r in range(RESIDUES):
            for c, cols in enumerate(lane_tiles):
                y_ref[r, :, cols] = scratch[c, pl.ds(r, per, stride=RESIDUES), :]
    elif shuffle == "to_natural":
        for r in range(RESIDUES):
            for c, cols in enumerate(lane_tiles):
                scratch[c, pl.ds(r, per, stride=RESIDUES), :] = y[r * per:(r + 1) * per, cols]
        for c, cols in enumerate(lane_tiles):
            y_ref[:, cols] = scratch[c]
    else:
        y_ref[...] = y


TAIL_VMEM_LIMIT = 58 * 1024 * 1024


def _layer_tail(x, w1_bf, w2_bf, g2, b2, tm, mix=None, w_out_bf=None, g1=None, b1=None,
                shuffle=None, seq_len=None, side=None):
    const = lambda i: (0, 0)
    resident = pl.Buffered(1)
    in_tiles = out_tiles = None
    out_shape = x.shape
    if shuffle == "to_natural":
        in_tiles = x.shape[2] * RESIDUES // tm
        steps = x.shape[0] * in_tiles
        out_shape = (x.shape[0] * x.shape[1] * x.shape[2], D_MODEL)
    else:
        steps = x.shape[0] // tm
        if shuffle == "to_residue":
            out_tiles = seq_len // tm
            out_shape = (x.shape[0] // seq_len, RESIDUES, seq_len // RESIDUES, D_MODEL)
    in_specs, args = [], []
    if mix is not None:
        kin = mix.shape[-1]
        in_specs += [_row_spec(kin, tm, in_tiles),
                     pl.BlockSpec((kin, D_MODEL), const, pipeline_mode=resident)]
        args += [mix, w_out_bf]
    in_specs.append(_row_spec(D_MODEL, tm, in_tiles))
    args.append(x)
    if mix is not None:
        in_specs += [pl.BlockSpec((1, D_MODEL), const)] * 2
        args += [g1, b1]
    in_specs += [pl.BlockSpec((D_MODEL, D_FF), const, pipeline_mode=resident),
                 pl.BlockSpec((D_FF, D_MODEL), const, pipeline_mode=resident),
                 pl.BlockSpec((1, D_MODEL), const), pl.BlockSpec((1, D_MODEL), const)]
    args += [w1_bf, w2_bf, g2, b2]
    out_specs = _row_spec(D_MODEL, tm, out_tiles)
    out_shapes = jax.ShapeDtypeStruct(out_shape, F32)
    aliases = {}
    static_side = None
    if side is not None:
        cache_t, hb, first = side["cache_t"], side["hb"], side["first"]
        _, n_seq, _, kv_heads, d, length = cache_t.shape
        hsteps = kv_heads // hb
        seq_of = lambda i: (first + i) // hsteps
        cblock = pl.BlockSpec((1, 1, 2, hb, d, length), lambda i: (0, seq_of(i), 0, (first + i) % hsteps, 0, 0))
        per_seq = pl.BlockSpec((1, kv_heads, d), lambda i: (seq_of(i), 0, 0))
        in_specs += [per_seq, per_seq, per_seq, cblock]
        args += [side["q"], side["k_new"], side["v_new"], cache_t]
        if side["partial"] is not None:
            in_specs.append(pl.BlockSpec(memory_space=pl.ANY))
            args.append(side["partial"])
            aliases = {len(args) - 1: 3}
        part = pl.BlockSpec((1, kv_heads, d), lambda i: (i // hsteps, 0, 0))
        part_shape = jax.ShapeDtypeStruct((steps // hsteps, kv_heads, d), F32)
        out_specs = [out_specs, part, part, cblock]
        out_shapes = [out_shapes, part_shape, part_shape, jax.ShapeDtypeStruct(cache_t.shape, F32)]
        static_side = dict(first=first, hb=hb, hsteps=hsteps, length=length, dil=side["dil"],
                           chained=side["partial"] is not None)
    return pl.pallas_call(
        functools.partial(_tail_kernel, with_mix=mix is not None, shuffle=shuffle, side=static_side),
        grid=(steps,),
        in_specs=in_specs,
        out_specs=out_specs,
        out_shape=out_shapes,
        input_output_aliases=aliases,
        scratch_shapes=[pltpu.VMEM((D_MODEL // LANES, tm, LANES), F32)] if shuffle else [],
        compiler_params=_params(1, TAIL_VMEM_LIMIT if side is not None else VMEM_LIMIT),
        name="layer_tail",
    )(*args)


def _mlstm_prompt_kernel(q_ref, k_ref, v_ref, og_ref, gc_ref, gr_ref, brow_ref, bcol_ref, g_ref,
                         h_ref, c_ref, n_ref, m_ref, *, nb, chunk):
    step = pl.program_id(0)

    @pl.when(step == 0)
    def _():
        c_ref[...] = jnp.zeros(c_ref.shape, F32)
        n_ref[...] = jnp.zeros(n_ref.shape, F32)
        m_ref[...] = jnp.zeros(m_ref.shape, F32)

    row = lax.broadcasted_iota(jnp.int32, (chunk, chunk), 0)
    col = lax.broadcasted_iota(jnp.int32, (chunk, chunk), 1)
    causal = col <= row
    tril = jnp.where(causal, 1.0, 0.0).astype(BF16)
    triu = jnp.where(row <= col, 1.0, 0.0).astype(BF16)
    scale = ML_DK ** -0.5
    last = chunk - 1

    for b in range(nb):
        gc = gc_ref[b] + brow_ref[...]
        ig_cols = gc[:, :LANES]
        b_cols = _dot01_left(tril, _log_sigmoid(gc[:, LANES:]))
        gr = gr_ref[b] + bcol_ref[...]
        b_rows = _dot01_right(_log_sigmoid(gr), triu)
        for h in range(ML_HEADS):
            bc = b_cols[:, h:h + 1]
            igc = ig_cols[:, h:h + 1]
            br = b_rows[ML_HEADS + h:ML_HEADS + h + 1, :]
            igr = gr[h:h + 1, :]
            m_prev = m_ref[b, h][:, 0:1]
            log_intra = jnp.where(causal, bc - br + igr, NEG)
            log_prev = bc + m_prev
            m_t = jnp.maximum(log_prev, jnp.max(log_intra, axis=1, keepdims=True))
            w_intra = jnp.exp(log_intra - m_t)
            w_prev = jnp.exp(log_prev - m_t)
            q = q_ref[b, :, h * ML_DK:(h + 1) * ML_DK]
            k = k_ref[b, :, h * ML_DK:(h + 1) * ML_DK] * scale
            v = v_ref[b, :, h * ML_DV:(h + 1) * ML_DV]
            qb, kb, vb = q.astype(BF16), k.astype(BF16), v.astype(BF16)
            a = _dot_nt(qb, kb) * w_intra
            c_old = c_ref[b, h]
            n_old = n_ref[b, h]
            num = _dot(a.astype(BF16), vb) + w_prev * _dot(qb, c_old.astype(BF16))
            den = jnp.sum(a, axis=1, keepdims=True) + w_prev * jnp.sum(q * n_old, axis=1, keepdims=True)
            hh = num / jnp.maximum(jnp.abs(den), jnp.exp(-m_t))
            m_new = m_t[last:last + 1, :]
            b_end = bc[last:last + 1, :]
            w_end = jnp.exp(b_end - bc + igc - m_new)
            decay = jnp.exp(b_end + m_prev - m_new)
            kw = k * w_end
            c_ref[b, h] = decay * c_old + _dot(kw.T.astype(BF16), vb)
            n_ref[b, h] = decay * n_old + jnp.sum(kw, axis=0, keepdims=True)
            m_ref[b, h] = jnp.broadcast_to(m_new, (1, LANES))
            sl = slice(h * ML_DV, (h + 1) * ML_DV)
            h_ref[b, :, sl] = _rms_gate(hh, g_ref[:, sl], og_ref[b, :, sl])


def _mlstm_prompt(p, gates, b_gates, norm_g, nb, t):
    chunk = ML_CHUNK
    g8 = jnp.concatenate([gates[..., :ML_HEADS], gates[..., LANES:LANES + ML_HEADS]], axis=-1)
    g_rows = jnp.transpose(g8, (0, 2, 1))
    zeros = jnp.zeros((LANES - ML_HEADS,), F32)
    b_row = jnp.concatenate([b_gates[:ML_HEADS], zeros, b_gates[ML_HEADS:], zeros])[None, :]
    b_col = b_gates[:, None]
    kernel = functools.partial(_mlstm_prompt_kernel, nb=nb, chunk=chunk)
    const2 = lambda c: (0, 0)
    return pl.pallas_call(
        kernel,
        grid=(t // chunk,),
        in_specs=[pl.BlockSpec((nb, chunk, ML_QK), lambda c: (0, c, 0)),
                  pl.BlockSpec((nb, chunk, ML_QK), lambda c: (0, c, 1)),
                  pl.BlockSpec((nb, chunk, ML_VD), lambda c: (0, c, 1)),
                  pl.BlockSpec((nb, chunk, ML_VD), lambda c: (0, c, 2)),
                  pl.BlockSpec((nb, chunk, 2 * LANES), lambda c: (0, c, 0)),
                  pl.BlockSpec((nb, 2 * ML_HEADS, chunk), lambda c: (0, 0, c)),
                  pl.BlockSpec((1, 2 * LANES), const2),
                  pl.BlockSpec((2 * ML_HEADS, 1), const2),
                  pl.BlockSpec((1, ML_VD), const2)],
        out_specs=[pl.BlockSpec((nb, chunk, ML_VD), lambda c: (0, c, 0)),
                   pl.BlockSpec((nb, ML_HEADS, ML_DK, ML_DV), lambda c: (0, 0, 0, 0)),
                   pl.BlockSpec((nb, ML_HEADS, 1, ML_DK), lambda c: (0, 0, 0, 0)),
                   pl.BlockSpec((nb, ML_HEADS, 1, LANES), lambda c: (0, 0, 0, 0))],
        out_shape=[jax.ShapeDtypeStruct((nb, t, ML_VD), F32),
                   jax.ShapeDtypeStruct((nb, ML_HEADS, ML_DK, ML_DV), F32),
                   jax.ShapeDtypeStruct((nb, ML_HEADS, 1, ML_DK), F32),
                   jax.ShapeDtypeStruct((nb, ML_HEADS, 1, LANES), F32)],
        compiler_params=_params(1),
        name="mlstm_prompt",
    )(p, p, p, p, gates, g_rows, b_row, b_col, norm_g)


STATE_BLOCK = 8


def _column(row_vec, eye):
    return jnp.sum(jnp.where(eye, row_vec, 0.0), axis=1, keepdims=True)


def _mlstm_sample_kernel(p_ref, gate_ref, bias_ref, g_ref, c_ref, n_ref, m_ref,
                         h_ref, c_out, n_out, m_out):
    eye = (lax.broadcasted_iota(jnp.int32, (ML_DK, ML_DK), 0)
           == lax.broadcasted_iota(jnp.int32, (ML_DK, ML_DK), 1))
    for s in range(p_ref.shape[0]):
        _mlstm_sample_one(s, eye, p_ref, gate_ref, bias_ref, g_ref, c_ref, n_ref, m_ref,
                          h_ref, c_out, n_out, m_out)


def _mlstm_sample_one(s, eye, p_ref, gate_ref, bias_ref, g_ref, c_ref, n_ref, m_ref,
                      h_ref, c_out, n_out, m_out):
    gates = gate_ref[s] + bias_ref[...]
    ig = gates[:, :LANES]
    lf = _log_sigmoid(gates[:, LANES:])
    m_old = m_ref[s]
    log_prev = lf + m_old
    m_t = jnp.maximum(log_prev, ig)
    w_i_all = jnp.exp(ig - m_t)
    w_p_all = jnp.exp(log_prev - m_t)
    floor_all = jnp.exp(-m_t)
    m_out[s] = m_t
    scale = ML_DK ** -0.5
    for h in range(ML_HEADS):
        q = p_ref[s, :, h * ML_DK:(h + 1) * ML_DK]
        k = p_ref[s, :, ML_QK + h * ML_DK:ML_QK + (h + 1) * ML_DK] * scale
        v = p_ref[s, :, 2 * ML_QK + h * ML_DV:2 * ML_QK + (h + 1) * ML_DV]
        og = p_ref[s, :, 2 * ML_QK + ML_VD + h * ML_DV:2 * ML_QK + ML_VD + (h + 1) * ML_DV]
        w_i = w_i_all[:, h:h + 1]
        w_p = w_p_all[:, h:h + 1]
        c_old = c_ref[0, s, h]
        n_old = n_ref[0, s, h:h + 1, :]
        q_col = _column(q, eye)
        k_col = _column(k, eye)
        a = jnp.sum(q * k, axis=1, keepdims=True) * w_i
        num = a * v + w_p * jnp.sum(c_old * q_col, axis=0, keepdims=True)
        den = a + w_p * jnp.sum(q * n_old, axis=1, keepdims=True)
        hh = num / jnp.maximum(jnp.abs(den), floor_all[:, h:h + 1])
        c_out[0, s, h] = w_p * c_old + (w_i * k_col) * v
        n_out[0, s, h:h + 1, :] = w_p * n_old + w_i * k
        sl = slice(h * ML_DV, (h + 1) * ML_DV)
        h_ref[s, :, sl] = _rms_gate(hh, g_ref[:, sl], og)


def _mlstm_sample(p, gates, b_gates, norm_g, c0, n0, m0):
    n = p.shape[0]
    zeros = jnp.zeros((LANES - ML_HEADS,), F32)
    b_row = jnp.concatenate([b_gates[:ML_HEADS], zeros, b_gates[ML_HEADS:], zeros])[None, :]
    m_pad = jnp.pad(m0[0], ((0, 0), (0, LANES - ML_HEADS)))[:, None, :]
    row3 = lambda i: (i, 0, 0)
    const2 = lambda i: (0, 0)
    sb = STATE_BLOCK
    h, c1, n1, m1 = pl.pallas_call(
        _mlstm_sample_kernel,
        grid=(n // sb,),
        in_specs=[pl.BlockSpec((sb, 1, p.shape[1]), row3),
                  pl.BlockSpec((sb, 1, 2 * LANES), row3),
                  pl.BlockSpec((1, 2 * LANES), const2),
                  pl.BlockSpec((1, ML_VD), const2),
                  pl.BlockSpec((1, sb, ML_HEADS, ML_DK, ML_DV), lambda i: (0, i, 0, 0, 0)),
                  pl.BlockSpec((1, sb, ML_HEADS, ML_DK), lambda i: (0, i, 0, 0)),
                  pl.BlockSpec((sb, 1, LANES), row3)],
        out_specs=[pl.BlockSpec((sb, 1, ML_VD), row3),
                   pl.BlockSpec((1, sb, ML_HEADS, ML_DK, ML_DV), lambda i: (0, i, 0, 0, 0)),
                   pl.BlockSpec((1, sb, ML_HEADS, ML_DK), lambda i: (0, i, 0, 0)),
                   pl.BlockSpec((sb, 1, LANES), row3)],
        out_shape=[jax.ShapeDtypeStruct((n, 1, ML_VD), F32),
                   jax.ShapeDtypeStruct(c0.shape, F32),
                   jax.ShapeDtypeStruct(n0.shape, F32),
                   jax.ShapeDtypeStruct((n, 1, LANES), F32)],
        compiler_params=_params(1),
        name="mlstm_sample",
    )(p[:, None, :], gates[:, None, :], b_row, norm_g, c0, n0, m_pad)
    return h[:, 0, :], c1, n1, m1[:, 0, :ML_HEADS][None]


HG_LEVELS = (1, 2, 4, 8, 16, 32, 64)


def _bcast_rows(x, group, which):
    n, w = x.shape
    x3 = x.reshape(n // group, group, w)
    return jnp.broadcast_to(x3[:, which:which + 1, :], x3.shape).reshape(n, w)


def _level_row_masks(level, t_idx):
    pos = t_idx & (level - 1)
    return dict(odd=((t_idx >> int(math.log2(level))) & 1) == 1,
                at_least=[pos >= i for i in range(level)],
                at_most=[pos <= i for i in range(level)])


def _level_log_decay(level, lf, b_incl, masks, shifted):
    if level >= 8:
        since = b_incl - _bcast_rows(b_incl - lf, level, 0)
        until = _bcast_rows(b_incl, level, level - 1) - b_incl
        return jnp.where(masks["odd"], since, until)
    since = lf
    until = jnp.zeros_like(lf)
    for i in range(1, level):
        since = since + jnp.where(masks["at_least"][i], shifted(i), 0.0)
        until = until + jnp.where(masks["at_most"][level - 1 - i], shifted(-i), 0.0)
    return jnp.where(masks["odd"], since, until)


def _hgrn_prompt_kernel(q_ref, f_ref, i_ref, g_ref, bf_ref, lb_ref, ng_ref, o_ref, s_ref, st_ref,
                        *, nb, chunk, layer_idx):
    step = pl.program_id(0)

    @pl.when(step == 0)
    def _():
        st_ref[...] = jnp.zeros(st_ref.shape, F32)

    logits = [lb_ref[i:i + 1, :] for i in range(DEPTH)]
    mx = functools.reduce(jnp.maximum, logits)
    es = [jnp.exp(l - mx) for l in logits]
    tot = functools.reduce(lambda a, c: a + c, es)
    cum = []
    for e in es:
        cum.append(e / tot if not cum else cum[-1] + e / tot)
    lb = cum[layer_idx] - cum[0]

    row = lax.broadcasted_iota(jnp.int32, (chunk, chunk), 0)
    col = lax.broadcasted_iota(jnp.int32, (chunk, chunk), 1)
    tril = jnp.where(col <= row, 1.0, 0.0).astype(BF16)
    eye = row == col
    level_masks = []
    for level in HG_LEVELS:
        sh = int(math.log2(level))
        u, w = row >> sh, col >> sh
        level_masks.append(((u & 1) * 4096 + (u - w)) == 4097)
    t_idx = lax.broadcasted_iota(jnp.int32, (chunk, HG_DK), 0)
    row_masks = [_level_row_masks(level, t_idx) for level in HG_LEVELS]
    last = chunk - 1

    for b in range(nb):
        fg_all = lb + (1.0 - lb) * _sigmoid(f_ref[b] + bf_ref[...])
        lf_all = jnp.log(fg_all)
        b_all = _dot01_left(tril, lf_all)
        for h in range(HG_HEADS):
            sl = slice(h * HG_DK, (h + 1) * HG_DK)
            qx = q_ref[b, :, sl]
            q = qx * _sigmoid(qx)
            k = 1.0 - fg_all[:, sl]
            lf = lf_all[:, sl]
            bi = b_all[:, sl]
            v = i_ref[b, :, sl]
            vb = v.astype(BF16)
            a = jnp.where(eye, _dot_nt(q.astype(BF16), k.astype(BF16)), 0.0)
            rolled = {}

            def shifted(i, lf=lf, rolled=rolled):
                if i not in rolled:
                    rolled[i] = pltpu.roll(lf, i % chunk, 0)
                return rolled[i]

            for level, mask, rows in zip(HG_LEVELS, level_masks, row_masks):
                e = jnp.exp(_level_log_decay(level, lf, bi, rows, shifted))
                a = a + jnp.where(mask, _dot_nt((q * e).astype(BF16), (k * e).astype(BF16)), 0.0)
            st = st_ref[b, h]
            o = _dot(a.astype(BF16), vb) + _dot_nt((q * jnp.exp(bi)).astype(BF16), st.astype(BF16))
            b_end = bi[last:last + 1, :]
            kd = k * jnp.exp(b_end - bi)
            st_ref[b, h] = st * jnp.exp(b_end) + _dot(v.T.astype(BF16), kd.astype(BF16))
            o_ref[b, :, sl] = _rms_gate(o, ng_ref[:, sl], g_ref[b, :, sl])

    @pl.when(step == pl.num_programs(0) - 1)
    def _():
        for b in range(nb):
            for h in range(HG_HEADS):
                s_ref[b, h] = st_ref[b, h].T


def _hgrn_prompt(p, b_f, lb_logits, norm_g, nb, t, layer_idx):
    chunk = HG_CHUNK
    kernel = functools.partial(_hgrn_prompt_kernel, nb=nb, chunk=chunk, layer_idx=layer_idx)
    const2 = lambda c: (0, 0)
    blk = lambda j: pl.BlockSpec((nb, chunk, HG_W), lambda c: (0, c, j))
    return pl.pallas_call(
        kernel,
        grid=(t // chunk,),
        in_specs=[blk(0), blk(1), blk(2), blk(3),
                  pl.BlockSpec((1, HG_W), const2), pl.BlockSpec((DEPTH, HG_W), const2),
                  pl.BlockSpec((1, HG_W), const2)],
        out_specs=[pl.BlockSpec((nb, chunk, HG_W), lambda c: (0, c, 0)),
                   pl.BlockSpec((nb, HG_HEADS, HG_DK, HG_DK), lambda c: (0, 0, 0, 0))],
        out_shape=[jax.ShapeDtypeStruct((nb, t, HG_W), F32),
                   jax.ShapeDtypeStruct((nb, HG_HEADS, HG_DK, HG_DK), F32)],
        scratch_shapes=[pltpu.VMEM((nb, HG_HEADS, HG_DK, HG_DK), F32)],
        compiler_params=_params(1),
        name="hgrn_prompt",
    )(p, p, p, p, b_f, lb_logits, norm_g)


def _hgrn_sample_kernel(p_ref, bf_ref, lb_ref, ng_ref, s_ref, o_ref, s_out, *, layer_idx):
    logits = [lb_ref[i:i + 1, :] for i in range(DEPTH)]
    mx = functools.reduce(jnp.maximum, logits)
    es = [jnp.exp(l - mx) for l in logits]
    tot = functools.reduce(lambda a, c: a + c, es)
    cum = []
    for e in es:
        cum.append(e / tot if not cum else cum[-1] + e / tot)
    lb = cum[layer_idx] - cum[0]
    eye = (lax.broadcasted_iota(jnp.int32, (HG_DK, HG_DK), 0)
           == lax.broadcasted_iota(jnp.int32, (HG_DK, HG_DK), 1))
    for s in range(p_ref.shape[0]):
        qx = p_ref[s, :, 0:HG_W]
        q_all = qx * _sigmoid(qx)
        fg_all = lb + (1.0 - lb) * _sigmoid(p_ref[s, :, HG_W:2 * HG_W] + bf_ref[...])
        for h in range(HG_HEADS):
            sl = slice(h * HG_DK, (h + 1) * HG_DK)
            q = q_all[:, sl]
            fg = fg_all[:, sl]
            k = 1.0 - fg
            v = p_ref[s, :, 2 * HG_W + h * HG_DK:2 * HG_W + (h + 1) * HG_DK]
            gate = p_ref[s, :, 3 * HG_W + h * HG_DK:3 * HG_W + (h + 1) * HG_DK]
            s_old = s_ref[0, s, h]
            decay = jnp.exp(jnp.log(fg))
            a = jnp.sum(q * k, axis=1, keepdims=True)
            o = a * v + jnp.sum(s_old * _column(q * decay, eye), axis=0, keepdims=True)
            s_out[0, s, h] = _column(decay, eye) * s_old + _column(k, eye) * v
            o_ref[s, :, sl] = _rms_gate(o, ng_ref[:, sl], gate)


def _hgrn_sample(p, b_f, lb_logits, norm_g, s0, layer_idx):
    n = p.shape[0]
    row3 = lambda i: (i, 0, 0)
    const2 = lambda i: (0, 0)
    sb = STATE_BLOCK
    state = pl.BlockSpec((1, sb, HG_HEADS, HG_DK, HG_DK), lambda i: (0, i, 0, 0, 0))
    o, s1 = pl.pallas_call(
        functools.partial(_hgrn_sample_kernel, layer_idx=layer_idx),
        grid=(n // sb,),
        in_specs=[pl.BlockSpec((sb, 1, p.shape[1]), row3), pl.BlockSpec((1, HG_W), const2),
                  pl.BlockSpec((DEPTH, HG_W), const2), pl.BlockSpec((1, HG_W), const2), state],
        out_specs=[pl.BlockSpec((sb, 1, HG_W), row3), state],
        out_shape=[jax.ShapeDtypeStruct((n, 1, HG_W), F32), jax.ShapeDtypeStruct(s0.shape, F32)],
        compiler_params=_params(1),
        name="hgrn_sample",
    )(p[:, None, :], b_f, lb_logits, norm_g, s0)
    return o[:, 0, :], s1


def _band_kernel(*refs, n_heads, group, fold, n_blocks, with_sinks, with_lse):
    refs = list(refs)
    sink_ref = refs.pop(0) if with_sinks else None
    q_ref, ko_ref, kp_ref, vo_ref, vp_ref, o_ref = refs[:6]
    lse_ref = refs[6] if with_lse else None
    step = pl.program_id(2)
    per = SPAN // fold
    shift = int(math.log2(per))

    def local_pos(idx):
        return ((idx & (per - 1)) * fold) + (idx >> shift)

    qi = lax.broadcasted_iota(jnp.int32, (SPAN, 2 * SPAN), 0)
    ki = lax.broadcasted_iota(jnp.int32, (SPAN, 2 * SPAN), 1)
    qpos = SPAN + local_pos(qi)
    kpos = (ki & SPAN) + local_pos(ki & (SPAN - 1))
    band = (kpos <= qpos) & (kpos >= qpos - SPAN)
    first = jnp.where(step > 0, 0, SPAN)
    band_first = band & (kpos >= first)
    lo_q = lax.broadcasted_iota(jnp.int32, (SPAN, LANES), 1) < HEAD_DIM
    lo_kv = lax.broadcasted_iota(jnp.int32, (2 * SPAN, LANES), 1) < HEAD_DIM
    scale = HEAD_DIM ** -0.5

    def block_rows(ref, blk):
        v = ref[:, blk * per:(blk + 1) * per, :]
        return v.reshape(SPAN, v.shape[-1])

    for blk in range(n_blocks):
        valid = band_first if blk == 0 else band
        k_prev = _rows(kp_ref) if blk == 0 else block_rows(ko_ref, blk - 1)
        v_prev = _rows(vp_ref) if blk == 0 else block_rows(vo_ref, blk - 1)
        kcat = jnp.concatenate([k_prev, block_rows(ko_ref, blk)], axis=0)
        vcat = jnp.concatenate([v_prev, block_rows(vo_ref, blk)], axis=0)
        q_blk = block_rows(q_ref, blk)
        tiles = {}

        def kv_tile(name, src, kv_head, want_hi):
            key = (name, kv_head, want_hi)
            if key not in tiles:
                t = src[:, (kv_head // 2) * LANES:(kv_head // 2 + 1) * LANES]
                if (kv_head % 2 == 1) != want_hi:
                    t = pltpu.roll(t, HEAD_DIM, 1)
                tiles[key] = t
            return tiles[key]

        def softmax_parts(s, head):
            s = jnp.where(valid, s, NEG)
            m = jnp.max(s, axis=1, keepdims=True)
            if with_sinks:
                m = jnp.maximum(m, sink_ref[head])
            p = jnp.exp(s - m)
            l = jnp.sum(p, axis=1, keepdims=True)
            if with_sinks:
                l = l + jnp.exp(sink_ref[head] - m)
            return p.astype(BF16), m, l

        for j in range(n_heads // 2):
            sl = slice(j * LANES, (j + 1) * LANES)
            qpair = q_blk[:, sl] * scale
            h_lo, h_hi = 2 * j, 2 * j + 1
            q_lo = jnp.where(lo_q, qpair, 0.0).astype(BF16)
            q_hi = jnp.where(lo_q, 0.0, qpair).astype(BF16)
            s_lo = _dot_nt(q_lo, kv_tile("k", kcat, h_lo // group, False).astype(BF16))
            s_hi = _dot_nt(q_hi, kv_tile("k", kcat, h_hi // group, True).astype(BF16))
            p_lo, m_lo, l_lo = softmax_parts(s_lo, h_lo)
            p_hi, m_hi, l_hi = softmax_parts(s_hi, h_hi)
            v_lo = jnp.where(lo_kv, kv_tile("v", vcat, h_lo // group, False), 0.0).astype(BF16)
            v_hi = jnp.where(lo_kv, 0.0, kv_tile("v", vcat, h_hi // group, True)).astype(BF16)
            o = (_dot(p_lo, v_lo) + _dot(p_hi, v_hi)) / jnp.where(lo_q, l_lo, l_hi)
            o_ref[:, blk * per:(blk + 1) * per, sl] = o.reshape(fold, per, LANES)
            if with_lse:
                lse = jnp.where(lo_q, m_lo + jnp.log(l_lo), m_hi + jnp.log(l_hi))
                lse_ref[:, blk * per:(blk + 1) * per, sl] = lse.reshape(fold, per, LANES)


BAND_BLOCKS = 4


def _band_call(name, arrays, sinks, fold, classes, rows, q_col, k_col, v_col, q_width, kv_width,
               n_heads, group, with_lse):
    seqs = arrays.shape[0]
    per = SPAN // fold
    steps = rows // (per * BAND_BLOCKS)
    own = lambda width, col: pl.BlockSpec((None, fold, None, per * BAND_BLOCKS, width),
                                          lambda b, c, n: (b, 0, c, n, col))
    prev = lambda width, col: pl.BlockSpec(
        (None, fold, None, per, width),
        lambda b, c, n: (b, 0, c, jnp.maximum(n * BAND_BLOCKS - 1, 0), col))
    out = pl.BlockSpec((None, fold, None, per * BAND_BLOCKS, q_width), lambda b, c, n: (b, 0, c, n, 0))
    shape = jax.ShapeDtypeStruct((seqs, fold, classes, rows, q_width), F32)
    in_specs = [own(q_width, q_col), own(kv_width, k_col), prev(kv_width, k_col),
                own(kv_width, v_col), prev(kv_width, v_col)]
    args = [arrays] * 5
    if sinks is not None:
        in_specs = [pl.BlockSpec(memory_space=pltpu.SMEM)] + in_specs
        args = [sinks] + args
    return pl.pallas_call(
        functools.partial(_band_kernel, n_heads=n_heads, group=group, fold=fold, n_blocks=BAND_BLOCKS,
                          with_sinks=sinks is not None, with_lse=with_lse),
        grid=(seqs, classes, steps),
        in_specs=in_specs,
        out_specs=[out, out] if with_lse else out,
        out_shape=[shape, shape] if with_lse else shape,
        compiler_params=_params(3),
        name=name,
    )(*args)


def _dil_prompt_group(p, g):
    _, dil = DIL_GROUPS[g]
    seqs, _, rows, width = p.shape
    fold = RESIDUES // dil
    w = DIL_HEADS * HEAD_DIM
    o, lse = _band_call("dil_prompt_g%d" % g, p.reshape(seqs, fold, dil, rows, width), None, fold, dil, rows,
                        3 * g, 3 * g + 1, 3 * g + 2, w, w, DIL_HEADS, 1, True)
    return o.reshape(seqs, RESIDUES, rows, w), lse.reshape(seqs, RESIDUES, rows, w)


def _swa_prompt(p, sinks, nb, t):
    kcol = SWA_Q // SWA_KV
    o = _band_call("swa_prompt", p.reshape(nb, 1, 1, t, p.shape[-1]), sinks, 1, 1, t,
                   0, kcol, kcol + 1, SWA_Q, SWA_KV, SWA_HEADS, SWA_HEADS // SWA_KV_HEADS, False)
    return o.reshape(nb * t, SWA_Q)


def _decode_roll_kernel(*refs, hb, with_sinks, **static):
    refs = list(refs)
    sink_ref = refs.pop(0) if with_sinks else None
    _decode_block(pl.program_id(1) * hb, sink_ref, *refs, hb=hb, with_sinks=with_sinks, **static)


def _decode_block(h0, sink_ref, q_ref, kn_ref, vn_ref, c_ref, o_ref, lse_ref, cout_ref,
                  *, nb, hb, group, length, dil, with_sinks):
    eye = (lax.broadcasted_iota(jnp.int32, (HEAD_DIM, HEAD_DIM), 0)
           == lax.broadcasted_iota(jnp.int32, (HEAD_DIM, HEAD_DIM), 1))
    pos = lax.broadcasted_iota(jnp.int32, (1, length), 1)
    attended = (pos & (dil - 1)) == 0
    is_last = lax.broadcasted_iota(jnp.int32, (HEAD_DIM, length), 1) == length - 1
    scale = HEAD_DIM ** -0.5
    for i in range(nb if group == 1 else 0):
        kt = c_ref[0, i, 0]
        vt = c_ref[0, i, 1]
        row = lambda ref: jnp.stack([ref[i, pl.ds(h0 + j, 1), :] for j in range(hb)])
        q, kn, vn = row(q_ref) * scale, row(kn_ref), row(vn_ref)
        column = lambda r: jnp.sum(jnp.where(eye[None], r, 0.0), axis=2, keepdims=True)
        s = jnp.sum(kt * column(q), axis=1, keepdims=True)
        s = jnp.where(attended[None], s, NEG)
        s_new = jnp.sum(q * kn, axis=2, keepdims=True)
        m = jnp.maximum(jnp.max(s, axis=2, keepdims=True), s_new)
        p = jnp.exp(s - m)
        p_new = jnp.exp(s_new - m)
        l = jnp.sum(p, axis=2, keepdims=True) + p_new
        acc_col = jnp.sum(vt * p, axis=2, keepdims=True)
        acc = jnp.sum(jnp.where(eye[None], acc_col, 0.0), axis=1, keepdims=True)
        out = (acc + p_new * vn) / l
        lse = jnp.broadcast_to(m + jnp.log(l), out.shape)
        for j in range(hb):
            o_ref[i, pl.ds(h0 + j, 1), :] = out[j]
            lse_ref[i, pl.ds(h0 + j, 1), :] = lse[j]
        cout_ref[0, i, 0] = jnp.where(is_last[None], column(kn), pltpu.roll(kt, length - 1, 2))
        cout_ref[0, i, 1] = jnp.where(is_last[None], column(vn), pltpu.roll(vt, length - 1, 2))
    for i in range(nb if group > 1 else 0):
        for j in range(hb):
            kt = c_ref[0, i, 0, j]
            vt = c_ref[0, i, 1, j]
            kn = kn_ref[i, pl.ds(h0 + j, 1), :]
            vn = vn_ref[i, pl.ds(h0 + j, 1), :]
            rows = pl.ds((h0 + j) * group, group)
            q = q_ref[i, rows, :] * scale
            s = _dot(q.astype(BF16), kt.astype(BF16))
            s = jnp.where(attended, s, NEG)
            s_new = jnp.sum(q * kn, axis=1, keepdims=True)
            m = jnp.maximum(jnp.max(s, axis=1, keepdims=True), s_new)
            if with_sinks:
                sink = sink_ref[rows, :]
                m = jnp.maximum(m, sink)
            p = jnp.exp(s - m)
            p_new = jnp.exp(s_new - m)
            l = jnp.sum(p, axis=1, keepdims=True) + p_new
            if with_sinks:
                l = l + jnp.exp(sink - m)
            acc = _dot_nt(p.astype(BF16), vt.astype(BF16))
            o_ref[i, rows, :] = (acc + p_new * vn) / l
            lse_ref[i, rows, :] = jnp.broadcast_to(m + jnp.log(l), (group, HEAD_DIM))
            cout_ref[0, i, 0, j] = jnp.where(is_last, _column(kn, eye), pltpu.roll(kt, length - 1, 1))
            cout_ref[0, i, 1, j] = jnp.where(is_last, _column(vn, eye), pltpu.roll(vt, length - 1, 1))


def _decode_roll(q, k_new, v_new, cache, dil, nb, hb, sinks=None):
    _, n, length, _, kv_heads, d = cache.shape
    heads = q.shape[1]
    group = heads // kv_heads
    cache_t = jnp.transpose(cache, (0, 1, 3, 4, 5, 2))
    cblock = pl.BlockSpec((1, nb, 2, hb, d, length), lambda i, h: (0, i, 0, h, 0, 0))
    per_seq = lambda width: pl.BlockSpec((nb, width, d), lambda i, h: (i, 0, 0))
    in_specs = [per_seq(heads), per_seq(kv_heads), per_seq(kv_heads), cblock]
    args = [q, k_new, v_new, cache_t]
    if sinks is not None:
        in_specs = [pl.BlockSpec((heads, 1), lambda i, h: (0, 0))] + in_specs
        args = [sinks] + args
    o, lse, new_t = pl.pallas_call(
        functools.partial(_decode_roll_kernel, nb=nb, hb=hb, group=group, length=length, dil=dil,
                          with_sinks=sinks is not None),
        grid=(n // nb, kv_heads // hb),
        in_specs=in_specs,
        out_specs=[per_seq(heads), per_seq(heads), cblock],
        out_shape=[jax.ShapeDtypeStruct(q.shape, F32), jax.ShapeDtypeStruct(q.shape, F32),
                   jax.ShapeDtypeStruct(cache_t.shape, F32)],
        compiler_params=_params(2),
        name="decode_roll",
    )(*args)
    return o, lse, jnp.transpose(new_t, (0, 1, 5, 2, 3, 4))


PROMPT_TM = 512
LN_TM = 256
DIL_DECODE_BLOCKS = ((8, 8), (2, 8))
RIDE_HEADS = 4
SWA_DECODE_BLOCK = (16, 2)


def _pad_cols(w, width):
    return jnp.pad(w, ((0, 0), (0, width - w.shape[1])))


def kernel(x_prompt, x_sample, state_mlstm_C, state_mlstm_n, state_mlstm_m, cache_dil_kv0, cache_dil_kv1, cache_dil_kv2, state_hgrn_S, cache_swa_kv, mlstm_w_in, mlstm_b_gates, mlstm_norm_g, mlstm_w_out, dil_w_in, dil_w_out, hgrn_w_in, hgrn_b_f, hgrn_lb_logits, hgrn_norm_g, hgrn_w_out, swa_w_in, swa_sinks, swa_w_out, ln1_g, ln1_b, ln2_g, ln2_b, mlp_w1, mlp_w2):
    nb, t, _ = x_prompt.shape
    ns = x_sample.shape[0]
    assert x_sample.shape[1] == 1 and DEPTH == 4
    hp = x_prompt.reshape(nb * t, D_MODEL)
    hs = x_sample.reshape(ns, D_MODEL)
    cos_p, sin_p = _rope_tables(jnp.arange(t, dtype=jnp.int32))
    cos_s, sin_s = _rope_tables(jnp.full((ns,), PAST_LEN, dtype=jnp.int32))

    ln = lambda i: (ln1_g[i][None], ln1_b[i][None], ln2_g[i][None], ln2_b[i][None])
    mlp = lambda i: (mlp_w1[i].astype(BF16), mlp_w2[i].astype(BF16))
    w_out = [w[0].astype(BF16) for w in (mlstm_w_out, dil_w_out, hgrn_w_out, swa_w_out)]

    w_in = mlstm_w_in[0]
    main = 2 * ML_QK + 2 * ML_VD
    w_mlstm = w_in[:, :main].astype(BF16)
    w_gate = jnp.concatenate([_pad_cols(w_in[:, main:main + ML_HEADS], LANES),
                              _pad_cols(w_in[:, main + ML_HEADS:], LANES)], axis=1).astype(BF16)
    w_dil = dil_w_in[0].astype(BF16)
    dil_qk = tuple((g * DIL_GW, g * DIL_GW + 2 * DIL_HEADS * HEAD_DIM) for g in range(len(DIL_GROUPS)))
    w_hgrn = hgrn_w_in[0].astype(BF16)
    w_swa = swa_w_in[0].astype(BF16)
    swa_qk = ((0, SWA_Q + SWA_KV),)
    norm_g = mlstm_norm_g[0][None]
    b_f, ng = hgrn_b_f[0][None], hgrn_norm_g[0][None]


    ps = _proj(hs, w_mlstm, ns)
    gs = _proj(hs, w_gate, ns)
    mix_s, c_s, n_s, m_s = _mlstm_sample(ps, gs, mlstm_b_gates[0], norm_g,
                                         state_mlstm_C, state_mlstm_n, state_mlstm_m)
    hs = _layer_tail(hs, *mlp(0), *ln(0)[2:], ns, mix_s, w_out[0], *ln(0)[:2])
    qkv_s = _proj(hs, w_dil, ns, dil_qk, (cos_s, sin_s)).reshape(ns, 9, DIL_HEADS, HEAD_DIM)

    big = len(DIL_GROUPS) - 1
    big_win, big_dil = DIL_GROUPS[big]
    assert cache_dil_kv2.shape[2] == big_win and big_win // big_dil == SPAN
    big_t = jnp.transpose(cache_dil_kv2, (0, 1, 3, 4, 5, 2))
    tail_steps = nb * t // LN_TM
    assert DEPTH * tail_steps == ns * (DIL_HEADS // RIDE_HEADS)
    ride = lambda layer, partial: dict(q=qkv_s[:, 3 * big], k_new=qkv_s[:, 3 * big + 1],
                                       v_new=qkv_s[:, 3 * big + 2], cache_t=big_t, hb=RIDE_HEADS,
                                       dil=big_dil, first=layer * tail_steps, partial=partial)
    big_o, big_lse = [], []

    def prompt_tail(layer, hp, mix, partial, shuffle=None):
        g1, b1, g2, b2 = ln(layer)
        with_mix = (mix, w_out[layer], g1, b1) if mix is not None else ()
        hp, o, lse, partial = _layer_tail(hp, *mlp(layer), g2, b2, LN_TM, *with_mix, shuffle=shuffle,
                                          seq_len=t, side=ride(layer, partial))
        big_o.append(o)
        big_lse.append(lse)
        return hp, partial

    pp = _proj(hp, w_mlstm, PROMPT_TM)
    gp = _proj(hp, w_gate, PROMPT_TM)
    mix_p, c_p, n_p, m_p = _mlstm_prompt(pp.reshape(nb, t, main), gp.reshape(nb, t, 2 * LANES),
                                         mlstm_b_gates[0], norm_g, nb, t)
    out_c_p = c_p[None]
    out_n_p = n_p[:, :, 0, :][None]
    out_m_p = m_p[:, :, 0, 0][None]
    hp, rolled = prompt_tail(0, hp, mix_p.reshape(nb * t, ML_VD), None, shuffle="to_residue")

    by_residue = lambda tab: jnp.transpose(tab.reshape(t // RESIDUES, RESIDUES, LANES), (1, 0, 2))
    pp = _proj(hp, w_dil, PROMPT_TM, dil_qk, (by_residue(cos_p), by_residue(sin_p)), by_residue=True)
    outs, lses, dil_kv_p = [], [], []
    for g, (win, _) in enumerate(DIL_GROUPS):
        o, lse = _dil_prompt_group(pp, g)
        outs.append(o)
        lses.append(lse)
        kv = pp[:, :, (t - win) // RESIDUES:, g * DIL_GW + DIL_HEADS * HEAD_DIM:(g + 1) * DIL_GW]
        kv = jnp.transpose(kv, (0, 2, 1, 3))
        dil_kv_p.append(kv.reshape(nb, win, 2, DIL_HEADS, HEAD_DIM)[None])
    hp = _dil_out_ln(outs, lses, w_out[1], hp, *ln(1)[:2], LN_TM, by_residue=True)
    hp, rolled = prompt_tail(1, hp, None, rolled, shuffle="to_natural")

    pp = _proj(hp, w_hgrn, PROMPT_TM)
    mix_p, s_p = _hgrn_prompt(pp.reshape(nb, t, 4 * HG_W), b_f, hgrn_lb_logits, ng, nb, t, 2)
    hp, rolled = prompt_tail(2, hp, mix_p.reshape(nb * t, HG_W), rolled)

    pp = _proj(hp, w_swa, PROMPT_TM, swa_qk, (cos_p, sin_p)).reshape(nb, t, -1)
    mix_p = _swa_prompt(pp, swa_sinks[0], nb, t)
    swa_kv_p = pp[:, t - SPAN:, SWA_Q:].reshape(nb, SPAN, 2, SWA_KV_HEADS, HEAD_DIM)[None]
    hp, rolled = prompt_tail(3, hp, mix_p, rolled)

    outs_s, lses_s, dil_kv_s = [], [], []
    for g, cache in enumerate((cache_dil_kv0, cache_dil_kv1)):
        win, dil = DIL_GROUPS[g]
        assert cache.shape[2] == win and win // dil == SPAN
        o, lse, new_cache = _decode_roll(qkv_s[:, 3 * g], qkv_s[:, 3 * g + 1], qkv_s[:, 3 * g + 2],
                                         cache, dil, *DIL_DECODE_BLOCKS[g])
        outs_s.append(o)
        lses_s.append(lse)
        dil_kv_s.append(new_cache)
    outs_s.append(jnp.concatenate(big_o, axis=0))
    lses_s.append(jnp.concatenate(big_lse, axis=0))
    dil_kv_s.append(jnp.transpose(rolled, (0, 1, 5, 2, 3, 4)))
    flat = lambda a: a.reshape(ns, DIL_HEADS * HEAD_DIM)
    hs = _dil_out_ln([flat(o) for o in outs_s], [flat(l) for l in lses_s], w_out[1], hs, *ln(1)[:2], ns)
    hs = _layer_tail(hs, *mlp(1), *ln(1)[2:], ns)

    ps = _proj(hs, w_hgrn, ns)
    mix_s, s_s = _hgrn_sample(ps, b_f, hgrn_lb_logits, ng, state_hgrn_S, 2)
    hs = _layer_tail(hs, *mlp(2), *ln(2)[2:], ns, mix_s, w_out[2], *ln(2)[:2])

    ps = _proj(hs, w_swa, ns, swa_qk, (cos_s, sin_s))
    q_s = ps[:, :SWA_Q].reshape(ns, SWA_HEADS, HEAD_DIM)
    kv_s = ps[:, SWA_Q:].reshape(ns, 2, SWA_KV_HEADS, HEAD_DIM)
    assert cache_swa_kv.shape[2] == SPAN
    mix_s, _, swa_kv_s = _decode_roll(q_s, kv_s[:, 0], kv_s[:, 1], cache_swa_kv, 1, *SWA_DECODE_BLOCK,
                                      sinks=swa_sinks[0][:, None])
    hs = _layer_tail(hs, *mlp(3), *ln(3)[2:], ns, mix_s.reshape(ns, SWA_Q), w_out[3], *ln(3)[:2])

    return (hp.reshape(nb, t, D_MODEL), hs.reshape(ns, 1, D_MODEL),
            out_c_p, c_s, out_n_p, n_s, out_m_p, m_s,
            dil_kv_p[0], dil_kv_s[0], dil_kv_p[1], dil_kv_s[1], dil_kv_p[2], dil_kv_s[2],
            s_p[None], s_s, swa_kv_p, swa_kv_s)
```

```python
import functools
import math
from typing import Callable, NamedTuple

import numpy as np
import jax
import jax.numpy as jnp
from jax import lax
from jax.experimental import pallas as pl
from jax.experimental.pallas import tpu as pltpu

F32 = jnp.float32
BF16 = jnp.bfloat16

D_MODEL = 1024
DEPTH = 4
PAST_LEN = 8192
D_FF = 4 * D_MODEL
ALPHA = (2 * DEPTH) ** 0.25
LN_EPS = 1e-5
NORM_EPS = 1e-6
ROPE_THETA = 10000.0

ML_HEADS = 4
ML_DK = 128
ML_DV = 256
ML_QK = ML_HEADS * ML_DK
ML_VD = ML_HEADS * ML_DV
ML_CHUNK = 256

DIL_GROUPS = ((128, 1), (512, 4), (2048, 16))
DIL_HEADS = 8
HEAD_DIM = 64
DIL_GW = 3 * DIL_HEADS * HEAD_DIM
SPAN = 128

HG_HEADS = 8
HG_DK = 128
HG_CHUNK = 128
HG_W = HG_HEADS * HG_DK

SWA_HEADS = 16
SWA_KV_HEADS = 2
SWA_Q = SWA_HEADS * HEAD_DIM
SWA_KV = SWA_KV_HEADS * HEAD_DIM

LANES = 128
NEG = -1e30
VMEM_LIMIT = 48 * 1024 * 1024
HOST_VMEM_LIMIT = 58 * 1024 * 1024


def _params(n_axes, vmem=VMEM_LIMIT):
    return pltpu.CompilerParams(dimension_semantics=("arbitrary",) * n_axes, vmem_limit_bytes=vmem)


class SideJob(NamedTuple):
    in_specs: tuple
    args: tuple
    out_specs: tuple
    out_shapes: tuple
    body: Callable
    aliases: dict


def _host_kernel(*refs, own, n_in, n_out, sides):
    refs = list(refs)
    n_all_in = n_in + sum(job_in for job_in, _, _ in sides)
    n_all_out = n_out + sum(job_out for _, job_out, _ in sides)
    ins, outs, scratch = refs[:n_all_in], refs[n_all_in:n_all_in + n_all_out], refs[n_all_in + n_all_out:]
    at_in, at_out = n_in, n_out
    for job_in, job_out, body in sides:
        body(*ins[at_in:at_in + job_in], *outs[at_out:at_out + job_out])
        at_in += job_in
        at_out += job_out
    own(*ins[:n_in], *outs[:n_out], *scratch)


def _call(own, steps, in_specs, args, out_specs, out_shapes, name, scratch_shapes=(), sides=(),
          vmem=VMEM_LIMIT):
    all_in, all_args = list(in_specs), list(args)
    all_out, all_shapes = list(out_specs), list(out_shapes)
    aliases = {}
    for job in sides:
        for src, dst in job.aliases.items():
            aliases[len(all_args) + src] = len(all_out) + dst
        all_in += job.in_specs
        all_args += job.args
        all_out += job.out_specs
        all_shapes += job.out_shapes
    static = tuple((len(job.in_specs), len(job.out_specs), job.body) for job in sides)
    return pl.pallas_call(
        functools.partial(_host_kernel, own=own, n_in=len(in_specs), n_out=len(out_specs), sides=static),
        grid=(steps,),
        in_specs=all_in,
        out_specs=all_out,
        out_shape=all_shapes,
        input_output_aliases=aliases,
        scratch_shapes=list(scratch_shapes),
        compiler_params=_params(1, vmem),
        name=name,
    )(*all_args)


def _dot(a, b):
    return jnp.dot(a, b, preferred_element_type=F32)


def _dot_nt(a, b):
    return lax.dot_general(a, b, (((1,), (1,)), ((), ())), preferred_element_type=F32)


def _split3(x):
    hi = x.astype(BF16)
    r1 = x - hi.astype(F32)
    mid = r1.astype(BF16)
    lo = (r1 - mid.astype(F32)).astype(BF16)
    return hi, mid, lo


def _dot01_left(a01, x):
    hi, mid, lo = _split3(x)
    return _dot(a01, hi) + _dot(a01, mid) + _dot(a01, lo)


def _dot01_right(x, a01):
    hi, mid, lo = _split3(x)
    return _dot(hi, a01) + _dot(mid, a01) + _dot(lo, a01)


def _sigmoid(x):
    return 0.5 * jnp.tanh(0.5 * x) + 0.5


def _log_sigmoid(x):
    return jnp.minimum(x, 0.0) - jnp.log1p(jnp.exp(-jnp.abs(x)))


def _layer_norm(z, g, b):
    mu = jnp.mean(z, axis=-1, keepdims=True)
    zc = z - mu
    var = jnp.mean(zc * zc, axis=-1, keepdims=True)
    return zc * lax.rsqrt(var + LN_EPS) * g + b


def _rms_gate(h, g, gate):
    ms = jnp.mean(h * h, axis=-1, keepdims=True)
    return h * lax.rsqrt(ms + NORM_EPS) * g * _sigmoid(gate)


PROJ_CHUNK = 512
RESIDUES = 16


def _row_spec(width, tm, residue_tiles=None):
    if residue_tiles is None:
        return pl.BlockSpec((tm, width), lambda i: (i, 0))
    return pl.BlockSpec((None, RESIDUES, tm // RESIDUES, width),
                        lambda i: (i // residue_tiles, 0, i % residue_tiles, 0))


def _rows(ref):
    v = ref[...]
    return v.reshape(-1, v.shape[-1])


def _proj_kernel(*refs, plan):
    if any(rope for _, _, rope in plan):
        x_ref, w_ref, cos_ref, sin_ref, o_ref = refs
    else:
        x_ref, w_ref, o_ref = refs
    xb = _rows(x_ref).astype(BF16)
    for start, width, rope in plan:
        acc = _dot(xb, w_ref[:, start:start + width])
        if rope:
            reps = width // LANES
            cos = jnp.concatenate([_rows(cos_ref)] * reps, axis=1)
            sin = jnp.concatenate([_rows(sin_ref)] * reps, axis=1)
            lane = lax.broadcasted_iota(jnp.int32, acc.shape, 1)
            first_half = (lane & (HEAD_DIM - 1)) < HEAD_DIM // 2
            partner = jnp.where(first_half, pltpu.roll(acc, width - HEAD_DIM // 2, 1),
                                pltpu.roll(acc, HEAD_DIM // 2, 1))
            acc = acc * cos + partner * sin
        o_ref[..., start:start + width] = acc.reshape(o_ref.shape[:-1] + (width,))


def _proj(x, w_bf, tm, rope_cols=(), tables=None, by_residue=False, sides=()):
    k, n = w_bf.shape
    plan = []
    start = 0
    while start < n:
        rope = any(lo <= start < hi for lo, hi in rope_cols)
        limit = min([hi for lo, hi in rope_cols if lo <= start < hi] +
                    [lo for lo, hi in rope_cols if lo > start] + [n])
        width = min(PROJ_CHUNK, limit - start)
        plan.append((start, width, rope))
        start += width
    if by_residue:
        seqs, _, per_residue, _ = x.shape
        tiles = per_residue * RESIDUES // tm
        steps = seqs * tiles
        tab = pl.BlockSpec((RESIDUES, tm // RESIDUES, LANES), lambda i: (0, i % tiles, 0))
    else:
        tiles = None
        steps = x.shape[0] // tm
        if rope_cols:
            t_blocks = tables[0].shape[0] // tm
            tab = pl.BlockSpec((tm, LANES), lambda i: (i % t_blocks, 0))
    in_specs = [_row_spec(k, tm, tiles),
                pl.BlockSpec((k, n), lambda i: (0, 0), pipeline_mode=pl.Buffered(1))]
    args = [x, w_bf]
    if rope_cols:
        in_specs += [tab, tab]
        args += list(tables)
    results = _call(functools.partial(_proj_kernel, plan=tuple(plan)), steps, in_specs, args,
                    [_row_spec(n, tm, tiles)], [jax.ShapeDtypeStruct(x.shape[:-1] + (n,), F32)], "proj",
                    sides=sides, vmem=HOST_VMEM_LIMIT if sides else VMEM_LIMIT)
    return results if sides else results[0]


def _rope_tables(pos):
    half = HEAD_DIM // 2
    inv_freq = jnp.power(ROPE_THETA, -jnp.arange(half, dtype=F32) / half)
    ang = pos.astype(F32)[:, None] * inv_freq[None, :]
    cos = jnp.cos(ang)
    sin = jnp.sin(ang)
    cos = jnp.concatenate([cos, cos, cos, cos], axis=1)
    sin = jnp.concatenate([-sin, sin, -sin, sin], axis=1)
    return cos, sin


def _dil_out_ln_kernel(o0, o1, o2, l0, l1, l2, w_ref, x_ref, g_ref, b_ref, y_ref):
    a0, a1, a2 = _rows(l0), _rows(l1), _rows(l2)
    mx = jnp.maximum(jnp.maximum(a0, a1), a2)
    e0, e1, e2 = jnp.exp(a0 - mx), jnp.exp(a1 - mx), jnp.exp(a2 - mx)
    y = (e0 * _rows(o0) + e1 * _rows(o1) + e2 * _rows(o2)) / (e0 + e1 + e2)
    z = ALPHA * _rows(x_ref) + _dot(y.astype(BF16), w_ref[...])
    y_ref[...] = _layer_norm(z, g_ref[...], b_ref[...]).reshape(y_ref.shape)


def _dil_out_ln(outs, lses, w_bf, x, g, b, tm, by_residue=False):
    kin = outs[0].shape[-1]
    if by_residue:
        tiles = x.shape[2] * RESIDUES // tm
        steps = x.shape[0] * tiles
    else:
        tiles = None
        steps = x.shape[0] // tm
    const = lambda i: (0, 0)
    return pl.pallas_call(
        _dil_out_ln_kernel,
        grid=(steps,),
        in_specs=[_row_spec(kin, tm, tiles)] * 6 + [
            pl.BlockSpec((kin, D_MODEL), const), _row_spec(D_MODEL, tm, tiles),
            pl.BlockSpec((1, D_MODEL), const), pl.BlockSpec((1, D_MODEL), const)],
        out_specs=_row_spec(D_MODEL, tm, tiles),
        out_shape=jax.ShapeDtypeStruct(x.shape, F32),
        compiler_params=_params(1),
        name="dil_out_ln",
    )(*outs, *lses, w_bf, x, g, b)


FF_CHUNK = 1024


def _tail_kernel(*refs, with_mix, shuffle):
    refs = list(refs)
    scratch = refs.pop() if shuffle else None
    y_ref = refs.pop()
    if with_mix:
        mix_ref, wo_ref, x_ref, g1_ref, b1_ref = refs[:5]
        refs = refs[5:]
        z = ALPHA * _rows(x_ref) + _dot(_rows(mix_ref).astype(BF16), wo_ref[...])
        x = _layer_norm(z, g1_ref[...], b1_ref[...])
    else:
        x = _rows(refs.pop(0))
    w1_ref, w2_ref, g2_ref, b2_ref = refs
    xb = x.astype(BF16)
    acc = jnp.zeros(x.shape, F32)
    for c in range(D_FF // FF_CHUNK):
        a = _dot(xb, w1_ref[:, c * FF_CHUNK:(c + 1) * FF_CHUNK])
        a = jnp.square(jnp.maximum(a, 0.0)).astype(BF16)
        acc = acc + _dot(a, w2_ref[c * FF_CHUNK:(c + 1) * FF_CHUNK, :])
    y = _layer_norm(ALPHA * x + acc, g2_ref[...], b2_ref[...])
    per = y.shape[0] // RESIDUES
    lane_tiles = [slice(c * LANES, (c + 1) * LANES) for c in range(D_MODEL // LANES)]
    if shuffle == "to_residue":
        for c, cols in enumerate(lane_tiles):
            scratch[c] = y[:, cols]
        for r in range(RESIDUES):
            for c, cols in enumerate(lane_tiles):
                y_ref[r, :, cols] = scratch[c, pl.ds(r, per, stride=RESIDUES), :]
    elif shuffle == "to_natural":
        for r in range(RESIDUES):
            for c, cols in enumerate(lane_tiles):
                scratch[c, pl.ds(r, per, stride=RESIDUES), :] = y[r * per:(r + 1) * per, cols]
        for c, cols in enumerate(lane_tiles):
            y_ref[:, cols] = scratch[c]
    else:
        y_ref[...] = y


def _layer_tail(x, w1_bf, w2_bf, g2, b2, tm, mix=None, w_out_bf=None, g1=None, b1=None,
                shuffle=None, seq_len=None, sides=()):
    const = lambda i: (0, 0)
    resident = pl.Buffered(1)
    in_tiles = out_tiles = None
    out_shape = x.shape
    if shuffle == "to_natural":
        in_tiles = x.shape[2] * RESIDUES // tm
        steps = x.shape[0] * in_tiles
        out_shape = (x.shape[0] * x.shape[1] * x.shape[2], D_MODEL)
    else:
        steps = x.shape[0] // tm
        if shuffle == "to_residue":
            out_tiles = seq_len // tm
            out_shape = (x.shape[0] // seq_len, RESIDUES, seq_len // RESIDUES, D_MODEL)
    in_specs, args = [], []
    if mix is not None:
        kin = mix.shape[-1]
        in_specs += [_row_spec(kin, tm, in_tiles),
                     pl.BlockSpec((kin, D_MODEL), const, pipeline_mode=resident)]
        args += [mix, w_out_bf]
    in_specs.append(_row_spec(D_MODEL, tm, in_tiles))
    args.append(x)
    if mix is not None:
        in_specs += [pl.BlockSpec((1, D_MODEL), const)] * 2
        args += [g1, b1]
    in_specs += [pl.BlockSpec((D_MODEL, D_FF), const, pipeline_mode=resident),
                 pl.BlockSpec((D_FF, D_MODEL), const, pipeline_mode=resident),
                 pl.BlockSpec((1, D_MODEL), const), pl.BlockSpec((1, D_MODEL), const)]
    args += [w1_bf, w2_bf, g2, b2]
    scratch_shapes = [pltpu.VMEM((D_MODEL // LANES, tm, LANES), F32)] if shuffle else []
    results = _call(functools.partial(_tail_kernel, with_mix=mix is not None, shuffle=shuffle), steps,
                    in_specs, args, [_row_spec(D_MODEL, tm, out_tiles)],
                    [jax.ShapeDtypeStruct(out_shape, F32)], "layer_tail", scratch_shapes, sides,
                    HOST_VMEM_LIMIT if sides else VMEM_LIMIT)
    return results if sides else results[0]


def _mlstm_prompt_kernel(q_ref, k_ref, v_ref, og_ref, gc_ref, gr_ref, brow_ref, bcol_ref, g_ref,
                         h_ref, c_ref, n_ref, m_ref, *, nb, chunk):
    step = pl.program_id(0)

    @pl.when(step == 0)
    def _():
        c_ref[...] = jnp.zeros(c_ref.shape, F32)
        n_ref[...] = jnp.zeros(n_ref.shape, F32)
        m_ref[...] = jnp.zeros(m_ref.shape, F32)

    row = lax.broadcasted_iota(jnp.int32, (chunk, chunk), 0)
    col = lax.broadcasted_iota(jnp.int32, (chunk, chunk), 1)
    causal = col <= row
    tril = jnp.where(causal, 1.0, 0.0).astype(BF16)
    triu = jnp.where(row <= col, 1.0, 0.0).astype(BF16)
    scale = ML_DK ** -0.5
    last = chunk - 1

    for b in range(nb):
        gc = gc_ref[b] + brow_ref[...]
        ig_cols = gc[:, :LANES]
        b_cols = _dot01_left(tril, _log_sigmoid(gc[:, LANES:]))
        gr = gr_ref[b] + bcol_ref[...]
        b_rows = _dot01_right(_log_sigmoid(gr), triu)
        for h in range(ML_HEADS):
            bc = b_cols[:, h:h + 1]
            igc = ig_cols[:, h:h + 1]
            br = b_rows[ML_HEADS + h:ML_HEADS + h + 1, :]
            igr = gr[h:h + 1, :]
            m_prev = m_ref[b, h][:, 0:1]
            log_intra = jnp.where(causal, bc - br + igr, NEG)
            log_prev = bc + m_prev
            m_t = jnp.maximum(log_prev, jnp.max(log_intra, axis=1, keepdims=True))
            w_intra = jnp.exp(log_intra - m_t)
            w_prev = jnp.exp(log_prev - m_t)
            q = q_ref[b, :, h * ML_DK:(h + 1) * ML_DK]
            k = k_ref[b, :, h * ML_DK:(h + 1) * ML_DK] * scale
            v = v_ref[b, :, h * ML_DV:(h + 1) * ML_DV]
            qb, kb, vb = q.astype(BF16), k.astype(BF16), v.astype(BF16)
            a = _dot_nt(qb, kb) * w_intra
            c_old = c_ref[b, h]
            n_old = n_ref[b, h]
            num = _dot(a.astype(BF16), vb) + w_prev * _dot(qb, c_old.astype(BF16))
            den = jnp.sum(a, axis=1, keepdims=True) + w_prev * jnp.sum(q * n_old, axis=1, keepdims=True)
            hh = num / jnp.maximum(jnp.abs(den), jnp.exp(-m_t))
            m_new = m_t[last:last + 1, :]
            b_end = bc[last:last + 1, :]
            w_end = jnp.exp(b_end - bc + igc - m_new)
            decay = jnp.exp(b_end + m_prev - m_new)
            kw = k * w_end
            c_ref[b, h] = decay * c_old + _dot(kw.T.astype(BF16), vb)
            n_ref[b, h] = decay * n_old + jnp.sum(kw, axis=0, keepdims=True)
            m_ref[b, h] = jnp.broadcast_to(m_new, (1, LANES))
            sl = slice(h * ML_DV, (h + 1) * ML_DV)
            h_ref[b, :, sl] = _rms_gate(hh, g_ref[:, sl], og_ref[b, :, sl])


def _mlstm_prompt(p, gates, b_gates, norm_g, nb, t):
    chunk = ML_CHUNK
    g8 = jnp.concatenate([gates[..., :ML_HEADS], gates[..., LANES:LANES + ML_HEADS]], axis=-1)
    g_rows = jnp.transpose(g8, (0, 2, 1))
    zeros = jnp.zeros((LANES - ML_HEADS,), F32)
    b_row = jnp.concatenate([b_gates[:ML_HEADS], zeros, b_gates[ML_HEADS:], zeros])[None, :]
    b_col = b_gates[:, None]
    kernel = functools.partial(_mlstm_prompt_kernel, nb=nb, chunk=chunk)
    const2 = lambda c: (0, 0)
    return pl.pallas_call(
        kernel,
        grid=(t // chunk,),
        in_specs=[pl.BlockSpec((nb, chunk, ML_QK), lambda c: (0, c, 0)),
                  pl.BlockSpec((nb, chunk, ML_QK), lambda c: (0, c, 1)),
                  pl.BlockSpec((nb, chunk, ML_VD), lambda c: (0, c, 1)),
                  pl.BlockSpec((nb, chunk, ML_VD), lambda c: (0, c, 2)),
                  pl.BlockSpec((nb, chunk, 2 * LANES), lambda c: (0, c, 0)),
                  pl.BlockSpec((nb, 2 * ML_HEADS, chunk), lambda c: (0, 0, c)),
                  pl.BlockSpec((1, 2 * LANES), const2),
                  pl.BlockSpec((2 * ML_HEADS, 1), const2),
                  pl.BlockSpec((1, ML_VD), const2)],
        out_specs=[pl.BlockSpec((nb, chunk, ML_VD), lambda c: (0, c, 0)),
                   pl.BlockSpec((nb, ML_HEADS, ML_DK, ML_DV), lambda c: (0, 0, 0, 0)),
                   pl.BlockSpec((nb, ML_HEADS, 1, ML_DK), lambda c: (0, 0, 0, 0)),
                   pl.BlockSpec((nb, ML_HEADS, 1, LANES), lambda c: (0, 0, 0, 0))],
        out_shape=[jax.ShapeDtypeStruct((nb, t, ML_VD), F32),
                   jax.ShapeDtypeStruct((nb, ML_HEADS, ML_DK, ML_DV), F32),
                   jax.ShapeDtypeStruct((nb, ML_HEADS, 1, ML_DK), F32),
                   jax.ShapeDtypeStruct((nb, ML_HEADS, 1, LANES), F32)],
        compiler_params=_params(1),
        name="mlstm_prompt",
    )(p, p, p, p, gates, g_rows, b_row, b_col, norm_g)


STATE_BLOCK = 8


def _column(row_vec, eye):
    return jnp.sum(jnp.where(eye, row_vec, 0.0), axis=1, keepdims=True)


def _mlstm_sample_kernel(p_ref, gate_ref, bias_ref, g_ref, c_ref, n_ref, m_ref,
                         h_ref, c_out, n_out, m_out):
    eye = (lax.broadcasted_iota(jnp.int32, (ML_DK, ML_DK), 0)
           == lax.broadcasted_iota(jnp.int32, (ML_DK, ML_DK), 1))
    for s in range(p_ref.shape[0]):
        _mlstm_sample_one(s, eye, p_ref, gate_ref, bias_ref, g_ref, c_ref, n_ref, m_ref,
                          h_ref, c_out, n_out, m_out)


def _mlstm_sample_one(s, eye, p_ref, gate_ref, bias_ref, g_ref, c_ref, n_ref, m_ref,
                      h_ref, c_out, n_out, m_out):
    gates = gate_ref[s] + bias_ref[...]
    ig = gates[:, :LANES]
    lf = _log_sigmoid(gates[:, LANES:])
    m_old = m_ref[s]
    log_prev = lf + m_old
    m_t = jnp.maximum(log_prev, ig)
    w_i_all = jnp.exp(ig - m_t)
    w_p_all = jnp.exp(log_prev - m_t)
    floor_all = jnp.exp(-m_t)
    m_out[s] = m_t
    scale = ML_DK ** -0.5
    for h in range(ML_HEADS):
        q = p_ref[s, :, h * ML_DK:(h + 1) * ML_DK]
        k = p_ref[s, :, ML_QK + h * ML_DK:ML_QK + (h + 1) * ML_DK] * scale
        v = p_ref[s, :, 2 * ML_QK + h * ML_DV:2 * ML_QK + (h + 1) * ML_DV]
        og = p_ref[s, :, 2 * ML_QK + ML_VD + h * ML_DV:2 * ML_QK + ML_VD + (h + 1) * ML_DV]
        w_i = w_i_all[:, h:h + 1]
        w_p = w_p_all[:, h:h + 1]
        c_old = c_ref[0, s, h]
        n_old = n_ref[0, s, h:h + 1, :]
        q_col = _column(q, eye)
        k_col = _column(k, eye)
        a = jnp.sum(q * k, axis=1, keepdims=True) * w_i
        num = a * v + w_p * jnp.sum(c_old * q_col, axis=0, keepdims=True)
        den = a + w_p * jnp.sum(q * n_old, axis=1, keepdims=True)
        hh = num / jnp.maximum(jnp.abs(den), floor_all[:, h:h + 1])
        c_out[0, s, h] = w_p * c_old + (w_i * k_col) * v
        n_out[0, s, h:h + 1, :] = w_p * n_old + w_i * k
        sl = slice(h * ML_DV, (h + 1) * ML_DV)
        h_ref[s, :, sl] = _rms_gate(hh, g_ref[:, sl], og)


def _mlstm_sample_job(p, gates, b_gates, norm_g, c0, n0, m0, steps):
    n = p.shape[0]
    sb = n // steps
    zeros = jnp.zeros((LANES - ML_HEADS,), F32)
    b_row = jnp.concatenate([b_gates[:ML_HEADS], zeros, b_gates[ML_HEADS:], zeros])[None, :]
    m_pad = jnp.pad(m0[0], ((0, 0), (0, LANES - ML_HEADS)))[:, None, :]
    row3 = lambda i: (i, 0, 0)
    const2 = lambda i: (0, 0)
    c_spec = pl.BlockSpec((1, sb, ML_HEADS, ML_DK, ML_DV), lambda i: (0, i, 0, 0, 0))
    n_spec = pl.BlockSpec((1, sb, ML_HEADS, ML_DK), lambda i: (0, i, 0, 0))
    return SideJob(
        in_specs=(pl.BlockSpec((sb, 1, p.shape[1]), row3), pl.BlockSpec((sb, 1, 2 * LANES), row3),
                  pl.BlockSpec((1, 2 * LANES), const2), pl.BlockSpec((1, ML_VD), const2),
                  c_spec, n_spec, pl.BlockSpec((sb, 1, LANES), row3)),
        args=(p[:, None, :], gates[:, None, :], b_row, norm_g, c0, n0, m_pad),
        out_specs=(pl.BlockSpec((sb, 1, ML_VD), row3), c_spec, n_spec, pl.BlockSpec((sb, 1, LANES), row3)),
        out_shapes=(jax.ShapeDtypeStruct((n, 1, ML_VD), F32), jax.ShapeDtypeStruct(c0.shape, F32),
                    jax.ShapeDtypeStruct(n0.shape, F32), jax.ShapeDtypeStruct((n, 1, LANES), F32)),
        body=_mlstm_sample_kernel,
        aliases={})


HG_LEVELS = (1, 2, 4, 8, 16, 32, 64)


def _bcast_rows(x, group, which):
    n, w = x.shape
    x3 = x.reshape(n // group, group, w)
    return jnp.broadcast_to(x3[:, which:which + 1, :], x3.shape).reshape(n, w)


def _level_row_masks(level, t_idx):
    pos = t_idx & (level - 1)
    return dict(odd=((t_idx >> int(math.log2(level))) & 1) == 1,
                at_least=[pos >= i for i in range(level)],
                at_most=[pos <= i for i in range(level)])


def _level_log_decay(level, lf, b_incl, masks, shifted):
    if level >= 8:
        since = b_incl - _bcast_rows(b_incl - lf, level, 0)
        until = _bcast_rows(b_incl, level, level - 1) - b_incl
        return jnp.where(masks["odd"], since, until)
    since = lf
    until = jnp.zeros_like(lf)
    for i in range(1, level):
        since = since + jnp.where(masks["at_least"][i], shifted(i), 0.0)
        until = until + jnp.where(masks["at_most"][level - 1 - i], shifted(-i), 0.0)
    return jnp.where(masks["odd"], since, until)


def _hgrn_prompt_kernel(q_ref, f_ref, i_ref, g_ref, bf_ref, lb_ref, ng_ref, o_ref, s_ref, st_ref,
                        *, nb, chunk, layer_idx):
    step = pl.program_id(0)

    @pl.when(step == 0)
    def _():
        st_ref[...] = jnp.zeros(st_ref.shape, F32)

    logits = [lb_ref[i:i + 1, :] for i in range(DEPTH)]
    mx = functools.reduce(jnp.maximum, logits)
    es = [jnp.exp(l - mx) for l in logits]
    tot = functools.reduce(lambda a, c: a + c, es)
    cum = []
    for e in es:
        cum.append(e / tot if not cum else cum[-1] + e / tot)
    lb = cum[layer_idx] - cum[0]

    row = lax.broadcasted_iota(jnp.int32, (chunk, chunk), 0)
    col = lax.broadcasted_iota(jnp.int32, (chunk, chunk), 1)
    tril = jnp.where(col <= row, 1.0, 0.0).astype(BF16)
    eye = row == col
    level_masks = []
    for level in HG_LEVELS:
        sh = int(math.log2(level))
        u, w = row >> sh, col >> sh
        level_masks.append(((u & 1) * 4096 + (u - w)) == 4097)
    t_idx = lax.broadcasted_iota(jnp.int32, (chunk, HG_DK), 0)
    row_masks = [_level_row_masks(level, t_idx) for level in HG_LEVELS]
    last = chunk - 1

    for b in range(nb):
        fg_all = lb + (1.0 - lb) * _sigmoid(f_ref[b] + bf_ref[...])
        lf_all = jnp.log(fg_all)
        b_all = _dot01_left(tril, lf_all)
        for h in range(HG_HEADS):
            sl = slice(h * HG_DK, (h + 1) * HG_DK)
            qx = q_ref[b, :, sl]
            q = qx * _sigmoid(qx)
            k = 1.0 - fg_all[:, sl]
            lf = lf_all[:, sl]
            bi = b_all[:, sl]
            v = i_ref[b, :, sl]
            vb = v.astype(BF16)
            a = jnp.where(eye, _dot_nt(q.astype(BF16), k.astype(BF16)), 0.0)
            rolled = {}

            def shifted(i, lf=lf, rolled=rolled):
                if i not in rolled:
                    rolled[i] = pltpu.roll(lf, i % chunk, 0)
                return rolled[i]

            for level, mask, rows in zip(HG_LEVELS, level_masks, row_masks):
                e = jnp.exp(_level_log_decay(level, lf, bi, rows, shifted))
                a = a + jnp.where(mask, _dot_nt((q * e).astype(BF16), (k * e).astype(BF16)), 0.0)
            st = st_ref[b, h]
            o = _dot(a.astype(BF16), vb) + _dot_nt((q * jnp.exp(bi)).astype(BF16), st.astype(BF16))
            b_end = bi[last:last + 1, :]
            kd = k * jnp.exp(b_end - bi)
            st_ref[b, h] = st * jnp.exp(b_end) + _dot(v.T.astype(BF16), kd.astype(BF16))
            o_ref[b, :, sl] = _rms_gate(o, ng_ref[:, sl], g_ref[b, :, sl])

    @pl.when(step == pl.num_programs(0) - 1)
    def _():
        for b in range(nb):
            for h in range(HG_HEADS):
                s_ref[b, h] = st_ref[b, h].T


def _hgrn_prompt(p, b_f, lb_logits, norm_g, nb, t, layer_idx):
    chunk = HG_CHUNK
    kernel = functools.partial(_hgrn_prompt_kernel, nb=nb, chunk=chunk, layer_idx=layer_idx)
    const2 = lambda c: (0, 0)
    blk = lambda j: pl.BlockSpec((nb, chunk, HG_W), lambda c: (0, c, j))
    return pl.pallas_call(
        kernel,
        grid=(t // chunk,),
        in_specs=[blk(0), blk(1), blk(2), blk(3),
                  pl.BlockSpec((1, HG_W), const2), pl.BlockSpec((DEPTH, HG_W), const2),
                  pl.BlockSpec((1, HG_W), const2)],
        out_specs=[pl.BlockSpec((nb, chunk, HG_W), lambda c: (0, c, 0)),
                   pl.BlockSpec((nb, HG_HEADS, HG_DK, HG_DK), lambda c: (0, 0, 0, 0))],
        out_shape=[jax.ShapeDtypeStruct((nb, t, HG_W), F32),
                   jax.ShapeDtypeStruct((nb, HG_HEADS, HG_DK, HG_DK), F32)],
        scratch_shapes=[pltpu.VMEM((nb, HG_HEADS, HG_DK, HG_DK), F32)],
        compiler_params=_params(1),
        name="hgrn_prompt",
    )(p, p, p, p, b_f, lb_logits, norm_g)


def _hgrn_sample_kernel(p_ref, bf_ref, lb_ref, ng_ref, s_ref, o_ref, s_out, *, layer_idx):
    logits = [lb_ref[i:i + 1, :] for i in range(DEPTH)]
    mx = functools.reduce(jnp.maximum, logits)
    es = [jnp.exp(l - mx) for l in logits]
    tot = functools.reduce(lambda a, c: a + c, es)
    cum = []
    for e in es:
        cum.append(e / tot if not cum else cum[-1] + e / tot)
    lb = cum[layer_idx] - cum[0]
    eye = (lax.broadcasted_iota(jnp.int32, (HG_DK, HG_DK), 0)
           == lax.broadcasted_iota(jnp.int32, (HG_DK, HG_DK), 1))
    for s in range(p_ref.shape[0]):
        qx = p_ref[s, :, 0:HG_W]
        q_all = qx * _sigmoid(qx)
        fg_all = lb + (1.0 - lb) * _sigmoid(p_ref[s, :, HG_W:2 * HG_W] + bf_ref[...])
        for h in range(HG_HEADS):
            sl = slice(h * HG_DK, (h + 1) * HG_DK)
            q = q_all[:, sl]
            fg = fg_all[:, sl]
            k = 1.0 - fg
            v = p_ref[s, :, 2 * HG_W + h * HG_DK:2 * HG_W + (h + 1) * HG_DK]
            gate = p_ref[s, :, 3 * HG_W + h * HG_DK:3 * HG_W + (h + 1) * HG_DK]
            s_old = s_ref[0, s, h]
            decay = jnp.exp(jnp.log(fg))
            a = jnp.sum(q * k, axis=1, keepdims=True)
            o = a * v + jnp.sum(s_old * _column(q * decay, eye), axis=0, keepdims=True)
            s_out[0, s, h] = _column(decay, eye) * s_old + _column(k, eye) * v
            o_ref[s, :, sl] = _rms_gate(o, ng_ref[:, sl], gate)


def _hgrn_sample(p, b_f, lb_logits, norm_g, s0, layer_idx):
    n = p.shape[0]
    row3 = lambda i: (i, 0, 0)
    const2 = lambda i: (0, 0)
    sb = STATE_BLOCK
    state = pl.BlockSpec((1, sb, HG_HEADS, HG_DK, HG_DK), lambda i: (0, i, 0, 0, 0))
    o, s1 = pl.pallas_call(
        functools.partial(_hgrn_sample_kernel, layer_idx=layer_idx),
        grid=(n // sb,),
        in_specs=[pl.BlockSpec((sb, 1, p.shape[1]), row3), pl.BlockSpec((1, HG_W), const2),
                  pl.BlockSpec((DEPTH, HG_W), const2), pl.BlockSpec((1, HG_W), const2), state],
        out_specs=[pl.BlockSpec((sb, 1, HG_W), row3), state],
        out_shape=[jax.ShapeDtypeStruct((n, 1, HG_W), F32), jax.ShapeDtypeStruct(s0.shape, F32)],
        compiler_params=_params(1),
        name="hgrn_sample",
    )(p[:, None, :], b_f, lb_logits, norm_g, s0)
    return o[:, 0, :], s1


def _band_kernel(*refs, n_heads, group, fold, n_blocks, with_sinks, with_lse):
    refs = list(refs)
    sink_ref = refs.pop(0) if with_sinks else None
    q_ref, ko_ref, kp_ref, vo_ref, vp_ref, o_ref = refs[:6]
    lse_ref = refs[6] if with_lse else None
    step = pl.program_id(2)
    per = SPAN // fold
    shift = int(math.log2(per))

    def local_pos(idx):
        return ((idx & (per - 1)) * fold) + (idx >> shift)

    qi = lax.broadcasted_iota(jnp.int32, (SPAN, 2 * SPAN), 0)
    ki = lax.broadcasted_iota(jnp.int32, (SPAN, 2 * SPAN), 1)
    qpos = SPAN + local_pos(qi)
    kpos = (ki & SPAN) + local_pos(ki & (SPAN - 1))
    band = (kpos <= qpos) & (kpos >= qpos - SPAN)
    first = jnp.where(step > 0, 0, SPAN)
    band_first = band & (kpos >= first)
    lo_q = lax.broadcasted_iota(jnp.int32, (SPAN, LANES), 1) < HEAD_DIM
    lo_kv = lax.broadcasted_iota(jnp.int32, (2 * SPAN, LANES), 1) < HEAD_DIM
    scale = HEAD_DIM ** -0.5

    def block_rows(ref, blk):
        v = ref[:, blk * per:(blk + 1) * per, :]
        return v.reshape(SPAN, v.shape[-1])

    for blk in range(n_blocks):
        valid = band_first if blk == 0 else band
        k_prev = _rows(kp_ref) if blk == 0 else block_rows(ko_ref, blk - 1)
        v_prev = _rows(vp_ref) if blk == 0 else block_rows(vo_ref, blk - 1)
        kcat = jnp.concatenate([k_prev, block_rows(ko_ref, blk)], axis=0)
        vcat = jnp.concatenate([v_prev, block_rows(vo_ref, blk)], axis=0)
        q_blk = block_rows(q_ref, blk)
        tiles = {}

        def kv_tile(name, src, kv_head, want_hi):
            key = (name, kv_head, want_hi)
            if key not in tiles:
                t = src[:, (kv_head // 2) * LANES:(kv_head // 2 + 1) * LANES]
                if (kv_head % 2 == 1) != want_hi:
                    t = pltpu.roll(t, HEAD_DIM, 1)
                tiles[key] = t
            return tiles[key]

        def softmax_parts(s, head):
            s = jnp.where(valid, s, NEG)
            m = jnp.max(s, axis=1, keepdims=True)
            if with_sinks:
                m = jnp.maximum(m, sink_ref[head])
            p = jnp.exp(s - m)
            l = jnp.sum(p, axis=1, keepdims=True)
            if with_sinks:
                l = l + jnp.exp(sink_ref[head] - m)
            return p.astype(BF16), m, l

        for j in range(n_heads // 2):
            sl = slice(j * LANES, (j + 1) * LANES)
            qpair = q_blk[:, sl] * scale
            h_lo, h_hi = 2 * j, 2 * j + 1
            q_lo = jnp.where(lo_q, qpair, 0.0).astype(BF16)
            q_hi = jnp.where(lo_q, 0.0, qpair).astype(BF16)
            s_lo = _dot_nt(q_lo, kv_tile("k", kcat, h_lo // group, False).astype(BF16))
            s_hi = _dot_nt(q_hi, kv_tile("k", kcat, h_hi // group, True).astype(BF16))
            p_lo, m_lo, l_lo = softmax_parts(s_lo, h_lo)
            p_hi, m_hi, l_hi = softmax_parts(s_hi, h_hi)
            v_lo = jnp.where(lo_kv, kv_tile("v", vcat, h_lo // group, False), 0.0).astype(BF16)
            v_hi = jnp.where(lo_kv, 0.0, kv_tile("v", vcat, h_hi // group, True)).astype(BF16)
            o = (_dot(p_lo, v_lo) + _dot(p_hi, v_hi)) / jnp.where(lo_q, l_lo, l_hi)
            o_ref[:, blk * per:(blk + 1) * per, sl] = o.reshape(fold, per, LANES)
            if with_lse:
                lse = jnp.where(lo_q, m_lo + jnp.log(l_lo), m_hi + jnp.log(l_hi))
                lse_ref[:, blk * per:(blk + 1) * per, sl] = lse.reshape(fold, per, LANES)


BAND_BLOCKS = 4


def _band_call(name, arrays, sinks, fold, classes, rows, q_col, k_col, v_col, q_width, kv_width,
               n_heads, group, with_lse):
    seqs = arrays.shape[0]
    per = SPAN // fold
    steps = rows // (per * BAND_BLOCKS)
    own = lambda width, col: pl.BlockSpec((None, fold, None, per * BAND_BLOCKS, width),
                                          lambda b, c, n: (b, 0, c, n, col))
    prev = lambda width, col: pl.BlockSpec(
        (None, fold, None, per, width),
        lambda b, c, n: (b, 0, c, jnp.maximum(n * BAND_BLOCKS - 1, 0), col))
    out = pl.BlockSpec((None, fold, None, per * BAND_BLOCKS, q_width), lambda b, c, n: (b, 0, c, n, 0))
    shape = jax.ShapeDtypeStruct((seqs, fold, classes, rows, q_width), F32)
    in_specs = [own(q_width, q_col), own(kv_width, k_col), prev(kv_width, k_col),
                own(kv_width, v_col), prev(kv_width, v_col)]
    args = [arrays] * 5
    if sinks is not None:
        in_specs = [pl.BlockSpec(memory_space=pltpu.SMEM)] + in_specs
        args = [sinks] + args
    return pl.pallas_call(
        functools.partial(_band_kernel, n_heads=n_heads, group=group, fold=fold, n_blocks=BAND_BLOCKS,
                          with_sinks=sinks is not None, with_lse=with_lse),
        grid=(seqs, classes, steps),
        in_specs=in_specs,
        out_specs=[out, out] if with_lse else out,
        out_shape=[shape, shape] if with_lse else shape,
        compiler_params=_params(3),
        name=name,
    )(*args)


def _dil_prompt_group(p, g):
    _, dil = DIL_GROUPS[g]
    seqs, _, rows, width = p.shape
    fold = RESIDUES // dil
    w = DIL_HEADS * HEAD_DIM
    o, lse = _band_call("dil_prompt_g%d" % g, p.reshape(seqs, fold, dil, rows, width), None, fold, dil, rows,
                        3 * g, 3 * g + 1, 3 * g + 2, w, w, DIL_HEADS, 1, True)
    return o.reshape(seqs, RESIDUES, rows, w), lse.reshape(seqs, RESIDUES, rows, w)


def _swa_prompt(p, sinks, nb, t):
    kcol = SWA_Q // SWA_KV
    o = _band_call("swa_prompt", p.reshape(nb, 1, 1, t, p.shape[-1]), sinks, 1, 1, t,
                   0, kcol, kcol + 1, SWA_Q, SWA_KV, SWA_HEADS, SWA_HEADS // SWA_KV_HEADS, False)
    return o.reshape(nb * t, SWA_Q)


def _decode_roll_kernel(*refs, hb, with_sinks, **static):
    refs = list(refs)
    sink_ref = refs.pop(0) if with_sinks else None
    _decode_block(pl.program_id(1) * hb, sink_ref, *refs, hb=hb, with_sinks=with_sinks, **static)


def _decode_block(h0, sink_ref, q_ref, kn_ref, vn_ref, c_ref, o_ref, lse_ref, cout_ref,
                  *, nb, hb, group, length, dil, with_sinks):
    eye = (lax.broadcasted_iota(jnp.int32, (HEAD_DIM, HEAD_DIM), 0)
           == lax.broadcasted_iota(jnp.int32, (HEAD_DIM, HEAD_DIM), 1))
    pos = lax.broadcasted_iota(jnp.int32, (1, length), 1)
    attended = (pos & (dil - 1)) == 0
    is_last = lax.broadcasted_iota(jnp.int32, (HEAD_DIM, length), 1) == length - 1
    scale = HEAD_DIM ** -0.5
    for i in range(nb if group == 1 else 0):
        kt = c_ref[0, i, 0]
        vt = c_ref[0, i, 1]
        row = lambda ref: jnp.stack([ref[i, pl.ds(h0 + j, 1), :] for j in range(hb)])
        q, kn, vn = row(q_ref) * scale, row(kn_ref), row(vn_ref)
        column = lambda r: jnp.sum(jnp.where(eye[None], r, 0.0), axis=2, keepdims=True)
        s = jnp.sum(kt * column(q), axis=1, keepdims=True)
        s = jnp.where(attended[None], s, NEG)
        s_new = jnp.sum(q * kn, axis=2, keepdims=True)
        m = jnp.maximum(jnp.max(s, axis=2, keepdims=True), s_new)
        p = jnp.exp(s - m)
        p_new = jnp.exp(s_new - m)
        l = jnp.sum(p, axis=2, keepdims=True) + p_new
        acc_col = jnp.sum(vt * p, axis=2, keepdims=True)
        acc = jnp.sum(jnp.where(eye[None], acc_col, 0.0), axis=1, keepdims=True)
        out = (acc + p_new * vn) / l
        lse = jnp.broadcast_to(m + jnp.log(l), out.shape)
        for j in range(hb):
            o_ref[i, pl.ds(h0 + j, 1), :] = out[j]
            lse_ref[i, pl.ds(h0 + j, 1), :] = lse[j]
        cout_ref[0, i, 0] = jnp.where(is_last[None], column(kn), pltpu.roll(kt, length - 1, 2))
        cout_ref[0, i, 1] = jnp.where(is_last[None], column(vn), pltpu.roll(vt, length - 1, 2))
    for i in range(nb if group > 1 else 0):
        for j in range(hb):
            kt = c_ref[0, i, 0, j]
            vt = c_ref[0, i, 1, j]
            kn = kn_ref[i, pl.ds(h0 + j, 1), :]
            vn = vn_ref[i, pl.ds(h0 + j, 1), :]
            rows = pl.ds((h0 + j) * group, group)
            q = q_ref[i, rows, :] * scale
            s = _dot(q.astype(BF16), kt.astype(BF16))
            s = jnp.where(attended, s, NEG)
            s_new = jnp.sum(q * kn, axis=1, keepdims=True)
            m = jnp.maximum(jnp.max(s, axis=1, keepdims=True), s_new)
            if with_sinks:
                sink = sink_ref[rows, :]
                m = jnp.maximum(m, sink)
            p = jnp.exp(s - m)
            p_new = jnp.exp(s_new - m)
            l = jnp.sum(p, axis=1, keepdims=True) + p_new
            if with_sinks:
                l = l + jnp.exp(sink - m)
            acc = _dot_nt(p.astype(BF16), vt.astype(BF16))
            o_ref[i, rows, :] = (acc + p_new * vn) / l
            lse_ref[i, rows, :] = jnp.broadcast_to(m + jnp.log(l), (group, HEAD_DIM))
            cout_ref[0, i, 0, j] = jnp.where(is_last, _column(kn, eye), pltpu.roll(kt, length - 1, 1))
            cout_ref[0, i, 1, j] = jnp.where(is_last, _column(vn, eye), pltpu.roll(vt, length - 1, 1))


def _decode_job(q, k_new, v_new, cache_t, dil, nb, hb, steps, first=0, partial=None):
    _, _, _, kv_heads, d, length = cache_t.shape
    hsteps = kv_heads // hb
    seq_of = lambda i: (first + i) // hsteps
    cblock = pl.BlockSpec((1, nb, 2, hb, d, length), lambda i: (0, seq_of(i), 0, (first + i) % hsteps, 0, 0))
    per_seq = pl.BlockSpec((nb, kv_heads, d), lambda i: (seq_of(i), 0, 0))
    part = pl.BlockSpec((nb, kv_heads, d), lambda i: (i // hsteps, 0, 0))
    part_shape = jax.ShapeDtypeStruct((steps // hsteps * nb, kv_heads, d), F32)

    def body(q_ref, kn_ref, vn_ref, c_ref, *rest):
        o_ref, lse_ref, cout_ref = rest[-3:]
        block = first + pl.program_id(0)
        _decode_block((block % hsteps) * hb, None, q_ref, kn_ref, vn_ref, c_ref, o_ref, lse_ref, cout_ref,
                      nb=nb, hb=hb, group=1, length=length, dil=dil, with_sinks=False)

    in_specs, args, aliases = [per_seq, per_seq, per_seq, cblock], [q, k_new, v_new, cache_t], {}
    if partial is not None:
        in_specs.append(pl.BlockSpec(memory_space=pl.ANY))
        args.append(partial)
        aliases = {4: 2}
    return SideJob(tuple(in_specs), tuple(args), (part, part, cblock),
                   (part_shape, part_shape, jax.ShapeDtypeStruct(cache_t.shape, F32)), body, aliases)


def _decode_roll(q, k_new, v_new, cache, dil, nb, hb, sinks=None):
    _, n, length, _, kv_heads, d = cache.shape
    heads = q.shape[1]
    group = heads // kv_heads
    cache_t = jnp.transpose(cache, (0, 1, 3, 4, 5, 2))
    cblock = pl.BlockSpec((1, nb, 2, hb, d, length), lambda i, h: (0, i, 0, h, 0, 0))
    per_seq = lambda width: pl.BlockSpec((nb, width, d), lambda i, h: (i, 0, 0))
    in_specs = [per_seq(heads), per_seq(kv_heads), per_seq(kv_heads), cblock]
    args = [q, k_new, v_new, cache_t]
    if sinks is not None:
        in_specs = [pl.BlockSpec((heads, 1), lambda i, h: (0, 0))] + in_specs
        args = [sinks] + args
    o, lse, new_t = pl.pallas_call(
        functools.partial(_decode_roll_kernel, nb=nb, hb=hb, group=group, length=length, dil=dil,
                          with_sinks=sinks is not None),
        grid=(n // nb, kv_heads // hb),
        in_specs=in_specs,
        out_specs=[per_seq(heads), per_seq(heads), cblock],
        out_shape=[jax.ShapeDtypeStruct(q.shape, F32), jax.ShapeDtypeStruct(q.shape, F32),
                   jax.ShapeDtypeStruct(cache_t.shape, F32)],
        compiler_params=_params(2),
        name="decode_roll",
    )(*args)
    return o, lse, jnp.transpose(new_t, (0, 1, 5, 2, 3, 4))


PROMPT_TM = 512
LN_TM = 256
RIDE_HEADS = 4
RIDE_TM = 256
SWA_DECODE_BLOCK = (16, 2)


def _pad_cols(w, width):
    return jnp.pad(w, ((0, 0), (0, width - w.shape[1])))


def kernel(x_prompt, x_sample, state_mlstm_C, state_mlstm_n, state_mlstm_m, cache_dil_kv0, cache_dil_kv1, cache_dil_kv2, state_hgrn_S, cache_swa_kv, mlstm_w_in, mlstm_b_gates, mlstm_norm_g, mlstm_w_out, dil_w_in, dil_w_out, hgrn_w_in, hgrn_b_f, hgrn_lb_logits, hgrn_norm_g, hgrn_w_out, swa_w_in, swa_sinks, swa_w_out, ln1_g, ln1_b, ln2_g, ln2_b, mlp_w1, mlp_w2):
    nb, t, _ = x_prompt.shape
    ns = x_sample.shape[0]
    assert x_sample.shape[1] == 1 and DEPTH == 4
    hp = x_prompt.reshape(nb * t, D_MODEL)
    hs = x_sample.reshape(ns, D_MODEL)
    cos_p, sin_p = _rope_tables(jnp.arange(t, dtype=jnp.int32))
    cos_s, sin_s = _rope_tables(jnp.full((ns,), PAST_LEN, dtype=jnp.int32))

    ln = lambda i: (ln1_g[i][None], ln1_b[i][None], ln2_g[i][None], ln2_b[i][None])
    mlp = lambda i: (mlp_w1[i].astype(BF16), mlp_w2[i].astype(BF16))
    w_out = [w[0].astype(BF16) for w in (mlstm_w_out, dil_w_out, hgrn_w_out, swa_w_out)]

    w_in = mlstm_w_in[0]
    main = 2 * ML_QK + 2 * ML_VD
    w_mlstm = w_in[:, :main].astype(BF16)
    w_gate = jnp.concatenate([_pad_cols(w_in[:, main:main + ML_HEADS], LANES),
                              _pad_cols(w_in[:, main + ML_HEADS:], LANES)], axis=1).astype(BF16)
    w_dil = dil_w_in[0].astype(BF16)
    dil_qk = tuple((g * DIL_GW, g * DIL_GW + 2 * DIL_HEADS * HEAD_DIM) for g in range(len(DIL_GROUPS)))
    w_hgrn = hgrn_w_in[0].astype(BF16)
    w_swa = swa_w_in[0].astype(BF16)
    swa_qk = ((0, SWA_Q + SWA_KV),)
    norm_g = mlstm_norm_g[0][None]
    b_f, ng = hgrn_b_f[0][None], hgrn_norm_g[0][None]


    ps = _proj(hs, w_mlstm, ns)
    gs = _proj(hs, w_gate, ns)
    job = _mlstm_sample_job(ps, gs, mlstm_b_gates[0], norm_g, state_mlstm_C, state_mlstm_n, state_mlstm_m,
                            nb * t // PROMPT_TM)
    pp, mix_s, c_s, n_s, m_s = _proj(hp, w_mlstm, PROMPT_TM, sides=(job,))
    gp = _proj(hp, w_gate, PROMPT_TM)
    m_s = m_s[:, 0, :ML_HEADS][None]

    hs = _layer_tail(hs, *mlp(0), *ln(0)[2:], ns, mix_s[:, 0, :], w_out[0], *ln(0)[:2])
    qkv_s = _proj(hs, w_dil, ns, dil_qk, (cos_s, sin_s)).reshape(ns, 9, DIL_HEADS, HEAD_DIM)
    position_minor = lambda cache: jnp.transpose(cache, (0, 1, 3, 4, 5, 2))
    position_major = lambda cache_t: jnp.transpose(cache_t, (0, 1, 5, 2, 3, 4))

    def decode_job(g, cache, nb_, hb, steps, first=0, partial=None):
        win, dil = DIL_GROUPS[g]
        assert cache.shape[2] == win and win // dil == SPAN
        return _decode_job(qkv_s[:, 3 * g], qkv_s[:, 3 * g + 1], qkv_s[:, 3 * g + 2], position_minor(cache),
                           dil, nb_, hb, steps, first, partial)

    tail_steps = nb * t // LN_TM
    assert DEPTH * tail_steps == ns * (DIL_HEADS // RIDE_HEADS)
    big_o, big_lse = [], []

    def prompt_tail(layer, hp, mix, partial, shuffle=None):
        g1, b1, g2, b2 = ln(layer)
        with_mix = (mix, w_out[layer], g1, b1) if mix is not None else ()
        job = decode_job(len(DIL_GROUPS) - 1, cache_dil_kv2, 1, RIDE_HEADS, tail_steps,
                         layer * tail_steps, partial)
        hp, o, lse, partial = _layer_tail(hp, *mlp(layer), g2, b2, LN_TM, *with_mix, shuffle=shuffle,
                                          seq_len=t, sides=(job,))
        big_o.append(o)
        big_lse.append(lse)
        return hp, partial

    mix_p, c_p, n_p, m_p = _mlstm_prompt(pp.reshape(nb, t, main), gp.reshape(nb, t, 2 * LANES),
                                         mlstm_b_gates[0], norm_g, nb, t)
    out_c_p = c_p[None]
    out_n_p = n_p[:, :, 0, :][None]
    out_m_p = m_p[:, :, 0, 0][None]
    hp, rolled = prompt_tail(0, hp, mix_p.reshape(nb * t, ML_VD), None, shuffle="to_residue")

    by_residue = lambda tab: jnp.transpose(tab.reshape(t // RESIDUES, RESIDUES, LANES), (1, 0, 2))
    small = [decode_job(g, cache, ns * RIDE_TM // (nb * t), DIL_HEADS, nb * t // RIDE_TM)
             for g, cache in enumerate((cache_dil_kv0, cache_dil_kv1))]
    pp, o0, lse0, new0, o1, lse1, new1 = _proj(hp, w_dil, RIDE_TM, dil_qk, (by_residue(cos_p), by_residue(sin_p)),
                                               by_residue=True, sides=small)
    outs, lses, dil_kv_p = [], [], []
    for g, (win, _) in enumerate(DIL_GROUPS):
        o, lse = _dil_prompt_group(pp, g)
        outs.append(o)
        lses.append(lse)
        kv = pp[:, :, (t - win) // RESIDUES:, g * DIL_GW + DIL_HEADS * HEAD_DIM:(g + 1) * DIL_GW]
        kv = jnp.transpose(kv, (0, 2, 1, 3))
        dil_kv_p.append(kv.reshape(nb, win, 2, DIL_HEADS, HEAD_DIM)[None])
    hp = _dil_out_ln(outs, lses, w_out[1], hp, *ln(1)[:2], LN_TM, by_residue=True)
    hp, rolled = prompt_tail(1, hp, None, rolled, shuffle="to_natural")

    pp = _proj(hp, w_hgrn, PROMPT_TM)
    mix_p, s_p = _hgrn_prompt(pp.reshape(nb, t, 4 * HG_W), b_f, hgrn_lb_logits, ng, nb, t, 2)
    hp, rolled = prompt_tail(2, hp, mix_p.reshape(nb * t, HG_W), rolled)

    pp = _proj(hp, w_swa, PROMPT_TM, swa_qk, (cos_p, sin_p)).reshape(nb, t, -1)
    mix_p = _swa_prompt(pp, swa_sinks[0], nb, t)
    swa_kv_p = pp[:, t - SPAN:, SWA_Q:].reshape(nb, SPAN, 2, SWA_KV_HEADS, HEAD_DIM)[None]
    hp, rolled = prompt_tail(3, hp, mix_p, rolled)

    outs_s = [o0, o1, jnp.concatenate(big_o, axis=0)]
    lses_s = [lse0, lse1, jnp.concatenate(big_lse, axis=0)]
    dil_kv_s = [position_major(c) for c in (new0, new1, rolled)]
    flat = lambda a: a.reshape(ns, DIL_HEADS * HEAD_DIM)
    hs = _dil_out_ln([flat(o) for o in outs_s], [flat(l) for l in lses_s], w_out[1], hs, *ln(1)[:2], ns)
    hs = _layer_tail(hs, *mlp(1), *ln(1)[2:], ns)

    ps = _proj(hs, w_hgrn, ns)
    mix_s, s_s = _hgrn_sample(ps, b_f, hgrn_lb_logits, ng, state_hgrn_S, 2)
    hs = _layer_tail(hs, *mlp(2), *ln(2)[2:], ns, mix_s, w_out[2], *ln(2)[:2])

    ps = _proj(hs, w_swa, ns, swa_qk, (cos_s, sin_s))
    q_s = ps[:, :SWA_Q].reshape(ns, SWA_HEADS, HEAD_DIM)
    kv_s = ps[:, SWA_Q:].reshape(ns, 2, SWA_KV_HEADS, HEAD_DIM)
    assert cache_swa_kv.shape[2] == SPAN
    mix_s, _, swa_kv_s = _decode_roll(q_s, kv_s[:, 0], kv_s[:, 1], cache_swa_kv, 1, *SWA_DECODE_BLOCK,
                                      sinks=swa_sinks[0][:, None])
    hs = _layer_tail(hs, *mlp(3), *ln(3)[2:], ns, mix_s.reshape(ns, SWA_Q), w_out[3], *ln(3)[:2])

    return (hp.reshape(nb, t, D_MODEL), hs.reshape(ns, 1, D_MODEL),
            out_c_p, c_s, out_n_p, n_s, out_m_p, m_s,
            dil_kv_p[0], dil_kv_s[0], dil_kv_p[1], dil_kv_s[1], dil_kv_p[2], dil_kv_s[2],
            s_p[None], s_s, swa_kv_p, swa_kv_s)
```

```python
import functools
import math
from typing import Callable, NamedTuple

import numpy as np
import jax
import jax.numpy as jnp
from jax import lax
from jax.experimental import pallas as pl
from jax.experimental.pallas import tpu as pltpu

F32 = jnp.float32
BF16 = jnp.bfloat16

D_MODEL = 1024
DEPTH = 4
PAST_LEN = 8192
D_FF = 4 * D_MODEL
ALPHA = (2 * DEPTH) ** 0.25
LN_EPS = 1e-5
NORM_EPS = 1e-6
ROPE_THETA = 10000.0

ML_HEADS = 4
ML_DK = 128
ML_DV = 256
ML_QK = ML_HEADS * ML_DK
ML_VD = ML_HEADS * ML_DV
ML_CHUNK = 256

DIL_GROUPS = ((128, 1), (512, 4), (2048, 16))
DIL_HEADS = 8
HEAD_DIM = 64
DIL_GW = 3 * DIL_HEADS * HEAD_DIM
SPAN = 128

HG_HEADS = 8
HG_DK = 128
HG_CHUNK = 128
HG_W = HG_HEADS * HG_DK

SWA_HEADS = 16
SWA_KV_HEADS = 2
SWA_Q = SWA_HEADS * HEAD_DIM
SWA_KV = SWA_KV_HEADS * HEAD_DIM

LANES = 128
NEG = -1e30
VMEM_LIMIT = 48 * 1024 * 1024
HOST_VMEM_LIMIT = 58 * 1024 * 1024


def _params(n_axes, vmem=VMEM_LIMIT):
    return pltpu.CompilerParams(dimension_semantics=("arbitrary",) * n_axes, vmem_limit_bytes=vmem)


class SideJob(NamedTuple):
    in_specs: tuple
    args: tuple
    out_specs: tuple
    out_shapes: tuple
    body: Callable
    aliases: dict


def _host_kernel(*refs, own, n_in, n_out, sides):
    refs = list(refs)
    n_all_in = n_in + sum(job_in for job_in, _, _ in sides)
    n_all_out = n_out + sum(job_out for _, job_out, _ in sides)
    ins, outs, scratch = refs[:n_all_in], refs[n_all_in:n_all_in + n_all_out], refs[n_all_in + n_all_out:]
    at_in, at_out = n_in, n_out
    for job_in, job_out, body in sides:
        body(*ins[at_in:at_in + job_in], *outs[at_out:at_out + job_out])
        at_in += job_in
        at_out += job_out
    own(*ins[:n_in], *outs[:n_out], *scratch)


def _call(own, steps, in_specs, args, out_specs, out_shapes, name, scratch_shapes=(), sides=(),
          vmem=VMEM_LIMIT):
    all_in, all_args = list(in_specs), list(args)
    all_out, all_shapes = list(out_specs), list(out_shapes)
    aliases = {}
    for job in sides:
        for src, dst in job.aliases.items():
            aliases[len(all_args) + src] = len(all_out) + dst
        all_in += job.in_specs
        all_args += job.args
        all_out += job.out_specs
        all_shapes += job.out_shapes
    static = tuple((len(job.in_specs), len(job.out_specs), job.body) for job in sides)
    return pl.pallas_call(
        functools.partial(_host_kernel, own=own, n_in=len(in_specs), n_out=len(out_specs), sides=static),
        grid=(steps,),
        in_specs=all_in,
        out_specs=all_out,
        out_shape=all_shapes,
        input_output_aliases=aliases,
        scratch_shapes=list(scratch_shapes),
        compiler_params=_params(1, vmem),
        name=name,
    )(*all_args)


def _dot(a, b):
    return jnp.dot(a, b, preferred_element_type=F32)


def _dot_nt(a, b):
    return lax.dot_general(a, b, (((1,), (1,)), ((), ())), preferred_element_type=F32)


def _split3(x):
    hi = x.astype(BF16)
    r1 = x - hi.astype(F32)
    mid = r1.astype(BF16)
    lo = (r1 - mid.astype(F32)).astype(BF16)
    return hi, mid, lo


def _dot01_left(a01, x):
    hi, mid, lo = _split3(x)
    return _dot(a01, hi) + _dot(a01, mid) + _dot(a01, lo)


def _dot01_right(x, a01):
    hi, mid, lo = _split3(x)
    return _dot(hi, a01) + _dot(mid, a01) + _dot(lo, a01)


def _sigmoid(x):
    return 0.5 * jnp.tanh(0.5 * x) + 0.5


def _log_sigmoid(x):
    return jnp.minimum(x, 0.0) - jnp.log1p(jnp.exp(-jnp.abs(x)))


def _layer_norm(z, g, b):
    mu = jnp.mean(z, axis=-1, keepdims=True)
    zc = z - mu
    var = jnp.mean(zc * zc, axis=-1, keepdims=True)
    return zc * lax.rsqrt(var + LN_EPS) * g + b


def _rms_gate(h, g, gate):
    ms = jnp.mean(h * h, axis=-1, keepdims=True)
    return h * lax.rsqrt(ms + NORM_EPS) * g * _sigmoid(gate)


PROJ_CHUNK = 512
RESIDUES = 16


def _row_spec(width, tm, residue_tiles=None):
    if residue_tiles is None:
        return pl.BlockSpec((tm, width), lambda i: (i, 0))
    return pl.BlockSpec((None, RESIDUES, tm // RESIDUES, width),
                        lambda i: (i // residue_tiles, 0, i % residue_tiles, 0))


def _rows(ref):
    v = ref[...]
    return v.reshape(-1, v.shape[-1])


def _proj_kernel(*refs, plan):
    if any(rope for _, _, rope in plan):
        x_ref, w_ref, cos_ref, sin_ref, o_ref = refs
    else:
        x_ref, w_ref, o_ref = refs
    xb = _rows(x_ref).astype(BF16)
    for start, width, rope in plan:
        acc = _dot(xb, w_ref[:, start:start + width])
        if rope:
            reps = width // LANES
            cos = jnp.concatenate([_rows(cos_ref)] * reps, axis=1)
            sin = jnp.concatenate([_rows(sin_ref)] * reps, axis=1)
            lane = lax.broadcasted_iota(jnp.int32, acc.shape, 1)
            first_half = (lane & (HEAD_DIM - 1)) < HEAD_DIM // 2
            partner = jnp.where(first_half, pltpu.roll(acc, width - HEAD_DIM // 2, 1),
                                pltpu.roll(acc, HEAD_DIM // 2, 1))
            acc = acc * cos + partner * sin
        o_ref[..., start:start + width] = acc.reshape(o_ref.shape[:-1] + (width,))


def _proj(x, w_bf, tm, rope_cols=(), tables=None, by_residue=False, sides=()):
    k, n = w_bf.shape
    plan = []
    start = 0
    while start < n:
        rope = any(lo <= start < hi for lo, hi in rope_cols)
        limit = min([hi for lo, hi in rope_cols if lo <= start < hi] +
                    [lo for lo, hi in rope_cols if lo > start] + [n])
        width = min(PROJ_CHUNK, limit - start)
        plan.append((start, width, rope))
        start += width
    if by_residue:
        seqs, _, per_residue, _ = x.shape
        tiles = per_residue * RESIDUES // tm
        steps = seqs * tiles
        tab = pl.BlockSpec((RESIDUES, tm // RESIDUES, LANES), lambda i: (0, i % tiles, 0))
    else:
        tiles = None
        steps = x.shape[0] // tm
        if rope_cols:
            t_blocks = tables[0].shape[0] // tm
            tab = pl.BlockSpec((tm, LANES), lambda i: (i % t_blocks, 0))
    in_specs = [_row_spec(k, tm, tiles),
                pl.BlockSpec((k, n), lambda i: (0, 0), pipeline_mode=pl.Buffered(1))]
    args = [x, w_bf]
    if rope_cols:
        in_specs += [tab, tab]
        args += list(tables)
    results = _call(functools.partial(_proj_kernel, plan=tuple(plan)), steps, in_specs, args,
                    [_row_spec(n, tm, tiles)], [jax.ShapeDtypeStruct(x.shape[:-1] + (n,), F32)], "proj",
                    sides=sides, vmem=HOST_VMEM_LIMIT if sides else VMEM_LIMIT)
    return results if sides else results[0]


def _rope_tables(pos):
    half = HEAD_DIM // 2
    inv_freq = jnp.power(ROPE_THETA, -jnp.arange(half, dtype=F32) / half)
    ang = pos.astype(F32)[:, None] * inv_freq[None, :]
    cos = jnp.cos(ang)
    sin = jnp.sin(ang)
    cos = jnp.concatenate([cos, cos, cos, cos], axis=1)
    sin = jnp.concatenate([-sin, sin, -sin, sin], axis=1)
    return cos, sin


def _merge_groups(o0, o1, o2, l0, l1, l2):
    a0, a1, a2 = _rows(l0), _rows(l1), _rows(l2)
    mx = jnp.maximum(jnp.maximum(a0, a1), a2)
    e0, e1, e2 = jnp.exp(a0 - mx), jnp.exp(a1 - mx), jnp.exp(a2 - mx)
    return (e0 * _rows(o0) + e1 * _rows(o1) + e2 * _rows(o2)) / (e0 + e1 + e2)


def _dil_out_ln_kernel(o0, o1, o2, l0, l1, l2, w_ref, x_ref, g_ref, b_ref, y_ref):
    y = _merge_groups(o0, o1, o2, l0, l1, l2)
    z = ALPHA * _rows(x_ref) + _dot(y.astype(BF16), w_ref[...])
    y_ref[...] = _layer_norm(z, g_ref[...], b_ref[...]).reshape(y_ref.shape)


def _dil_out_ln(outs, lses, w_bf, x, g, b, tm):
    kin = outs[0].shape[-1]
    tiles = None
    steps = x.shape[0] // tm
    const = lambda i: (0, 0)
    return pl.pallas_call(
        _dil_out_ln_kernel,
        grid=(steps,),
        in_specs=[_row_spec(kin, tm, tiles)] * 6 + [
            pl.BlockSpec((kin, D_MODEL), const), _row_spec(D_MODEL, tm, tiles),
            pl.BlockSpec((1, D_MODEL), const), pl.BlockSpec((1, D_MODEL), const)],
        out_specs=_row_spec(D_MODEL, tm, tiles),
        out_shape=jax.ShapeDtypeStruct(x.shape, F32),
        compiler_params=_params(1),
        name="dil_out_ln",
    )(*outs, *lses, w_bf, x, g, b)


FF_CHUNK = 1024


def _tail_kernel(*refs, with_mix, shuffle):
    refs = list(refs)
    scratch = refs.pop() if shuffle else None
    y_ref = refs.pop()
    if with_mix:
        n_mix = 6 if with_mix == "groups" else 1
        mix_refs, (wo_ref, x_ref, g1_ref, b1_ref) = refs[:n_mix], refs[n_mix:n_mix + 4]
        refs = refs[n_mix + 4:]
        mix = _merge_groups(*mix_refs) if with_mix == "groups" else _rows(mix_refs[0])
        z = ALPHA * _rows(x_ref) + _dot(mix.astype(BF16), wo_ref[...])
        x = _layer_norm(z, g1_ref[...], b1_ref[...])
    else:
        x = _rows(refs.pop(0))
    w1_ref, w2_ref, g2_ref, b2_ref = refs
    xb = x.astype(BF16)
    acc = jnp.zeros(x.shape, F32)
    for c in range(D_FF // FF_CHUNK):
        a = _dot(xb, w1_ref[:, c * FF_CHUNK:(c + 1) * FF_CHUNK])
        a = jnp.square(jnp.maximum(a, 0.0)).astype(BF16)
        acc = acc + _dot(a, w2_ref[c * FF_CHUNK:(c + 1) * FF_CHUNK, :])
    y = _layer_norm(ALPHA * x + acc, g2_ref[...], b2_ref[...])
    per = y.shape[0] // RESIDUES
    lane_tiles = [slice(c * LANES, (c + 1) * LANES) for c in range(D_MODEL // LANES)]
    if shuffle == "to_residue":
        for c, cols in enumerate(lane_tiles):
            scratch[c] = y[:, cols]
        for r in range(RESIDUES):
            for c, cols in enumerate(lane_tiles):
                y_ref[r, :, cols] = scratch[c, pl.ds(r, per, stride=RESIDUES), :]
    elif shuffle == "to_natural":
        for r in range(RESIDUES):
            for c, cols in enumerate(lane_tiles):
                scratch[c, pl.ds(r, per, stride=RESIDUES), :] = y[r * per:(r + 1) * per, cols]
        for c, cols in enumerate(lane_tiles):
            y_ref[:, cols] = scratch[c]
    else:
        y_ref[...] = y


def _layer_tail(x, w1_bf, w2_bf, g2, b2, tm, mix=None, w_out_bf=None, g1=None, b1=None,
                shuffle=None, seq_len=None, sides=()):
    const = lambda i: (0, 0)
    resident = pl.Buffered(1)
    in_tiles = out_tiles = None
    out_shape = x.shape
    if shuffle == "to_natural":
        in_tiles = x.shape[2] * RESIDUES // tm
        steps = x.shape[0] * in_tiles
        out_shape = (x.shape[0] * x.shape[1] * x.shape[2], D_MODEL)
    else:
        steps = x.shape[0] // tm
        if shuffle == "to_residue":
            out_tiles = seq_len // tm
            out_shape = (x.shape[0] // seq_len, RESIDUES, seq_len // RESIDUES, D_MODEL)
    in_specs, args = [], []
    with_mix = False
    if mix is not None:
        grouped = isinstance(mix, tuple)
        mixes = list(mix[0]) + list(mix[1]) if grouped else [mix]
        with_mix = "groups" if grouped else True
        kin = mixes[0].shape[-1]
        in_specs += [_row_spec(kin, tm, in_tiles)] * len(mixes)
        in_specs.append(pl.BlockSpec((kin, D_MODEL), const, pipeline_mode=resident))
        args += mixes + [w_out_bf]
    in_specs.append(_row_spec(D_MODEL, tm, in_tiles))
    args.append(x)
    if mix is not None:
        in_specs += [pl.BlockSpec((1, D_MODEL), const)] * 2
        args += [g1, b1]
    in_specs += [pl.BlockSpec((D_MODEL, D_FF), const, pipeline_mode=resident),
                 pl.BlockSpec((D_FF, D_MODEL), const, pipeline_mode=resident),
                 pl.BlockSpec((1, D_MODEL), const), pl.BlockSpec((1, D_MODEL), const)]
    args += [w1_bf, w2_bf, g2, b2]
    scratch_shapes = [pltpu.VMEM((D_MODEL // LANES, tm, LANES), F32)] if shuffle else []
    results = _call(functools.partial(_tail_kernel, with_mix=with_mix, shuffle=shuffle), steps,
                    in_specs, args, [_row_spec(D_MODEL, tm, out_tiles)],
                    [jax.ShapeDtypeStruct(out_shape, F32)], "layer_tail", scratch_shapes, sides,
                    HOST_VMEM_LIMIT if sides else VMEM_LIMIT)
    return results if sides else results[0]


def _mlstm_prompt_kernel(q_ref, k_ref, v_ref, og_ref, gc_ref, gr_ref, brow_ref, bcol_ref, g_ref,
                         h_ref, c_ref, n_ref, m_ref, *, nb, chunk):
    step = pl.program_id(0)

    @pl.when(step == 0)
    def _():
        c_ref[...] = jnp.zeros(c_ref.shape, F32)
        n_ref[...] = jnp.zeros(n_ref.shape, F32)
        m_ref[...] = jnp.zeros(m_ref.shape, F32)

    row = lax.broadcasted_iota(jnp.int32, (chunk, chunk), 0)
    col = lax.broadcasted_iota(jnp.int32, (chunk, chunk), 1)
    causal = col <= row
    tril = jnp.where(causal, 1.0, 0.0).astype(BF16)
    triu = jnp.where(row <= col, 1.0, 0.0).astype(BF16)
    scale = ML_DK ** -0.5
    last = chunk - 1

    for b in range(nb):
        gc = gc_ref[b] + brow_ref[...]
        ig_cols = gc[:, :LANES]
        b_cols = _dot01_left(tril, _log_sigmoid(gc[:, LANES:]))
        gr = gr_ref[b] + bcol_ref[...]
        b_rows = _dot01_right(_log_sigmoid(gr), triu)
        for h in range(ML_HEADS):
            bc = b_cols[:, h:h + 1]
            igc = ig_cols[:, h:h + 1]
            br = b_rows[ML_HEADS + h:ML_HEADS + h + 1, :]
            igr = gr[h:h + 1, :]
            m_prev = m_ref[b, h][:, 0:1]
            log_intra = jnp.where(causal, bc - br + igr, NEG)
            log_prev = bc + m_prev
            m_t = jnp.maximum(log_prev, jnp.max(log_intra, axis=1, keepdims=True))
            w_intra = jnp.exp(log_intra - m_t)
            w_prev = jnp.exp(log_prev - m_t)
            q = q_ref[b, :, h * ML_DK:(h + 1) * ML_DK]
            k = k_ref[b, :, h * ML_DK:(h + 1) * ML_DK] * scale
            v = v_ref[b, :, h * ML_DV:(h + 1) * ML_DV]
            qb, kb, vb = q.astype(BF16), k.astype(BF16), v.astype(BF16)
            a = _dot_nt(qb, kb) * w_intra
            c_old = c_ref[b, h]
            n_old = n_ref[b, h]
            num = _dot(a.astype(BF16), vb) + w_prev * _dot(qb, c_old.astype(BF16))
            den = jnp.sum(a, axis=1, keepdims=True) + w_prev * jnp.sum(q * n_old, axis=1, keepdims=True)
            hh = num / jnp.maximum(jnp.abs(den), jnp.exp(-m_t))
            m_new = m_t[last:last + 1, :]
            b_end = bc[last:last + 1, :]
            w_end = jnp.exp(b_end - bc + igc - m_new)
            decay = jnp.exp(b_end + m_prev - m_new)
            kw = k * w_end
            c_ref[b, h] = decay * c_old + _dot(kw.T.astype(BF16), vb)
            n_ref[b, h] = decay * n_old + jnp.sum(kw, axis=0, keepdims=True)
            m_ref[b, h] = jnp.broadcast_to(m_new, (1, LANES))
            sl = slice(h * ML_DV, (h + 1) * ML_DV)
            h_ref[b, :, sl] = _rms_gate(hh, g_ref[:, sl], og_ref[b, :, sl])


def _mlstm_prompt(p, b_gates, norm_g, nb, t):
    chunk = ML_CHUNK
    main = 2 * ML_QK + 2 * ML_VD
    g8 = jnp.concatenate([p[..., main:main + ML_HEADS], p[..., main + LANES:main + LANES + ML_HEADS]], axis=-1)
    g_rows = jnp.transpose(g8, (0, 2, 1))
    zeros = jnp.zeros((LANES - ML_HEADS,), F32)
    b_row = jnp.concatenate([b_gates[:ML_HEADS], zeros, b_gates[ML_HEADS:], zeros])[None, :]
    b_col = b_gates[:, None]
    kernel = functools.partial(_mlstm_prompt_kernel, nb=nb, chunk=chunk)
    const2 = lambda c: (0, 0)
    return pl.pallas_call(
        kernel,
        grid=(t // chunk,),
        in_specs=[pl.BlockSpec((nb, chunk, ML_QK), lambda c: (0, c, 0)),
                  pl.BlockSpec((nb, chunk, ML_QK), lambda c: (0, c, 1)),
                  pl.BlockSpec((nb, chunk, ML_VD), lambda c: (0, c, 1)),
                  pl.BlockSpec((nb, chunk, ML_VD), lambda c: (0, c, 2)),
                  pl.BlockSpec((nb, chunk, 2 * LANES), lambda c: (0, c, main // (2 * LANES))),
                  pl.BlockSpec((nb, 2 * ML_HEADS, chunk), lambda c: (0, 0, c)),
                  pl.BlockSpec((1, 2 * LANES), const2),
                  pl.BlockSpec((2 * ML_HEADS, 1), const2),
                  pl.BlockSpec((1, ML_VD), const2)],
        out_specs=[pl.BlockSpec((nb, chunk, ML_VD), lambda c: (0, c, 0)),
                   pl.BlockSpec((nb, ML_HEADS, ML_DK, ML_DV), lambda c: (0, 0, 0, 0)),
                   pl.BlockSpec((nb, ML_HEADS, 1, ML_DK), lambda c: (0, 0, 0, 0)),
                   pl.BlockSpec((nb, ML_HEADS, 1, LANES), lambda c: (0, 0, 0, 0))],
        out_shape=[jax.ShapeDtypeStruct((nb, t, ML_VD), F32),
                   jax.ShapeDtypeStruct((nb, ML_HEADS, ML_DK, ML_DV), F32),
                   jax.ShapeDtypeStruct((nb, ML_HEADS, 1, ML_DK), F32),
                   jax.ShapeDtypeStruct((nb, ML_HEADS, 1, LANES), F32)],
        compiler_params=_params(1),
        name="mlstm_prompt",
    )(p, p, p, p, p, g_rows, b_row, b_col, norm_g)


STATE_BLOCK = 8


def _column(row_vec, eye):
    return jnp.sum(jnp.where(eye, row_vec, 0.0), axis=1, keepdims=True)


def _mlstm_sample_kernel(p_ref, gate_ref, bias_ref, g_ref, c_ref, n_ref, m_ref,
                         h_ref, c_out, n_out, m_out):
    eye = (lax.broadcasted_iota(jnp.int32, (ML_DK, ML_DK), 0)
           == lax.broadcasted_iota(jnp.int32, (ML_DK, ML_DK), 1))
    for s in range(p_ref.shape[0]):
        _mlstm_sample_one(s, eye, p_ref, gate_ref, bias_ref, g_ref, c_ref, n_ref, m_ref,
                          h_ref, c_out, n_out, m_out)


def _mlstm_sample_one(s, eye, p_ref, gate_ref, bias_ref, g_ref, c_ref, n_ref, m_ref,
                      h_ref, c_out, n_out, m_out):
    gates = gate_ref[s] + bias_ref[...]
    ig = gates[:, :LANES]
    lf = _log_sigmoid(gates[:, LANES:])
    m_old = m_ref[s]
    log_prev = lf + m_old
    m_t = jnp.maximum(log_prev, ig)
    w_i_all = jnp.exp(ig - m_t)
    w_p_all = jnp.exp(log_prev - m_t)
    floor_all = jnp.exp(-m_t)
    m_out[s] = m_t
    scale = ML_DK ** -0.5
    for h in range(ML_HEADS):
        q = p_ref[s, :, h * ML_DK:(h + 1) * ML_DK]
        k = p_ref[s, :, ML_QK + h * ML_DK:ML_QK + (h + 1) * ML_DK] * scale
        v = p_ref[s, :, 2 * ML_QK + h * ML_DV:2 * ML_QK + (h + 1) * ML_DV]
        og = p_ref[s, :, 2 * ML_QK + ML_VD + h * ML_DV:2 * ML_QK + ML_VD + (h + 1) * ML_DV]
        w_i = w_i_all[:, h:h + 1]
        w_p = w_p_all[:, h:h + 1]
        c_old = c_ref[0, s, h]
        n_old = n_ref[0, s, h:h + 1, :]
        q_col = _column(q, eye)
        k_col = _column(k, eye)
        a = jnp.sum(q * k, axis=1, keepdims=True) * w_i
        num = a * v + w_p * jnp.sum(c_old * q_col, axis=0, keepdims=True)
        den = a + w_p * jnp.sum(q * n_old, axis=1, keepdims=True)
        hh = num / jnp.maximum(jnp.abs(den), floor_all[:, h:h + 1])
        c_out[0, s, h] = w_p * c_old + (w_i * k_col) * v
        n_out[0, s, h:h + 1, :] = w_p * n_old + w_i * k
        sl = slice(h * ML_DV, (h + 1) * ML_DV)
        h_ref[s, :, sl] = _rms_gate(hh, g_ref[:, sl], og)


def _mlstm_sample_job(p, gates, b_gates, norm_g, c0, n0, m0, steps):
    n = p.shape[0]
    sb = n // steps
    zeros = jnp.zeros((LANES - ML_HEADS,), F32)
    b_row = jnp.concatenate([b_gates[:ML_HEADS], zeros, b_gates[ML_HEADS:], zeros])[None, :]
    m_pad = jnp.pad(m0[0], ((0, 0), (0, LANES - ML_HEADS)))[:, None, :]
    row3 = lambda i: (i, 0, 0)
    const2 = lambda i: (0, 0)
    c_spec = pl.BlockSpec((1, sb, ML_HEADS, ML_DK, ML_DV), lambda i: (0, i, 0, 0, 0))
    n_spec = pl.BlockSpec((1, sb, ML_HEADS, ML_DK), lambda i: (0, i, 0, 0))
    return SideJob(
        in_specs=(pl.BlockSpec((sb, 1, p.shape[1]), row3), pl.BlockSpec((sb, 1, 2 * LANES), row3),
                  pl.BlockSpec((1, 2 * LANES), const2), pl.BlockSpec((1, ML_VD), const2),
                  c_spec, n_spec, pl.BlockSpec((sb, 1, LANES), row3)),
        args=(p[:, None, :], gates[:, None, :], b_row, norm_g, c0, n0, m_pad),
        out_specs=(pl.BlockSpec((sb, 1, ML_VD), row3), c_spec, n_spec, pl.BlockSpec((sb, 1, LANES), row3)),
        out_shapes=(jax.ShapeDtypeStruct((n, 1, ML_VD), F32), jax.ShapeDtypeStruct(c0.shape, F32),
                    jax.ShapeDtypeStruct(n0.shape, F32), jax.ShapeDtypeStruct((n, 1, LANES), F32)),
        body=_mlstm_sample_kernel,
        aliases={})


HG_LEVELS = (1, 2, 4, 8, 16, 32, 64)


def _bcast_rows(x, group, which):
    n, w = x.shape
    x3 = x.reshape(n // group, group, w)
    return jnp.broadcast_to(x3[:, which:which + 1, :], x3.shape).reshape(n, w)


def _level_row_masks(level, t_idx):
    pos = t_idx & (level - 1)
    return dict(odd=((t_idx >> int(math.log2(level))) & 1) == 1,
                at_least=[pos >= i for i in range(level)],
                at_most=[pos <= i for i in range(level)])


def _level_log_decay(level, lf, b_incl, masks, shifted):
    if level >= 8:
        since = b_incl - _bcast_rows(b_incl - lf, level, 0)
        until = _bcast_rows(b_incl, level, level - 1) - b_incl
        return jnp.where(masks["odd"], since, until)
    since = lf
    until = jnp.zeros_like(lf)
    for i in range(1, level):
        since = since + jnp.where(masks["at_least"][i], shifted(i), 0.0)
        until = until + jnp.where(masks["at_most"][level - 1 - i], shifted(-i), 0.0)
    return jnp.where(masks["odd"], since, until)


def _hgrn_prompt_kernel(q_ref, f_ref, i_ref, g_ref, bf_ref, lb_ref, ng_ref, o_ref, s_ref, st_ref,
                        *, nb, chunk, layer_idx):
    step = pl.program_id(0)

    @pl.when(step == 0)
    def _():
        st_ref[...] = jnp.zeros(st_ref.shape, F32)

    logits = [lb_ref[i:i + 1, :] for i in range(DEPTH)]
    mx = functools.reduce(jnp.maximum, logits)
    es = [jnp.exp(l - mx) for l in logits]
    tot = functools.reduce(lambda a, c: a + c, es)
    cum = []
    for e in es:
        cum.append(e / tot if not cum else cum[-1] + e / tot)
    lb = cum[layer_idx] - cum[0]

    row = lax.broadcasted_iota(jnp.int32, (chunk, chunk), 0)
    col = lax.broadcasted_iota(jnp.int32, (chunk, chunk), 1)
    tril = jnp.where(col <= row, 1.0, 0.0).astype(BF16)
    eye = row == col
    level_masks = []
    for level in HG_LEVELS:
        sh = int(math.log2(level))
        u, w = row >> sh, col >> sh
        level_masks.append(((u & 1) * 4096 + (u - w)) == 4097)
    t_idx = lax.broadcasted_iota(jnp.int32, (chunk, HG_DK), 0)
    row_masks = [_level_row_masks(level, t_idx) for level in HG_LEVELS]
    last = chunk - 1

    for b in range(nb):
        fg_all = lb + (1.0 - lb) * _sigmoid(f_ref[b] + bf_ref[...])
        lf_all = jnp.log(fg_all)
        b_all = _dot01_left(tril, lf_all)
        for h in range(HG_HEADS):
            sl = slice(h * HG_DK, (h + 1) * HG_DK)
            qx = q_ref[b, :, sl]
            q = qx * _sigmoid(qx)
            k = 1.0 - fg_all[:, sl]
            lf = lf_all[:, sl]
            bi = b_all[:, sl]
            v = i_ref[b, :, sl]
            vb = v.astype(BF16)
            a = jnp.where(eye, _dot_nt(q.astype(BF16), k.astype(BF16)), 0.0)
            rolled = {}

            def shifted(i, lf=lf, rolled=rolled):
                if i not in rolled:
                    rolled[i] = pltpu.roll(lf, i % chunk, 0)
                return rolled[i]

            for level, mask, rows in zip(HG_LEVELS, level_masks, row_masks):
                e = jnp.exp(_level_log_decay(level, lf, bi, rows, shifted))
                a = a + jnp.where(mask, _dot_nt((q * e).astype(BF16), (k * e).astype(BF16)), 0.0)
            st = st_ref[b, h]
            o = _dot(a.astype(BF16), vb) + _dot_nt((q * jnp.exp(bi)).astype(BF16), st.astype(BF16))
            b_end = bi[last:last + 1, :]
            kd = k * jnp.exp(b_end - bi)
            st_ref[b, h] = st * jnp.exp(b_end) + _dot(v.T.astype(BF16), kd.astype(BF16))
            o_ref[b, :, sl] = _rms_gate(o, ng_ref[:, sl], g_ref[b, :, sl])

    @pl.when(step == pl.num_programs(0) - 1)
    def _():
        for b in range(nb):
            for h in range(HG_HEADS):
                s_ref[b, h] = st_ref[b, h].T


def _hgrn_prompt(p, b_f, lb_logits, norm_g, nb, t, layer_idx):
    chunk = HG_CHUNK
    kernel = functools.partial(_hgrn_prompt_kernel, nb=nb, chunk=chunk, layer_idx=layer_idx)
    const2 = lambda c: (0, 0)
    blk = lambda j: pl.BlockSpec((nb, chunk, HG_W), lambda c: (0, c, j))
    return pl.pallas_call(
        kernel,
        grid=(t // chunk,),
        in_specs=[blk(0), blk(1), blk(2), blk(3),
                  pl.BlockSpec((1, HG_W), const2), pl.BlockSpec((DEPTH, HG_W), const2),
                  pl.BlockSpec((1, HG_W), const2)],
        out_specs=[pl.BlockSpec((nb, chunk, HG_W), lambda c: (0, c, 0)),
                   pl.BlockSpec((nb, HG_HEADS, HG_DK, HG_DK), lambda c: (0, 0, 0, 0))],
        out_shape=[jax.ShapeDtypeStruct((nb, t, HG_W), F32),
                   jax.ShapeDtypeStruct((nb, HG_HEADS, HG_DK, HG_DK), F32)],
        scratch_shapes=[pltpu.VMEM((nb, HG_HEADS, HG_DK, HG_DK), F32)],
        compiler_params=_params(1),
        name="hgrn_prompt",
    )(p, p, p, p, b_f, lb_logits, norm_g)


def _hgrn_sample_kernel(p_ref, bf_ref, lb_ref, ng_ref, s_ref, o_ref, s_out, *, layer_idx):
    logits = [lb_ref[i:i + 1, :] for i in range(DEPTH)]
    mx = functools.reduce(jnp.maximum, logits)
    es = [jnp.exp(l - mx) for l in logits]
    tot = functools.reduce(lambda a, c: a + c, es)
    cum = []
    for e in es:
        cum.append(e / tot if not cum else cum[-1] + e / tot)
    lb = cum[layer_idx] - cum[0]
    eye = (lax.broadcasted_iota(jnp.int32, (HG_DK, HG_DK), 0)
           == lax.broadcasted_iota(jnp.int32, (HG_DK, HG_DK), 1))
    for s in range(p_ref.shape[0]):
        qx = p_ref[s, :, 0:HG_W]
        q_all = qx * _sigmoid(qx)
        fg_all = lb + (1.0 - lb) * _sigmoid(p_ref[s, :, HG_W:2 * HG_W] + bf_ref[...])
        for h in range(HG_HEADS):
            sl = slice(h * HG_DK, (h + 1) * HG_DK)
            q = q_all[:, sl]
            fg = fg_all[:, sl]
            k = 1.0 - fg
            v = p_ref[s, :, 2 * HG_W + h * HG_DK:2 * HG_W + (h + 1) * HG_DK]
            gate = p_ref[s, :, 3 * HG_W + h * HG_DK:3 * HG_W + (h + 1) * HG_DK]
            s_old = s_ref[0, s, h]
            decay = jnp.exp(jnp.log(fg))
            a = jnp.sum(q * k, axis=1, keepdims=True)
            o = a * v + jnp.sum(s_old * _column(q * decay, eye), axis=0, keepdims=True)
            s_out[0, s, h] = _column(decay, eye) * s_old + _column(k, eye) * v
            o_ref[s, :, sl] = _rms_gate(o, ng_ref[:, sl], gate)


def _hgrn_sample(p, b_f, lb_logits, norm_g, s0, layer_idx):
    n = p.shape[0]
    row3 = lambda i: (i, 0, 0)
    const2 = lambda i: (0, 0)
    sb = STATE_BLOCK
    state = pl.BlockSpec((1, sb, HG_HEADS, HG_DK, HG_DK), lambda i: (0, i, 0, 0, 0))
    o, s1 = pl.pallas_call(
        functools.partial(_hgrn_sample_kernel, layer_idx=layer_idx),
        grid=(n // sb,),
        in_specs=[pl.BlockSpec((sb, 1, p.shape[1]), row3), pl.BlockSpec((1, HG_W), const2),
                  pl.BlockSpec((DEPTH, HG_W), const2), pl.BlockSpec((1, HG_W), const2), state],
        out_specs=[pl.BlockSpec((sb, 1, HG_W), row3), state],
        out_shape=[jax.ShapeDtypeStruct((n, 1, HG_W), F32), jax.ShapeDtypeStruct(s0.shape, F32)],
        compiler_params=_params(1),
        name="hgrn_sample",
    )(p[:, None, :], b_f, lb_logits, norm_g, s0)
    return o[:, 0, :], s1


def _band_kernel(*refs, n_heads, group, fold, n_blocks, with_sinks, with_lse):
    refs = list(refs)
    sink_ref = refs.pop(0) if with_sinks else None
    q_ref, ko_ref, kp_ref, vo_ref, vp_ref, o_ref = refs[:6]
    lse_ref = refs[6] if with_lse else None
    step = pl.program_id(2)
    per = SPAN // fold
    shift = int(math.log2(per))

    def local_pos(idx):
        return ((idx & (per - 1)) * fold) + (idx >> shift)

    qi = lax.broadcasted_iota(jnp.int32, (SPAN, 2 * SPAN), 0)
    ki = lax.broadcasted_iota(jnp.int32, (SPAN, 2 * SPAN), 1)
    qpos = SPAN + local_pos(qi)
    kpos = (ki & SPAN) + local_pos(ki & (SPAN - 1))
    band = (kpos <= qpos) & (kpos >= qpos - SPAN)
    first = jnp.where(step > 0, 0, SPAN)
    band_first = band & (kpos >= first)
    lo_q = lax.broadcasted_iota(jnp.int32, (SPAN, LANES), 1) < HEAD_DIM
    lo_kv = lax.broadcasted_iota(jnp.int32, (2 * SPAN, LANES), 1) < HEAD_DIM
    scale = HEAD_DIM ** -0.5

    def block_rows(ref, blk):
        v = ref[:, blk * per:(blk + 1) * per, :]
        return v.reshape(SPAN, v.shape[-1])

    for blk in range(n_blocks):
        valid = band_first if blk == 0 else band
        k_prev = _rows(kp_ref) if blk == 0 else block_rows(ko_ref, blk - 1)
        v_prev = _rows(vp_ref) if blk == 0 else block_rows(vo_ref, blk - 1)
        kcat = jnp.concatenate([k_prev, block_rows(ko_ref, blk)], axis=0)
        vcat = jnp.concatenate([v_prev, block_rows(vo_ref, blk)], axis=0)
        q_blk = block_rows(q_ref, blk)
        tiles = {}

        def kv_tile(name, src, kv_head, want_hi):
            key = (name, kv_head, want_hi)
            if key not in tiles:
                t = src[:, (kv_head // 2) * LANES:(kv_head // 2 + 1) * LANES]
                if (kv_head % 2 == 1) != want_hi:
                    t = pltpu.roll(t, HEAD_DIM, 1)
                tiles[key] = t
            return tiles[key]

        def softmax_parts(s, head):
            s = jnp.where(valid, s, NEG)
            m = jnp.max(s, axis=1, keepdims=True)
            if with_sinks:
                m = jnp.maximum(m, sink_ref[head])
            p = jnp.exp(s - m)
            l = jnp.sum(p, axis=1, keepdims=True)
            if with_sinks:
                l = l + jnp.exp(sink_ref[head] - m)
            return p.astype(BF16), m, l

        for j in range(n_heads // 2):
            sl = slice(j * LANES, (j + 1) * LANES)
            qpair = q_blk[:, sl] * scale
            h_lo, h_hi = 2 * j, 2 * j + 1
            q_lo = jnp.where(lo_q, qpair, 0.0).astype(BF16)
            q_hi = jnp.where(lo_q, 0.0, qpair).astype(BF16)
            s_lo = _dot_nt(q_lo, kv_tile("k", kcat, h_lo // group, False).astype(BF16))
            s_hi = _dot_nt(q_hi, kv_tile("k", kcat, h_hi // group, True).astype(BF16))
            p_lo, m_lo, l_lo = softmax_parts(s_lo, h_lo)
            p_hi, m_hi, l_hi = softmax_parts(s_hi, h_hi)
            v_lo = jnp.where(lo_kv, kv_tile("v", vcat, h_lo // group, False), 0.0).astype(BF16)
            v_hi = jnp.where(lo_kv, 0.0, kv_tile("v", vcat, h_hi // group, True)).astype(BF16)
            o = (_dot(p_lo, v_lo) + _dot(p_hi, v_hi)) / jnp.where(lo_q, l_lo, l_hi)
            o_ref[:, blk * per:(blk + 1) * per, sl] = o.reshape(fold, per, LANES)
            if with_lse:
                lse = jnp.where(lo_q, m_lo + jnp.log(l_lo), m_hi + jnp.log(l_hi))
                lse_ref[:, blk * per:(blk + 1) * per, sl] = lse.reshape(fold, per, LANES)


BAND_BLOCKS = 4


def _band_call(name, arrays, sinks, fold, classes, rows, q_col, k_col, v_col, q_width, kv_width,
               n_heads, group, with_lse):
    seqs = arrays.shape[0]
    per = SPAN // fold
    steps = rows // (per * BAND_BLOCKS)
    own = lambda width, col: pl.BlockSpec((None, fold, None, per * BAND_BLOCKS, width),
                                          lambda b, c, n: (b, 0, c, n, col))
    prev = lambda width, col: pl.BlockSpec(
        (None, fold, None, per, width),
        lambda b, c, n: (b, 0, c, jnp.maximum(n * BAND_BLOCKS - 1, 0), col))
    out = pl.BlockSpec((None, fold, None, per * BAND_BLOCKS, q_width), lambda b, c, n: (b, 0, c, n, 0))
    shape = jax.ShapeDtypeStruct((seqs, fold, classes, rows, q_width), F32)
    in_specs = [own(q_width, q_col), own(kv_width, k_col), prev(kv_width, k_col),
                own(kv_width, v_col), prev(kv_width, v_col)]
    args = [arrays] * 5
    if sinks is not None:
        in_specs = [pl.BlockSpec(memory_space=pltpu.SMEM)] + in_specs
        args = [sinks] + args
    return pl.pallas_call(
        functools.partial(_band_kernel, n_heads=n_heads, group=group, fold=fold, n_blocks=BAND_BLOCKS,
                          with_sinks=sinks is not None, with_lse=with_lse),
        grid=(seqs, classes, steps),
        in_specs=in_specs,
        out_specs=[out, out] if with_lse else out,
        out_shape=[shape, shape] if with_lse else shape,
        compiler_params=_params(3),
        name=name,
    )(*args)


def _dil_prompt_group(p, g):
    _, dil = DIL_GROUPS[g]
    seqs, _, rows, width = p.shape
    fold = RESIDUES // dil
    w = DIL_HEADS * HEAD_DIM
    o, lse = _band_call("dil_prompt_g%d" % g, p.reshape(seqs, fold, dil, rows, width), None, fold, dil, rows,
                        3 * g, 3 * g + 1, 3 * g + 2, w, w, DIL_HEADS, 1, True)
    return o.reshape(seqs, RESIDUES, rows, w), lse.reshape(seqs, RESIDUES, rows, w)


def _swa_prompt(p, sinks, nb, t):
    kcol = SWA_Q // SWA_KV
    o = _band_call("swa_prompt", p.reshape(nb, 1, 1, t, p.shape[-1]), sinks, 1, 1, t,
                   0, kcol, kcol + 1, SWA_Q, SWA_KV, SWA_HEADS, SWA_HEADS // SWA_KV_HEADS, False)
    return o.reshape(nb * t, SWA_Q)


def _decode_roll_kernel(*refs, hb, with_sinks, **static):
    refs = list(refs)
    sink_ref = refs.pop(0) if with_sinks else None
    _decode_block(pl.program_id(1) * hb, sink_ref, *refs, hb=hb, with_sinks=with_sinks, **static)


def _decode_block(h0, sink_ref, q_ref, kn_ref, vn_ref, c_ref, o_ref, lse_ref, cout_ref,
                  *, nb, hb, group, length, dil, with_sinks):
    eye = (lax.broadcasted_iota(jnp.int32, (HEAD_DIM, HEAD_DIM), 0)
           == lax.broadcasted_iota(jnp.int32, (HEAD_DIM, HEAD_DIM), 1))
    pos = lax.broadcasted_iota(jnp.int32, (1, length), 1)
    attended = (pos & (dil - 1)) == 0
    is_last = lax.broadcasted_iota(jnp.int32, (HEAD_DIM, length), 1) == length - 1
    scale = HEAD_DIM ** -0.5
    for i in range(nb if group == 1 else 0):
        kt = c_ref[0, i, 0]
        vt = c_ref[0, i, 1]
        row = lambda ref: jnp.stack([ref[i, pl.ds(h0 + j, 1), :] for j in range(hb)])
        q, kn, vn = row(q_ref) * scale, row(kn_ref), row(vn_ref)
        column = lambda r: jnp.sum(jnp.where(eye[None], r, 0.0), axis=2, keepdims=True)
        s = jnp.sum(kt * column(q), axis=1, keepdims=True)
        s = jnp.where(attended[None], s, NEG)
        s_new = jnp.sum(q * kn, axis=2, keepdims=True)
        m = jnp.maximum(jnp.max(s, axis=2, keepdims=True), s_new)
        p = jnp.exp(s - m)
        p_new = jnp.exp(s_new - m)
        l = jnp.sum(p, axis=2, keepdims=True) + p_new
        acc_col = jnp.sum(vt * p, axis=2, keepdims=True)
        acc = jnp.sum(jnp.where(eye[None], acc_col, 0.0), axis=1, keepdims=True)
        out = (acc + p_new * vn) / l
        lse = jnp.broadcast_to(m + jnp.log(l), out.shape)
        for j in range(hb):
            o_ref[i, pl.ds(h0 + j, 1), :] = out[j]
            lse_ref[i, pl.ds(h0 + j, 1), :] = lse[j]
        cout_ref[0, i, 0] = jnp.where(is_last[None], column(kn), pltpu.roll(kt, length - 1, 2))
        cout_ref[0, i, 1] = jnp.where(is_last[None], column(vn), pltpu.roll(vt, length - 1, 2))
    for i in range(nb if group > 1 else 0):
        for j in range(hb):
            kt = c_ref[0, i, 0, j]
            vt = c_ref[0, i, 1, j]
            kn = kn_ref[i, pl.ds(h0 + j, 1), :]
            vn = vn_ref[i, pl.ds(h0 + j, 1), :]
            rows = pl.ds((h0 + j) * group, group)
            q = q_ref[i, rows, :] * scale
            s = _dot(q.astype(BF16), kt.astype(BF16))
            s = jnp.where(attended, s, NEG)
            s_new = jnp.sum(q * kn, axis=1, keepdims=True)
            m = jnp.maximum(jnp.max(s, axis=1, keepdims=True), s_new)
            if with_sinks:
                sink = sink_ref[rows, :]
                m = jnp.maximum(m, sink)
            p = jnp.exp(s - m)
            p_new = jnp.exp(s_new - m)
            l = jnp.sum(p, axis=1, keepdims=True) + p_new
            if with_sinks:
                l = l + jnp.exp(sink - m)
            acc = _dot_nt(p.astype(BF16), vt.astype(BF16))
            o_ref[i, rows, :] = (acc + p_new * vn) / l
            lse_ref[i, rows, :] = jnp.broadcast_to(m + jnp.log(l), (group, HEAD_DIM))
            cout_ref[0, i, 0, j] = jnp.where(is_last, _column(kn, eye), pltpu.roll(kt, length - 1, 1))
            cout_ref[0, i, 1, j] = jnp.where(is_last, _column(vn, eye), pltpu.roll(vt, length - 1, 1))


def _decode_job(q, k_new, v_new, cache_t, dil, nb, hb, steps, first=0, partial=None):
    _, _, _, kv_heads, d, length = cache_t.shape
    hsteps = kv_heads // hb
    seq_of = lambda i: (first + i) // hsteps
    cblock = pl.BlockSpec((1, nb, 2, hb, d, length), lambda i: (0, seq_of(i), 0, (first + i) % hsteps, 0, 0))
    per_seq = pl.BlockSpec((nb, kv_heads, d), lambda i: (seq_of(i), 0, 0))
    part = pl.BlockSpec((nb, kv_heads, d), lambda i: (i // hsteps, 0, 0))
    part_shape = jax.ShapeDtypeStruct((steps // hsteps * nb, kv_heads, d), F32)

    def body(q_ref, kn_ref, vn_ref, c_ref, *rest):
        o_ref, lse_ref, cout_ref = rest[-3:]
        block = first + pl.program_id(0)
        _decode_block((block % hsteps) * hb, None, q_ref, kn_ref, vn_ref, c_ref, o_ref, lse_ref, cout_ref,
                      nb=nb, hb=hb, group=1, length=length, dil=dil, with_sinks=False)

    in_specs, args, aliases = [per_seq, per_seq, per_seq, cblock], [q, k_new, v_new, cache_t], {}
    if partial is not None:
        in_specs.append(pl.BlockSpec(memory_space=pl.ANY))
        args.append(partial)
        aliases = {4: 2}
    return SideJob(tuple(in_specs), tuple(args), (part, part, cblock),
                   (part_shape, part_shape, jax.ShapeDtypeStruct(cache_t.shape, F32)), body, aliases)


def _decode_roll(q, k_new, v_new, cache, dil, nb, hb, sinks=None):
    _, n, length, _, kv_heads, d = cache.shape
    heads = q.shape[1]
    group = heads // kv_heads
    cache_t = jnp.transpose(cache, (0, 1, 3, 4, 5, 2))
    cblock = pl.BlockSpec((1, nb, 2, hb, d, length), lambda i, h: (0, i, 0, h, 0, 0))
    per_seq = lambda width: pl.BlockSpec((nb, width, d), lambda i, h: (i, 0, 0))
    in_specs = [per_seq(heads), per_seq(kv_heads), per_seq(kv_heads), cblock]
    args = [q, k_new, v_new, cache_t]
    if sinks is not None:
        in_specs = [pl.BlockSpec((heads, 1), lambda i, h: (0, 0))] + in_specs
        args = [sinks] + args
    o, lse, new_t = pl.pallas_call(
        functools.partial(_decode_roll_kernel, nb=nb, hb=hb, group=group, length=length, dil=dil,
                          with_sinks=sinks is not None),
        grid=(n // nb, kv_heads // hb),
        in_specs=in_specs,
        out_specs=[per_seq(heads), per_seq(heads), cblock],
        out_shape=[jax.ShapeDtypeStruct(q.shape, F32), jax.ShapeDtypeStruct(q.shape, F32),
                   jax.ShapeDtypeStruct(cache_t.shape, F32)],
        compiler_params=_params(2),
        name="decode_roll",
    )(*args)
    return o, lse, jnp.transpose(new_t, (0, 1, 5, 2, 3, 4))


PROMPT_TM = 512
LN_TM = 256
RIDE_HEADS = 4
RIDE_TM = 256
SWA_DECODE_BLOCK = (16, 2)


def _pad_cols(w, width):
    return jnp.pad(w, ((0, 0), (0, width - w.shape[1])))


def kernel(x_prompt, x_sample, state_mlstm_C, state_mlstm_n, state_mlstm_m, cache_dil_kv0, cache_dil_kv1, cache_dil_kv2, state_hgrn_S, cache_swa_kv, mlstm_w_in, mlstm_b_gates, mlstm_norm_g, mlstm_w_out, dil_w_in, dil_w_out, hgrn_w_in, hgrn_b_f, hgrn_lb_logits, hgrn_norm_g, hgrn_w_out, swa_w_in, swa_sinks, swa_w_out, ln1_g, ln1_b, ln2_g, ln2_b, mlp_w1, mlp_w2):
    nb, t, _ = x_prompt.shape
    ns = x_sample.shape[0]
    assert x_sample.shape[1] == 1 and DEPTH == 4
    hp = x_prompt.reshape(nb * t, D_MODEL)
    hs = x_sample.reshape(ns, D_MODEL)
    cos_p, sin_p = _rope_tables(jnp.arange(t, dtype=jnp.int32))
    cos_s, sin_s = _rope_tables(jnp.full((ns,), PAST_LEN, dtype=jnp.int32))

    ln = lambda i: (ln1_g[i][None], ln1_b[i][None], ln2_g[i][None], ln2_b[i][None])
    mlp = lambda i: (mlp_w1[i].astype(BF16), mlp_w2[i].astype(BF16))
    w_out = [w[0].astype(BF16) for w in (mlstm_w_out, dil_w_out, hgrn_w_out, swa_w_out)]

    w_in = mlstm_w_in[0]
    main = 2 * ML_QK + 2 * ML_VD
    w_mlstm = jnp.concatenate([w_in[:, :main], _pad_cols(w_in[:, main:main + ML_HEADS], LANES),
                               _pad_cols(w_in[:, main + ML_HEADS:], LANES)], axis=1).astype(BF16)
    w_dil = dil_w_in[0].astype(BF16)
    dil_qk = tuple((g * DIL_GW, g * DIL_GW + 2 * DIL_HEADS * HEAD_DIM) for g in range(len(DIL_GROUPS)))
    w_hgrn = hgrn_w_in[0].astype(BF16)
    w_swa = swa_w_in[0].astype(BF16)
    swa_qk = ((0, SWA_Q + SWA_KV),)
    norm_g = mlstm_norm_g[0][None]
    b_f, ng = hgrn_b_f[0][None], hgrn_norm_g[0][None]


    ps = _proj(hs, w_mlstm, ns)
    job = _mlstm_sample_job(ps[:, :main], ps[:, main:], mlstm_b_gates[0], norm_g,
                            state_mlstm_C, state_mlstm_n, state_mlstm_m, nb * t // PROMPT_TM)
    pp, mix_s, c_s, n_s, m_s = _proj(hp, w_mlstm, PROMPT_TM, sides=(job,))
    m_s = m_s[:, 0, :ML_HEADS][None]

    hs = _layer_tail(hs, *mlp(0), *ln(0)[2:], ns, mix_s[:, 0, :], w_out[0], *ln(0)[:2])
    qkv_s = _proj(hs, w_dil, ns, dil_qk, (cos_s, sin_s)).reshape(ns, 9, DIL_HEADS, HEAD_DIM)
    position_minor = lambda cache: jnp.transpose(cache, (0, 1, 3, 4, 5, 2))
    position_major = lambda cache_t: jnp.transpose(cache_t, (0, 1, 5, 2, 3, 4))

    def decode_job(g, cache, nb_, hb, steps, first=0, partial=None):
        win, dil = DIL_GROUPS[g]
        assert cache.shape[2] == win and win // dil == SPAN
        return _decode_job(qkv_s[:, 3 * g], qkv_s[:, 3 * g + 1], qkv_s[:, 3 * g + 2], position_minor(cache),
                           dil, nb_, hb, steps, first, partial)

    tail_steps = nb * t // LN_TM
    assert DEPTH * tail_steps == ns * (DIL_HEADS // RIDE_HEADS)
    big_o, big_lse = [], []

    def prompt_tail(layer, hp, mix, partial, shuffle=None):
        g1, b1, g2, b2 = ln(layer)
        with_mix = (mix, w_out[layer], g1, b1) if mix is not None else ()
        job = decode_job(len(DIL_GROUPS) - 1, cache_dil_kv2, 1, RIDE_HEADS, tail_steps,
                         layer * tail_steps, partial)
        hp, o, lse, partial = _layer_tail(hp, *mlp(layer), g2, b2, LN_TM, *with_mix, shuffle=shuffle,
                                          seq_len=t, sides=(job,))
        big_o.append(o)
        big_lse.append(lse)
        return hp, partial

    mix_p, c_p, n_p, m_p = _mlstm_prompt(pp.reshape(nb, t, -1), mlstm_b_gates[0], norm_g, nb, t)
    out_c_p = c_p[None]
    out_n_p = n_p[:, :, 0, :][None]
    out_m_p = m_p[:, :, 0, 0][None]
    hp, rolled = prompt_tail(0, hp, mix_p.reshape(nb * t, ML_VD), None, shuffle="to_residue")

    by_residue = lambda tab: jnp.transpose(tab.reshape(t // RESIDUES, RESIDUES, LANES), (1, 0, 2))
    small = [decode_job(g, cache, ns * RIDE_TM // (nb * t), DIL_HEADS, nb * t // RIDE_TM)
             for g, cache in enumerate((cache_dil_kv0, cache_dil_kv1))]
    pp, o0, lse0, new0, o1, lse1, new1 = _proj(hp, w_dil, RIDE_TM, dil_qk, (by_residue(cos_p), by_residue(sin_p)),
                                               by_residue=True, sides=small)
    outs, lses, dil_kv_p = [], [], []
    for g, (win, _) in enumerate(DIL_GROUPS):
        o, lse = _dil_prompt_group(pp, g)
        outs.append(o)
        lses.append(lse)
        kv = pp[:, :, (t - win) // RESIDUES:, g * DIL_GW + DIL_HEADS * HEAD_DIM:(g + 1) * DIL_GW]
        kv = jnp.transpose(kv, (0, 2, 1, 3))
        dil_kv_p.append(kv.reshape(nb, win, 2, DIL_HEADS, HEAD_DIM)[None])
    hp, rolled = prompt_tail(1, hp, (outs, lses), rolled, shuffle="to_natural")

    pp = _proj(hp, w_hgrn, PROMPT_TM)
    mix_p, s_p = _hgrn_prompt(pp.reshape(nb, t, 4 * HG_W), b_f, hgrn_lb_logits, ng, nb, t, 2)
    hp, rolled = prompt_tail(2, hp, mix_p.reshape(nb * t, HG_W), rolled)

    pp = _proj(hp, w_swa, PROMPT_TM, swa_qk, (cos_p, sin_p)).reshape(nb, t, -1)
    mix_p = _swa_prompt(pp, swa_sinks[0], nb, t)
    swa_kv_p = pp[:, t - SPAN:, SWA_Q:].reshape(nb, SPAN, 2, SWA_KV_HEADS, HEAD_DIM)[None]
    hp, rolled = prompt_tail(3, hp, mix_p, rolled)

    outs_s = [o0, o1, jnp.concatenate(big_o, axis=0)]
    lses_s = [lse0, lse1, jnp.concatenate(big_lse, axis=0)]
    dil_kv_s = [position_major(c) for c in (new0, new1, rolled)]
    flat = lambda a: a.reshape(ns, DIL_HEADS * HEAD_DIM)
    hs = _dil_out_ln([flat(o) for o in outs_s], [flat(l) for l in lses_s], w_out[1], hs, *ln(1)[:2], ns)
    hs = _layer_tail(hs, *mlp(1), *ln(1)[2:], ns)

    ps = _proj(hs, w_hgrn, ns)
    mix_s, s_s = _hgrn_sample(ps, b_f, hgrn_lb_logits, ng, state_hgrn_S, 2)
    hs = _layer_tail(hs, *mlp(2), *ln(2)[2:], ns, mix_s, w_out[2], *ln(2)[:2])

    ps = _proj(hs, w_swa, ns, swa_qk, (cos_s, sin_s))
    q_s = ps[:, :SWA_Q].reshape(ns, SWA_HEADS, HEAD_DIM)
    kv_s = ps[:, SWA_Q:].reshape(ns, 2, SWA_KV_HEADS, HEAD_DIM)
    assert cache_swa_kv.shape[2] == SPAN
    mix_s, _, swa_kv_s = _decode_roll(q_s, kv_s[:, 0], kv_s[:, 1], cache_swa_kv, 1, *SWA_DECODE_BLOCK,
                                      sinks=swa_sinks[0][:, None])
    hs = _layer_tail(hs, *mlp(3), *ln(3)[2:], ns, mix_s.reshape(ns, SWA_Q), w_out[3], *ln(3)[:2])

    return (hp.reshape(nb, t, D_MODEL), hs.reshape(ns, 1, D_MODEL),
            out_c_p, c_s, out_n_p, n_s, out_m_p, m_s,
            dil_kv_p[0], dil_kv_s[0], dil_kv_p[1], dil_kv_s[1], dil_kv_p[2], dil_kv_s[2],
            s_p[None], s_s, swa_kv_p, swa_kv_s)
```

```python
import functools
import math
from typing import Callable, NamedTuple

import numpy as np
import jax
import jax.numpy as jnp
from jax import lax
from jax.experimental import pallas as pl
from jax.experimental.pallas import tpu as pltpu

F32 = jnp.float32
BF16 = jnp.bfloat16

D_MODEL = 1024
DEPTH = 4
PAST_LEN = 8192
D_FF = 4 * D_MODEL
ALPHA = (2 * DEPTH) ** 0.25
LN_EPS = 1e-5
NORM_EPS = 1e-6
ROPE_THETA = 10000.0

ML_HEADS = 4
ML_DK = 128
ML_DV = 256
ML_QK = ML_HEADS * ML_DK
ML_VD = ML_HEADS * ML_DV
ML_CHUNK = 256

DIL_GROUPS = ((128, 1), (512, 4), (2048, 16))
DIL_HEADS = 8
HEAD_DIM = 64
DIL_GW = 3 * DIL_HEADS * HEAD_DIM
SPAN = 128

HG_HEADS = 8
HG_DK = 128
HG_CHUNK = 128
HG_W = HG_HEADS * HG_DK

SWA_HEADS = 16
SWA_KV_HEADS = 2
SWA_Q = SWA_HEADS * HEAD_DIM
SWA_KV = SWA_KV_HEADS * HEAD_DIM

LANES = 128
NEG = -1e30
VMEM_LIMIT = 48 * 1024 * 1024
HOST_VMEM_LIMIT = 58 * 1024 * 1024


def _params(n_axes, vmem=VMEM_LIMIT):
    return pltpu.CompilerParams(dimension_semantics=("arbitrary",) * n_axes, vmem_limit_bytes=vmem)


class SideJob(NamedTuple):
    in_specs: tuple
    args: tuple
    out_specs: tuple
    out_shapes: tuple
    body: Callable
    aliases: dict


def _host_kernel(*refs, own, n_in, n_out, sides):
    refs = list(refs)
    n_all_in = n_in + sum(job_in for job_in, _, _ in sides)
    n_all_out = n_out + sum(job_out for _, job_out, _ in sides)
    ins, outs, scratch = refs[:n_all_in], refs[n_all_in:n_all_in + n_all_out], refs[n_all_in + n_all_out:]
    at_in, at_out = n_in, n_out
    for job_in, job_out, body in sides:
        body(*ins[at_in:at_in + job_in], *outs[at_out:at_out + job_out])
        at_in += job_in
        at_out += job_out
    own(*ins[:n_in], *outs[:n_out], *scratch)


def _call(own, steps, in_specs, args, out_specs, out_shapes, name, scratch_shapes=(), sides=(),
          vmem=VMEM_LIMIT):
    all_in, all_args = list(in_specs), list(args)
    all_out, all_shapes = list(out_specs), list(out_shapes)
    aliases = {}
    for job in sides:
        for src, dst in job.aliases.items():
            aliases[len(all_args) + src] = len(all_out) + dst
        all_in += job.in_specs
        all_args += job.args
        all_out += job.out_specs
        all_shapes += job.out_shapes
    static = tuple((len(job.in_specs), len(job.out_specs), job.body) for job in sides)
    return pl.pallas_call(
        functools.partial(_host_kernel, own=own, n_in=len(in_specs), n_out=len(out_specs), sides=static),
        grid=(steps,),
        in_specs=all_in,
        out_specs=all_out,
        out_shape=all_shapes,
        input_output_aliases=aliases,
        scratch_shapes=list(scratch_shapes),
        compiler_params=_params(1, vmem),
        name=name,
    )(*all_args)


def _dot(a, b):
    return jnp.dot(a, b, preferred_element_type=F32)


def _dot_nt(a, b):
    return lax.dot_general(a, b, (((1,), (1,)), ((), ())), preferred_element_type=F32)


def _split3(x):
    hi = x.astype(BF16)
    r1 = x - hi.astype(F32)
    mid = r1.astype(BF16)
    lo = (r1 - mid.astype(F32)).astype(BF16)
    return hi, mid, lo


def _dot01_left(a01, x):
    hi, mid, lo = _split3(x)
    return _dot(a01, hi) + _dot(a01, mid) + _dot(a01, lo)


def _dot01_right(x, a01):
    hi, mid, lo = _split3(x)
    return _dot(hi, a01) + _dot(mid, a01) + _dot(lo, a01)


def _sigmoid(x):
    return 0.5 * jnp.tanh(0.5 * x) + 0.5


def _log_sigmoid(x):
    return jnp.minimum(x, 0.0) - jnp.log1p(jnp.exp(-jnp.abs(x)))


def _layer_norm(z, g, b):
    mu = jnp.mean(z, axis=-1, keepdims=True)
    zc = z - mu
    var = jnp.mean(zc * zc, axis=-1, keepdims=True)
    return zc * lax.rsqrt(var + LN_EPS) * g + b


def _rms_gate(h, g, gate):
    ms = jnp.mean(h * h, axis=-1, keepdims=True)
    return h * lax.rsqrt(ms + NORM_EPS) * g * _sigmoid(gate)


PROJ_CHUNK = 512
RESIDUES = 16


def _row_spec(width, tm, residue_tiles=None):
    if residue_tiles is None:
        return pl.BlockSpec((tm, width), lambda i: (i, 0))
    return pl.BlockSpec((None, RESIDUES, tm // RESIDUES, width),
                        lambda i: (i // residue_tiles, 0, i % residue_tiles, 0))


def _rows(ref):
    v = ref[...]
    return v.reshape(-1, v.shape[-1])


def _proj_kernel(*refs, plan):
    if any(rope for _, _, rope in plan):
        x_ref, w_ref, cos_ref, sin_ref, o_ref = refs
    else:
        x_ref, w_ref, o_ref = refs
    xb = _rows(x_ref).astype(BF16)
    for start, width, rope in plan:
        acc = _dot(xb, w_ref[:, start:start + width])
        if rope:
            reps = width // LANES
            cos = jnp.concatenate([_rows(cos_ref)] * reps, axis=1)
            sin = jnp.concatenate([_rows(sin_ref)] * reps, axis=1)
            lane = lax.broadcasted_iota(jnp.int32, acc.shape, 1)
            first_half = (lane & (HEAD_DIM - 1)) < HEAD_DIM // 2
            partner = jnp.where(first_half, pltpu.roll(acc, width - HEAD_DIM // 2, 1),
                                pltpu.roll(acc, HEAD_DIM // 2, 1))
            acc = acc * cos + partner * sin
        o_ref[..., start:start + width] = acc.reshape(o_ref.shape[:-1] + (width,))


def _proj(x, w_bf, tm, rope_cols=(), tables=None, by_residue=False, sides=()):
    k, n = w_bf.shape
    plan = []
    start = 0
    while start < n:
        rope = any(lo <= start < hi for lo, hi in rope_cols)
        limit = min([hi for lo, hi in rope_cols if lo <= start < hi] +
                    [lo for lo, hi in rope_cols if lo > start] + [n])
        width = min(PROJ_CHUNK, limit - start)
        plan.append((start, width, rope))
        start += width
    if by_residue:
        seqs, _, per_residue, _ = x.shape
        tiles = per_residue * RESIDUES // tm
        steps = seqs * tiles
        tab = pl.BlockSpec((RESIDUES, tm // RESIDUES, LANES), lambda i: (0, i % tiles, 0))
    else:
        tiles = None
        steps = x.shape[0] // tm
        if rope_cols:
            t_blocks = tables[0].shape[0] // tm
            tab = pl.BlockSpec((tm, LANES), lambda i: (i % t_blocks, 0))
    in_specs = [_row_spec(k, tm, tiles),
                pl.BlockSpec((k, n), lambda i: (0, 0), pipeline_mode=pl.Buffered(1))]
    args = [x, w_bf]
    if rope_cols:
        in_specs += [tab, tab]
        args += list(tables)
    results = _call(functools.partial(_proj_kernel, plan=tuple(plan)), steps, in_specs, args,
                    [_row_spec(n, tm, tiles)], [jax.ShapeDtypeStruct(x.shape[:-1] + (n,), F32)], "proj",
                    sides=sides, vmem=HOST_VMEM_LIMIT if sides else VMEM_LIMIT)
    return results if sides else results[0]


def _rope_tables(pos):
    half = HEAD_DIM // 2
    inv_freq = jnp.power(ROPE_THETA, -jnp.arange(half, dtype=F32) / half)
    ang = pos.astype(F32)[:, None] * inv_freq[None, :]
    cos = jnp.cos(ang)
    sin = jnp.sin(ang)
    cos = jnp.concatenate([cos, cos, cos, cos], axis=1)
    sin = jnp.concatenate([-sin, sin, -sin, sin], axis=1)
    return cos, sin


def _merge_groups(o0, o1, o2, l0, l1, l2):
    a0, a1, a2 = _rows(l0), _rows(l1), _rows(l2)
    mx = jnp.maximum(jnp.maximum(a0, a1), a2)
    e0, e1, e2 = jnp.exp(a0 - mx), jnp.exp(a1 - mx), jnp.exp(a2 - mx)
    return (e0 * _rows(o0) + e1 * _rows(o1) + e2 * _rows(o2)) / (e0 + e1 + e2)


def _dil_out_ln_kernel(o0, o1, o2, l0, l1, l2, w_ref, x_ref, g_ref, b_ref, y_ref):
    y = _merge_groups(o0, o1, o2, l0, l1, l2)
    z = ALPHA * _rows(x_ref) + _dot(y.astype(BF16), w_ref[...])
    y_ref[...] = _layer_norm(z, g_ref[...], b_ref[...]).reshape(y_ref.shape)


def _dil_out_ln(outs, lses, w_bf, x, g, b, tm):
    kin = outs[0].shape[-1]
    tiles = None
    steps = x.shape[0] // tm
    const = lambda i: (0, 0)
    return pl.pallas_call(
        _dil_out_ln_kernel,
        grid=(steps,),
        in_specs=[_row_spec(kin, tm, tiles)] * 6 + [
            pl.BlockSpec((kin, D_MODEL), const), _row_spec(D_MODEL, tm, tiles),
            pl.BlockSpec((1, D_MODEL), const), pl.BlockSpec((1, D_MODEL), const)],
        out_specs=_row_spec(D_MODEL, tm, tiles),
        out_shape=jax.ShapeDtypeStruct(x.shape, F32),
        compiler_params=_params(1),
        name="dil_out_ln",
    )(*outs, *lses, w_bf, x, g, b)


FF_CHUNK = 1024


def _tail_kernel(*refs, with_mix, shuffle):
    refs = list(refs)
    scratch = refs.pop() if shuffle else None
    y_ref = refs.pop()
    if with_mix:
        n_mix = 6 if with_mix == "groups" else 1
        mix_refs, (wo_ref, x_ref, g1_ref, b1_ref) = refs[:n_mix], refs[n_mix:n_mix + 4]
        refs = refs[n_mix + 4:]
        mix = _merge_groups(*mix_refs) if with_mix == "groups" else _rows(mix_refs[0])
        z = ALPHA * _rows(x_ref) + _dot(mix.astype(BF16), wo_ref[...])
        x = _layer_norm(z, g1_ref[...], b1_ref[...])
    else:
        x = _rows(refs.pop(0))
    w1_ref, w2_ref, g2_ref, b2_ref = refs
    xb = x.astype(BF16)
    acc = jnp.zeros(x.shape, F32)
    for c in range(D_FF // FF_CHUNK):
        a = _dot(xb, w1_ref[:, c * FF_CHUNK:(c + 1) * FF_CHUNK])
        a = jnp.square(jnp.maximum(a, 0.0)).astype(BF16)
        acc = acc + _dot(a, w2_ref[c * FF_CHUNK:(c + 1) * FF_CHUNK, :])
    y = _layer_norm(ALPHA * x + acc, g2_ref[...], b2_ref[...])
    per = y.shape[0] // RESIDUES
    lane_tiles = [slice(c * LANES, (c + 1) * LANES) for c in range(D_MODEL // LANES)]
    if shuffle == "to_residue":
        for c, cols in enumerate(lane_tiles):
            scratch[c] = y[:, cols]
        for r in range(RESIDUES):
            for c, cols in enumerate(lane_tiles):
                y_ref[r, :, cols] = scratch[c, pl.ds(r, per, stride=RESIDUES), :]
    elif shuffle == "to_natural":
        for r in range(RESIDUES):
            for c, cols in enumerate(lane_tiles):
                scratch[c, pl.ds(r, per, stride=RESIDUES), :] = y[r * per:(r + 1) * per, cols]
        for c, cols in enumerate(lane_tiles):
            y_ref[:, cols] = scratch[c]
    else:
        y_ref[...] = y


def _layer_tail(x, w1_bf, w2_bf, g2, b2, tm, mix=None, w_out_bf=None, g1=None, b1=None,
                shuffle=None, seq_len=None, sides=()):
    const = lambda i: (0, 0)
    resident = pl.Buffered(1)
    in_tiles = out_tiles = None
    out_shape = x.shape
    if shuffle == "to_natural":
        in_tiles = x.shape[2] * RESIDUES // tm
        steps = x.shape[0] * in_tiles
        out_shape = (x.shape[0] * x.shape[1] * x.shape[2], D_MODEL)
    else:
        steps = x.shape[0] // tm
        if shuffle == "to_residue":
            out_tiles = seq_len // tm
            out_shape = (x.shape[0] // seq_len, RESIDUES, seq_len // RESIDUES, D_MODEL)
    in_specs, args = [], []
    with_mix = False
    if mix is not None:
        grouped = isinstance(mix, tuple)
        mixes = list(mix[0]) + list(mix[1]) if grouped else [mix]
        with_mix = "groups" if grouped else True
        kin = mixes[0].shape[-1]
        in_specs += [_row_spec(kin, tm, in_tiles)] * len(mixes)
        in_specs.append(pl.BlockSpec((kin, D_MODEL), const, pipeline_mode=resident))
        args += mixes + [w_out_bf]
    in_specs.append(_row_spec(D_MODEL, tm, in_tiles))
    args.append(x)
    if mix is not None:
        in_specs += [pl.BlockSpec((1, D_MODEL), const)] * 2
        args += [g1, b1]
    in_specs += [pl.BlockSpec((D_MODEL, D_FF), const, pipeline_mode=resident),
                 pl.BlockSpec((D_FF, D_MODEL), const, pipeline_mode=resident),
                 pl.BlockSpec((1, D_MODEL), const), pl.BlockSpec((1, D_MODEL), const)]
    args += [w1_bf, w2_bf, g2, b2]
    scratch_shapes = [pltpu.VMEM((D_MODEL // LANES, tm, LANES), F32)] if shuffle else []
    results = _call(functools.partial(_tail_kernel, with_mix=with_mix, shuffle=shuffle), steps,
                    in_specs, args, [_row_spec(D_MODEL, tm, out_tiles)],
                    [jax.ShapeDtypeStruct(out_shape, F32)], "layer_tail", scratch_shapes, sides,
                    HOST_VMEM_LIMIT if sides else VMEM_LIMIT)
    return results if sides else results[0]


def _mlstm_prompt_kernel(q_ref, k_ref, v_ref, og_ref, gc_ref, gr_ref, brow_ref, bcol_ref, g_ref,
                         h_ref, c_ref, n_ref, m_ref, *, nb, chunk):
    step = pl.program_id(0)

    @pl.when(step == 0)
    def _():
        c_ref[...] = jnp.zeros(c_ref.shape, F32)
        n_ref[...] = jnp.zeros(n_ref.shape, F32)
        m_ref[...] = jnp.zeros(m_ref.shape, F32)

    row = lax.broadcasted_iota(jnp.int32, (chunk, chunk), 0)
    col = lax.broadcasted_iota(jnp.int32, (chunk, chunk), 1)
    causal = col <= row
    tril = jnp.where(causal, 1.0, 0.0).astype(BF16)
    triu = jnp.where(row <= col, 1.0, 0.0).astype(BF16)
    scale = ML_DK ** -0.5
    last = chunk - 1

    for b in range(nb):
        gc = gc_ref[b] + brow_ref[...]
        ig_cols = gc[:, :LANES]
        b_cols = _dot01_left(tril, _log_sigmoid(gc[:, LANES:]))
        gr = gr_ref[b] + bcol_ref[...]
        b_rows = _dot01_right(_log_sigmoid(gr), triu)
        for h in range(ML_HEADS):
            bc = b_cols[:, h:h + 1]
            igc = ig_cols[:, h:h + 1]
            br = b_rows[ML_HEADS + h:ML_HEADS + h + 1, :]
            igr = gr[h:h + 1, :]
            m_prev = m_ref[b, h][:, 0:1]
            log_intra = jnp.where(causal, bc - br + igr, NEG)
            log_prev = bc + m_prev
            m_t = jnp.maximum(log_prev, jnp.max(log_intra, axis=1, keepdims=True))
            w_intra = jnp.exp(log_intra - m_t)
            w_prev = jnp.exp(log_prev - m_t)
            q = q_ref[b, :, h * ML_DK:(h + 1) * ML_DK]
            k = k_ref[b, :, h * ML_DK:(h + 1) * ML_DK] * scale
            v = v_ref[b, :, h * ML_DV:(h + 1) * ML_DV]
            qb, kb, vb = q.astype(BF16), k.astype(BF16), v.astype(BF16)
            a = _dot_nt(qb, kb) * w_intra
            c_old = c_ref[b, h]
            n_old = n_ref[b, h]
            num = _dot(a.astype(BF16), vb) + w_prev * _dot(qb, c_old.astype(BF16))
            den = jnp.sum(a, axis=1, keepdims=True) + w_prev * jnp.sum(q * n_old, axis=1, keepdims=True)
            hh = num / jnp.maximum(jnp.abs(den), jnp.exp(-m_t))
            m_new = m_t[last:last + 1, :]
            b_end = bc[last:last + 1, :]
            w_end = jnp.exp(b_end - bc + igc - m_new)
            decay = jnp.exp(b_end + m_prev - m_new)
            kw = k * w_end
            c_ref[b, h] = decay * c_old + _dot(kw.T.astype(BF16), vb)
            n_ref[b, h] = decay * n_old + jnp.sum(kw, axis=0, keepdims=True)
            m_ref[b, h] = jnp.broadcast_to(m_new, (1, LANES))
            sl = slice(h * ML_DV, (h + 1) * ML_DV)
            h_ref[b, :, sl] = _rms_gate(hh, g_ref[:, sl], og_ref[b, :, sl])


def _mlstm_prompt(p, b_gates, norm_g, nb, t):
    chunk = ML_CHUNK
    main = 2 * ML_QK + 2 * ML_VD
    g8 = jnp.concatenate([p[..., main:main + ML_HEADS], p[..., main + LANES:main + LANES + ML_HEADS]], axis=-1)
    g_rows = jnp.transpose(g8, (0, 2, 1))
    zeros = jnp.zeros((LANES - ML_HEADS,), F32)
    b_row = jnp.concatenate([b_gates[:ML_HEADS], zeros, b_gates[ML_HEADS:], zeros])[None, :]
    b_col = b_gates[:, None]
    kernel = functools.partial(_mlstm_prompt_kernel, nb=nb, chunk=chunk)
    const2 = lambda c: (0, 0)
    return pl.pallas_call(
        kernel,
        grid=(t // chunk,),
        in_specs=[pl.BlockSpec((nb, chunk, ML_QK), lambda c: (0, c, 0)),
                  pl.BlockSpec((nb, chunk, ML_QK), lambda c: (0, c, 1)),
                  pl.BlockSpec((nb, chunk, ML_VD), lambda c: (0, c, 1)),
                  pl.BlockSpec((nb, chunk, ML_VD), lambda c: (0, c, 2)),
                  pl.BlockSpec((nb, chunk, 2 * LANES), lambda c: (0, c, main // (2 * LANES))),
                  pl.BlockSpec((nb, 2 * ML_HEADS, chunk), lambda c: (0, 0, c)),
                  pl.BlockSpec((1, 2 * LANES), const2),
                  pl.BlockSpec((2 * ML_HEADS, 1), const2),
                  pl.BlockSpec((1, ML_VD), const2)],
        out_specs=[pl.BlockSpec((nb, chunk, ML_VD), lambda c: (0, c, 0)),
                   pl.BlockSpec((nb, ML_HEADS, ML_DK, ML_DV), lambda c: (0, 0, 0, 0)),
                   pl.BlockSpec((nb, ML_HEADS, 1, ML_DK), lambda c: (0, 0, 0, 0)),
                   pl.BlockSpec((nb, ML_HEADS, 1, LANES), lambda c: (0, 0, 0, 0))],
        out_shape=[jax.ShapeDtypeStruct((nb, t, ML_VD), F32),
                   jax.ShapeDtypeStruct((nb, ML_HEADS, ML_DK, ML_DV), F32),
                   jax.ShapeDtypeStruct((nb, ML_HEADS, 1, ML_DK), F32),
                   jax.ShapeDtypeStruct((nb, ML_HEADS, 1, LANES), F32)],
        compiler_params=_params(1),
        name="mlstm_prompt",
    )(p, p, p, p, p, g_rows, b_row, b_col, norm_g)


STATE_BLOCK = 8


def _column(row_vec, eye):
    return jnp.sum(jnp.where(eye, row_vec, 0.0), axis=1, keepdims=True)


def _mlstm_sample_kernel(p_ref, gate_ref, bias_ref, g_ref, c_ref, n_ref, m_ref,
                         h_ref, c_out, n_out, m_out):
    eye = (lax.broadcasted_iota(jnp.int32, (ML_DK, ML_DK), 0)
           == lax.broadcasted_iota(jnp.int32, (ML_DK, ML_DK), 1))
    for s in range(p_ref.shape[0]):
        _mlstm_sample_one(s, eye, p_ref, gate_ref, bias_ref, g_ref, c_ref, n_ref, m_ref,
                          h_ref, c_out, n_out, m_out)


def _mlstm_sample_one(s, eye, p_ref, gate_ref, bias_ref, g_ref, c_ref, n_ref, m_ref,
                      h_ref, c_out, n_out, m_out):
    gates = gate_ref[s] + bias_ref[...]
    ig = gates[:, :LANES]
    lf = _log_sigmoid(gates[:, LANES:])
    m_old = m_ref[s]
    log_prev = lf + m_old
    m_t = jnp.maximum(log_prev, ig)
    w_i_all = jnp.exp(ig - m_t)
    w_p_all = jnp.exp(log_prev - m_t)
    floor_all = jnp.exp(-m_t)
    m_out[s] = m_t
    scale = ML_DK ** -0.5
    for h in range(ML_HEADS):
        q = p_ref[s, :, h * ML_DK:(h + 1) * ML_DK]
        k = p_ref[s, :, ML_QK + h * ML_DK:ML_QK + (h + 1) * ML_DK] * scale
        v = p_ref[s, :, 2 * ML_QK + h * ML_DV:2 * ML_QK + (h + 1) * ML_DV]
        og = p_ref[s, :, 2 * ML_QK + ML_VD + h * ML_DV:2 * ML_QK + ML_VD + (h + 1) * ML_DV]
        w_i = w_i_all[:, h:h + 1]
        w_p = w_p_all[:, h:h + 1]
        c_old = c_ref[0, s, h]
        n_old = n_ref[0, s, h:h + 1, :]
        q_col = _column(q, eye)
        k_col = _column(k, eye)
        a = jnp.sum(q * k, axis=1, keepdims=True) * w_i
        num = a * v + w_p * jnp.sum(c_old * q_col, axis=0, keepdims=True)
        den = a + w_p * jnp.sum(q * n_old, axis=1, keepdims=True)
        hh = num / jnp.maximum(jnp.abs(den), floor_all[:, h:h + 1])
        c_out[0, s, h] = w_p * c_old + (w_i * k_col) * v
        n_out[0, s, h:h + 1, :] = w_p * n_old + w_i * k
        sl = slice(h * ML_DV, (h + 1) * ML_DV)
        h_ref[s, :, sl] = _rms_gate(hh, g_ref[:, sl], og)


def _mlstm_sample_job(p, gates, b_gates, norm_g, c0, n0, m0, steps):
    n = p.shape[0]
    sb = n // steps
    zeros = jnp.zeros((LANES - ML_HEADS,), F32)
    b_row = jnp.concatenate([b_gates[:ML_HEADS], zeros, b_gates[ML_HEADS:], zeros])[None, :]
    m_pad = jnp.pad(m0[0], ((0, 0), (0, LANES - ML_HEADS)))[:, None, :]
    row3 = lambda i: (i, 0, 0)
    const2 = lambda i: (0, 0)
    c_spec = pl.BlockSpec((1, sb, ML_HEADS, ML_DK, ML_DV), lambda i: (0, i, 0, 0, 0))
    n_spec = pl.BlockSpec((1, sb, ML_HEADS, ML_DK), lambda i: (0, i, 0, 0))
    return SideJob(
        in_specs=(pl.BlockSpec((sb, 1, p.shape[1]), row3), pl.BlockSpec((sb, 1, 2 * LANES), row3),
                  pl.BlockSpec((1, 2 * LANES), const2), pl.BlockSpec((1, ML_VD), const2),
                  c_spec, n_spec, pl.BlockSpec((sb, 1, LANES), row3)),
        args=(p[:, None, :], gates[:, None, :], b_row, norm_g, c0, n0, m_pad),
        out_specs=(pl.BlockSpec((sb, 1, ML_VD), row3), c_spec, n_spec, pl.BlockSpec((sb, 1, LANES), row3)),
        out_shapes=(jax.ShapeDtypeStruct((n, 1, ML_VD), F32), jax.ShapeDtypeStruct(c0.shape, F32),
                    jax.ShapeDtypeStruct(n0.shape, F32), jax.ShapeDtypeStruct((n, 1, LANES), F32)),
        body=_mlstm_sample_kernel,
        aliases={})


HG_LEVELS = (1, 2, 4, 8, 16, 32, 64)


def _bcast_rows(x, group, which):
    n, w = x.shape
    x3 = x.reshape(n // group, group, w)
    return jnp.broadcast_to(x3[:, which:which + 1, :], x3.shape).reshape(n, w)


def _level_row_masks(level, t_idx):
    pos = t_idx & (level - 1)
    return dict(odd=((t_idx >> int(math.log2(level))) & 1) == 1,
                at_least=[pos >= i for i in range(level)],
                at_most=[pos <= i for i in range(level)])


def _level_log_decay(level, lf, b_incl, masks, shifted):
    if level >= 8:
        since = b_incl - _bcast_rows(b_incl - lf, level, 0)
        until = _bcast_rows(b_incl, level, level - 1) - b_incl
        return jnp.where(masks["odd"], since, until)
    since = lf
    until = jnp.zeros_like(lf)
    for i in range(1, level):
        since = since + jnp.where(masks["at_least"][i], shifted(i), 0.0)
        until = until + jnp.where(masks["at_most"][level - 1 - i], shifted(-i), 0.0)
    return jnp.where(masks["odd"], since, until)


def _hgrn_prompt_kernel(q_ref, f_ref, i_ref, g_ref, bf_ref, lb_ref, ng_ref, o_ref, s_ref, st_ref,
                        *, nb, chunk, layer_idx):
    step = pl.program_id(0)

    @pl.when(step == 0)
    def _():
        st_ref[...] = jnp.zeros(st_ref.shape, F32)

    logits = [lb_ref[i:i + 1, :] for i in range(DEPTH)]
    mx = functools.reduce(jnp.maximum, logits)
    es = [jnp.exp(l - mx) for l in logits]
    tot = functools.reduce(lambda a, c: a + c, es)
    cum = []
    for e in es:
        cum.append(e / tot if not cum else cum[-1] + e / tot)
    lb = cum[layer_idx] - cum[0]

    row = lax.broadcasted_iota(jnp.int32, (chunk, chunk), 0)
    col = lax.broadcasted_iota(jnp.int32, (chunk, chunk), 1)
    tril = jnp.where(col <= row, 1.0, 0.0).astype(BF16)
    eye = row == col
    level_masks = []
    for level in HG_LEVELS:
        sh = int(math.log2(level))
        u, w = row >> sh, col >> sh
        level_masks.append(((u & 1) * 4096 + (u - w)) == 4097)
    t_idx = lax.broadcasted_iota(jnp.int32, (chunk, HG_DK), 0)
    row_masks = [_level_row_masks(level, t_idx) for level in HG_LEVELS]
    last = chunk - 1

    for b in range(nb):
        fg_all = lb + (1.0 - lb) * _sigmoid(f_ref[b] + bf_ref[...])
        lf_all = jnp.log(fg_all)
        b_all = _dot01_left(tril, lf_all)
        for h in range(HG_HEADS):
            sl = slice(h * HG_DK, (h + 1) * HG_DK)
            qx = q_ref[b, :, sl]
            q = qx * _sigmoid(qx)
            k = 1.0 - fg_all[:, sl]
            lf = lf_all[:, sl]
            bi = b_all[:, sl]
            v = i_ref[b, :, sl]
            vb = v.astype(BF16)
            a = jnp.where(eye, _dot_nt(q.astype(BF16), k.astype(BF16)), 0.0)
            rolled = {}

            def shifted(i, lf=lf, rolled=rolled):
                if i not in rolled:
                    rolled[i] = pltpu.roll(lf, i % chunk, 0)
                return rolled[i]

            for level, mask, rows in zip(HG_LEVELS, level_masks, row_masks):
                e = jnp.exp(_level_log_decay(level, lf, bi, rows, shifted))
                a = a + jnp.where(mask, _dot_nt((q * e).astype(BF16), (k * e).astype(BF16)), 0.0)
            st = st_ref[b, h]
            o = _dot(a.astype(BF16), vb) + _dot_nt((q * jnp.exp(bi)).astype(BF16), st.astype(BF16))
            b_end = bi[last:last + 1, :]
            kd = k * jnp.exp(b_end - bi)
            st_ref[b, h] = st * jnp.exp(b_end) + _dot(v.T.astype(BF16), kd.astype(BF16))
            o_ref[b, :, sl] = _rms_gate(o, ng_ref[:, sl], g_ref[b, :, sl])

    @pl.when(step == pl.num_programs(0) - 1)
    def _():
        for b in range(nb):
            for h in range(HG_HEADS):
                s_ref[b, h] = st_ref[b, h].T


def _hgrn_prompt(p, b_f, lb_logits, norm_g, nb, t, layer_idx):
    chunk = HG_CHUNK
    kernel = functools.partial(_hgrn_prompt_kernel, nb=nb, chunk=chunk, layer_idx=layer_idx)
    const2 = lambda c: (0, 0)
    blk = lambda j: pl.BlockSpec((nb, chunk, HG_W), lambda c: (0, c, j))
    return pl.pallas_call(
        kernel,
        grid=(t // chunk,),
        in_specs=[blk(0), blk(1), blk(2), blk(3),
                  pl.BlockSpec((1, HG_W), const2), pl.BlockSpec((DEPTH, HG_W), const2),
                  pl.BlockSpec((1, HG_W), const2)],
        out_specs=[pl.BlockSpec((nb, chunk, HG_W), lambda c: (0, c, 0)),
                   pl.BlockSpec((nb, HG_HEADS, HG_DK, HG_DK), lambda c: (0, 0, 0, 0))],
        out_shape=[jax.ShapeDtypeStruct((nb, t, HG_W), F32),
                   jax.ShapeDtypeStruct((nb, HG_HEADS, HG_DK, HG_DK), F32)],
        scratch_shapes=[pltpu.VMEM((nb, HG_HEADS, HG_DK, HG_DK), F32)],
        compiler_params=_params(1),
        name="hgrn_prompt",
    )(p, p, p, p, b_f, lb_logits, norm_g)


def _hgrn_sample_kernel(p_ref, bf_ref, lb_ref, ng_ref, s_ref, o_ref, s_out, *, layer_idx):
    logits = [lb_ref[i:i + 1, :] for i in range(DEPTH)]
    mx = functools.reduce(jnp.maximum, logits)
    es = [jnp.exp(l - mx) for l in logits]
    tot = functools.reduce(lambda a, c: a + c, es)
    cum = []
    for e in es:
        cum.append(e / tot if not cum else cum[-1] + e / tot)
    lb = cum[layer_idx] - cum[0]
    eye = (lax.broadcasted_iota(jnp.int32, (HG_DK, HG_DK), 0)
           == lax.broadcasted_iota(jnp.int32, (HG_DK, HG_DK), 1))
    for s in range(p_ref.shape[0]):
        qx = p_ref[s, :, 0:HG_W]
        q_all = qx * _sigmoid(qx)
        fg_all = lb + (1.0 - lb) * _sigmoid(p_ref[s, :, HG_W:2 * HG_W] + bf_ref[...])
        for h in range(HG_HEADS):
            sl = slice(h * HG_DK, (h + 1) * HG_DK)
            q = q_all[:, sl]
            fg = fg_all[:, sl]
            k = 1.0 - fg
            v = p_ref[s, :, 2 * HG_W + h * HG_DK:2 * HG_W + (h + 1) * HG_DK]
            gate = p_ref[s, :, 3 * HG_W + h * HG_DK:3 * HG_W + (h + 1) * HG_DK]
            s_old = s_ref[0, s, h]
            decay = jnp.exp(jnp.log(fg))
            a = jnp.sum(q * k, axis=1, keepdims=True)
            o = a * v + jnp.sum(s_old * _column(q * decay, eye), axis=0, keepdims=True)
            s_out[0, s, h] = _column(decay, eye) * s_old + _column(k, eye) * v
            o_ref[s, :, sl] = _rms_gate(o, ng_ref[:, sl], gate)


def _hgrn_sample(p, b_f, lb_logits, norm_g, s0, layer_idx):
    n = p.shape[0]
    row3 = lambda i: (i, 0, 0)
    const2 = lambda i: (0, 0)
    sb = STATE_BLOCK
    state = pl.BlockSpec((1, sb, HG_HEADS, HG_DK, HG_DK), lambda i: (0, i, 0, 0, 0))
    o, s1 = pl.pallas_call(
        functools.partial(_hgrn_sample_kernel, layer_idx=layer_idx),
        grid=(n // sb,),
        in_specs=[pl.BlockSpec((sb, 1, p.shape[1]), row3), pl.BlockSpec((1, HG_W), const2),
                  pl.BlockSpec((DEPTH, HG_W), const2), pl.BlockSpec((1, HG_W), const2), state],
        out_specs=[pl.BlockSpec((sb, 1, HG_W), row3), state],
        out_shape=[jax.ShapeDtypeStruct((n, 1, HG_W), F32), jax.ShapeDtypeStruct(s0.shape, F32)],
        compiler_params=_params(1),
        name="hgrn_sample",
    )(p[:, None, :], b_f, lb_logits, norm_g, s0)
    return o[:, 0, :], s1


def _band_kernel(*refs, n_heads, group, fold, n_blocks, with_sinks, with_lse):
    refs = list(refs)
    sink_ref = refs.pop(0) if with_sinks else None
    q_ref, ko_ref, kp_ref, vo_ref, vp_ref, o_ref = refs[:6]
    lse_ref = refs[6] if with_lse else None
    step = pl.program_id(2)
    per = SPAN // fold
    shift = int(math.log2(per))

    def local_pos(idx):
        return ((idx & (per - 1)) * fold) + (idx >> shift)

    qi = lax.broadcasted_iota(jnp.int32, (SPAN, 2 * SPAN), 0)
    ki = lax.broadcasted_iota(jnp.int32, (SPAN, 2 * SPAN), 1)
    qpos = SPAN + local_pos(qi)
    kpos = (ki & SPAN) + local_pos(ki & (SPAN - 1))
    band = (kpos <= qpos) & (kpos >= qpos - SPAN)
    first = jnp.where(step > 0, 0, SPAN)
    band_first = band & (kpos >= first)
    lo_q = lax.broadcasted_iota(jnp.int32, (SPAN, LANES), 1) < HEAD_DIM
    lo_kv = lax.broadcasted_iota(jnp.int32, (2 * SPAN, LANES), 1) < HEAD_DIM
    scale = HEAD_DIM ** -0.5

    def block_rows(ref, blk):
        v = ref[:, blk * per:(blk + 1) * per, :]
        return v.reshape(SPAN, v.shape[-1])

    for blk in range(n_blocks):
        valid = band_first if blk == 0 else band
        k_prev = _rows(kp_ref) if blk == 0 else block_rows(ko_ref, blk - 1)
        v_prev = _rows(vp_ref) if blk == 0 else block_rows(vo_ref, blk - 1)
        kcat = jnp.concatenate([k_prev, block_rows(ko_ref, blk)], axis=0)
        vcat = jnp.concatenate([v_prev, block_rows(vo_ref, blk)], axis=0)
        q_blk = block_rows(q_ref, blk)
        tiles = {}

        def kv_tile(name, src, kv_head, want_hi):
            key = (name, kv_head, want_hi)
            if key not in tiles:
                t = src[:, (kv_head // 2) * LANES:(kv_head // 2 + 1) * LANES]
                if (kv_head % 2 == 1) != want_hi:
                    t = pltpu.roll(t, HEAD_DIM, 1)
                tiles[key] = t
            return tiles[key]

        def softmax_parts(s, head):
            s = jnp.where(valid, s, NEG)
            m = jnp.max(s, axis=1, keepdims=True)
            if with_sinks:
                m = jnp.maximum(m, sink_ref[head])
            p = jnp.exp(s - m)
            l = jnp.sum(p, axis=1, keepdims=True)
            if with_sinks:
                l = l + jnp.exp(sink_ref[head] - m)
            return p.astype(BF16), m, l

        for j in range(n_heads // 2):
            sl = slice(j * LANES, (j + 1) * LANES)
            qpair = q_blk[:, sl] * scale
            h_lo, h_hi = 2 * j, 2 * j + 1
            q_lo = jnp.where(lo_q, qpair, 0.0).astype(BF16)
            q_hi = jnp.where(lo_q, 0.0, qpair).astype(BF16)
            s_lo = _dot_nt(q_lo, kv_tile("k", kcat, h_lo // group, False).astype(BF16))
            s_hi = _dot_nt(q_hi, kv_tile("k", kcat, h_hi // group, True).astype(BF16))
            p_lo, m_lo, l_lo = softmax_parts(s_lo, h_lo)
            p_hi, m_hi, l_hi = softmax_parts(s_hi, h_hi)
            v_lo = jnp.where(lo_kv, kv_tile("v", vcat, h_lo // group, False), 0.0).astype(BF16)
            v_hi = jnp.where(lo_kv, 0.0, kv_tile("v", vcat, h_hi // group, True)).astype(BF16)
            o = (_dot(p_lo, v_lo) + _dot(p_hi, v_hi)) / jnp.where(lo_q, l_lo, l_hi)
            o_ref[:, blk * per:(blk + 1) * per, sl] = o.reshape(fold, per, LANES)
            if with_lse:
                lse = jnp.where(lo_q, m_lo + jnp.log(l_lo), m_hi + jnp.log(l_hi))
                lse_ref[:, blk * per:(blk + 1) * per, sl] = lse.reshape(fold, per, LANES)


BAND_BLOCKS = 4


def _band_call(name, arrays, sinks, fold, classes, rows, q_col, k_col, v_col, q_width, kv_width,
               n_heads, group, with_lse):
    seqs = arrays.shape[0]
    per = SPAN // fold
    steps = rows // (per * BAND_BLOCKS)
    own = lambda width, col: pl.BlockSpec((None, fold, None, per * BAND_BLOCKS, width),
                                          lambda b, c, n: (b, 0, c, n, col))
    prev = lambda width, col: pl.BlockSpec(
        (None, fold, None, per, width),
        lambda b, c, n: (b, 0, c, jnp.maximum(n * BAND_BLOCKS - 1, 0), col))
    out = pl.BlockSpec((None, fold, None, per * BAND_BLOCKS, q_width), lambda b, c, n: (b, 0, c, n, 0))
    shape = jax.ShapeDtypeStruct((seqs, fold, classes, rows, q_width), F32)
    in_specs = [own(q_width, q_col), own(kv_width, k_col), prev(kv_width, k_col),
                own(kv_width, v_col), prev(kv_width, v_col)]
    args = [arrays] * 5
    if sinks is not None:
        in_specs = [pl.BlockSpec(memory_space=pltpu.SMEM)] + in_specs
        args = [sinks] + args
    return pl.pallas_call(
        functools.partial(_band_kernel, n_heads=n_heads, group=group, fold=fold, n_blocks=BAND_BLOCKS,
                          with_sinks=sinks is not None, with_lse=with_lse),
        grid=(seqs, classes, steps),
        in_specs=in_specs,
        out_specs=[out, out] if with_lse else out,
        out_shape=[shape, shape] if with_lse else shape,
        compiler_params=_params(3),
        name=name,
    )(*args)


def _dil_prompt_group(p, g):
    _, dil = DIL_GROUPS[g]
    seqs, _, rows, width = p.shape
    fold = RESIDUES // dil
    w = DIL_HEADS * HEAD_DIM
    o, lse = _band_call("dil_prompt_g%d" % g, p.reshape(seqs, fold, dil, rows, width), None, fold, dil, rows,
                        3 * g, 3 * g + 1, 3 * g + 2, w, w, DIL_HEADS, 1, True)
    return o.reshape(seqs, RESIDUES, rows, w), lse.reshape(seqs, RESIDUES, rows, w)


def _swa_prompt(p, sinks, nb, t):
    kcol = SWA_Q // SWA_KV
    o = _band_call("swa_prompt", p.reshape(nb, 1, 1, t, p.shape[-1]), sinks, 1, 1, t,
                   0, kcol, kcol + 1, SWA_Q, SWA_KV, SWA_HEADS, SWA_HEADS // SWA_KV_HEADS, False)
    return o.reshape(nb * t, SWA_Q)


def _decode_roll_kernel(*refs, hb, with_sinks, **static):
    refs = list(refs)
    sink_ref = refs.pop(0) if with_sinks else None
    _decode_block(pl.program_id(1) * hb, sink_ref, *refs, hb=hb, with_sinks=with_sinks, **static)


def _decode_block(h0, sink_ref, q_ref, kn_ref, vn_ref, c_ref, o_ref, lse_ref, cout_ref,
                  *, nb, hb, group, length, dil, with_sinks):
    eye = (lax.broadcasted_iota(jnp.int32, (HEAD_DIM, HEAD_DIM), 0)
           == lax.broadcasted_iota(jnp.int32, (HEAD_DIM, HEAD_DIM), 1))
    pos = lax.broadcasted_iota(jnp.int32, (1, length), 1)
    attended = (pos & (dil - 1)) == 0
    is_last = lax.broadcasted_iota(jnp.int32, (HEAD_DIM, length), 1) == length - 1
    scale = HEAD_DIM ** -0.5
    for i in range(nb if group == 1 else 0):
        kt = c_ref[0, i, 0]
        vt = c_ref[0, i, 1]
        row = lambda ref: jnp.stack([ref[i, pl.ds(h0 + j, 1), :] for j in range(hb)])
        q, kn, vn = row(q_ref) * scale, row(kn_ref), row(vn_ref)
        column = lambda r: jnp.sum(jnp.where(eye[None], r, 0.0), axis=2, keepdims=True)
        s = jnp.sum(kt * column(q), axis=1, keepdims=True)
        s = jnp.where(attended[None], s, NEG)
        s_new = jnp.sum(q * kn, axis=2, keepdims=True)
        m = jnp.maximum(jnp.max(s, axis=2, keepdims=True), s_new)
        p = jnp.exp(s - m)
        p_new = jnp.exp(s_new - m)
        l = jnp.sum(p, axis=2, keepdims=True) + p_new
        acc_col = jnp.sum(vt * p, axis=2, keepdims=True)
        acc = jnp.sum(jnp.where(eye[None], acc_col, 0.0), axis=1, keepdims=True)
        out = (acc + p_new * vn) / l
        lse = jnp.broadcast_to(m + jnp.log(l), out.shape)
        for j in range(hb):
            o_ref[i, pl.ds(h0 + j, 1), :] = out[j]
            lse_ref[i, pl.ds(h0 + j, 1), :] = lse[j]
        cout_ref[0, i, 0] = jnp.where(is_last[None], column(kn), pltpu.roll(kt, length - 1, 2))
        cout_ref[0, i, 1] = jnp.where(is_last[None], column(vn), pltpu.roll(vt, length - 1, 2))
    for i in range(nb if group > 1 else 0):
        for j in range(hb):
            kt = c_ref[0, i, 0, j]
            vt = c_ref[0, i, 1, j]
            kn = kn_ref[i, pl.ds(h0 + j, 1), :]
            vn = vn_ref[i, pl.ds(h0 + j, 1), :]
            rows = pl.ds((h0 + j) * group, group)
            q = q_ref[i, rows, :] * scale
            s = _dot(q.astype(BF16), kt.astype(BF16))
            s = jnp.where(attended, s, NEG)
            s_new = jnp.sum(q * kn, axis=1, keepdims=True)
            m = jnp.maximum(jnp.max(s, axis=1, keepdims=True), s_new)
            if with_sinks:
                sink = sink_ref[rows, :]
                m = jnp.maximum(m, sink)
            p = jnp.exp(s - m)
            p_new = jnp.exp(s_new - m)
            l = jnp.sum(p, axis=1, keepdims=True) + p_new
            if with_sinks:
                l = l + jnp.exp(sink - m)
            acc = _dot_nt(p.astype(BF16), vt.astype(BF16))
            o_ref[i, rows, :] = (acc + p_new * vn) / l
            lse_ref[i, rows, :] = jnp.broadcast_to(m + jnp.log(l), (group, HEAD_DIM))
            cout_ref[0, i, 0, j] = jnp.where(is_last, _column(kn, eye), pltpu.roll(kt, length - 1, 1))
            cout_ref[0, i, 1, j] = jnp.where(is_last, _column(vn, eye), pltpu.roll(vt, length - 1, 1))


def _decode_job(q, k_new, v_new, cache_t, dil, nb, hb, steps, first=0, partial=None):
    _, _, _, kv_heads, d, length = cache_t.shape
    hsteps = kv_heads // hb
    seq_of = lambda i: (first + i) // hsteps
    cblock = pl.BlockSpec((1, nb, 2, hb, d, length), lambda i: (0, seq_of(i), 0, (first + i) % hsteps, 0, 0))
    per_seq = pl.BlockSpec((nb, kv_heads, d), lambda i: (seq_of(i), 0, 0))
    part = pl.BlockSpec((nb, kv_heads, d), lambda i: (i // hsteps, 0, 0))
    part_shape = jax.ShapeDtypeStruct((steps // hsteps * nb, kv_heads, d), F32)

    def body(q_ref, kn_ref, vn_ref, c_ref, *rest):
        o_ref, lse_ref, cout_ref = rest[-3:]
        block = first + pl.program_id(0)
        _decode_block((block % hsteps) * hb, None, q_ref, kn_ref, vn_ref, c_ref, o_ref, lse_ref, cout_ref,
                      nb=nb, hb=hb, group=1, length=length, dil=dil, with_sinks=False)

    in_specs, args, aliases = [per_seq, per_seq, per_seq, cblock], [q, k_new, v_new, cache_t], {}
    if partial is not None:
        in_specs.append(pl.BlockSpec(memory_space=pl.ANY))
        args.append(partial)
        aliases = {4: 2}
    return SideJob(tuple(in_specs), tuple(args), (part, part, cblock),
                   (part_shape, part_shape, jax.ShapeDtypeStruct(cache_t.shape, F32)), body, aliases)


def _decode_roll(q, k_new, v_new, cache, dil, nb, hb, sinks=None):
    _, n, length, _, kv_heads, d = cache.shape
    heads = q.shape[1]
    group = heads // kv_heads
    cache_t = jnp.transpose(cache, (0, 1, 3, 4, 5, 2))
    cblock = pl.BlockSpec((1, nb, 2, hb, d, length), lambda i, h: (0, i, 0, h, 0, 0))
    per_seq = lambda width: pl.BlockSpec((nb, width, d), lambda i, h: (i, 0, 0))
    in_specs = [per_seq(heads), per_seq(kv_heads), per_seq(kv_heads), cblock]
    args = [q, k_new, v_new, cache_t]
    if sinks is not None:
        in_specs = [pl.BlockSpec((heads, 1), lambda i, h: (0, 0))] + in_specs
        args = [sinks] + args
    o, lse, new_t = pl.pallas_call(
        functools.partial(_decode_roll_kernel, nb=nb, hb=hb, group=group, length=length, dil=dil,
                          with_sinks=sinks is not None),
        grid=(n // nb, kv_heads // hb),
        in_specs=in_specs,
        out_specs=[per_seq(heads), per_seq(heads), cblock],
        out_shape=[jax.ShapeDtypeStruct(q.shape, F32), jax.ShapeDtypeStruct(q.shape, F32),
                   jax.ShapeDtypeStruct(cache_t.shape, F32)],
        compiler_params=_params(2),
        name="decode_roll",
    )(*args)
    return o, lse, jnp.transpose(new_t, (0, 1, 5, 2, 3, 4))


PROMPT_TM = 512
LN_TM = 256
RIDE_HEADS = 4
RIDE_TM = 256
SWA_DECODE_BLOCK = (16, 2)


def _pad_cols(w, width):
    return jnp.pad(w, ((0, 0), (0, width - w.shape[1])))


def kernel(x_prompt, x_sample, state_mlstm_C, state_mlstm_n, state_mlstm_m, cache_dil_kv0, cache_dil_kv1, cache_dil_kv2, state_hgrn_S, cache_swa_kv, mlstm_w_in, mlstm_b_gates, mlstm_norm_g, mlstm_w_out, dil_w_in, dil_w_out, hgrn_w_in, hgrn_b_f, hgrn_lb_logits, hgrn_norm_g, hgrn_w_out, swa_w_in, swa_sinks, swa_w_out, ln1_g, ln1_b, ln2_g, ln2_b, mlp_w1, mlp_w2):
    nb, t, _ = x_prompt.shape
    ns = x_sample.shape[0]
    assert x_sample.shape[1] == 1 and DEPTH == 4
    hp = x_prompt.reshape(nb * t, D_MODEL)
    hs = x_sample.reshape(ns, D_MODEL)
    cos_p, sin_p = _rope_tables(jnp.arange(t, dtype=jnp.int32))
    cos_s, sin_s = _rope_tables(jnp.full((ns,), PAST_LEN, dtype=jnp.int32))

    ln = lambda i: (ln1_g[i][None], ln1_b[i][None], ln2_g[i][None], ln2_b[i][None])
    mlp = lambda i: (mlp_w1[i].astype(BF16), mlp_w2[i].astype(BF16))
    w_out = [w[0].astype(BF16) for w in (mlstm_w_out, dil_w_out, hgrn_w_out, swa_w_out)]

    w_in = mlstm_w_in[0]
    main = 2 * ML_QK + 2 * ML_VD
    w_mlstm = jnp.concatenate([w_in[:, :main], _pad_cols(w_in[:, main:main + ML_HEADS], LANES),
                               _pad_cols(w_in[:, main + ML_HEADS:], LANES)], axis=1).astype(BF16)
    w_dil = dil_w_in[0].astype(BF16)
    dil_qk = tuple((g * DIL_GW, g * DIL_GW + 2 * DIL_HEADS * HEAD_DIM) for g in range(len(DIL_GROUPS)))
    w_hgrn = hgrn_w_in[0].astype(BF16)
    w_swa = swa_w_in[0].astype(BF16)
    swa_qk = ((0, SWA_Q + SWA_KV),)
    norm_g = mlstm_norm_g[0][None]
    b_f, ng = hgrn_b_f[0][None], hgrn_norm_g[0][None]


    ps = _proj(hs, w_mlstm, ns)
    job = _mlstm_sample_job(ps[:, :main], ps[:, main:], mlstm_b_gates[0], norm_g,
                            state_mlstm_C, state_mlstm_n, state_mlstm_m, nb * t // PROMPT_TM)
    pp, mix_s, c_s, n_s, m_s = _proj(hp, w_mlstm, PROMPT_TM, sides=(job,))
    m_s = m_s[:, 0, :ML_HEADS][None]

    hs = _layer_tail(hs, *mlp(0), *ln(0)[2:], ns, mix_s[:, 0, :], w_out[0], *ln(0)[:2])
    qkv_s = _proj(hs, w_dil, ns, dil_qk, (cos_s, sin_s)).reshape(ns, 9, DIL_HEADS, HEAD_DIM)
    position_minor = lambda cache: jnp.transpose(cache, (0, 1, 3, 4, 5, 2))
    position_major = lambda cache_t: jnp.transpose(cache_t, (0, 1, 5, 2, 3, 4))

    def decode_job(g, cache, nb_, hb, steps, first=0, partial=None):
        win, dil = DIL_GROUPS[g]
        assert cache.shape[2] == win and win // dil == SPAN
        return _decode_job(qkv_s[:, 3 * g], qkv_s[:, 3 * g + 1], qkv_s[:, 3 * g + 2], position_minor(cache),
                           dil, nb_, hb, steps, first, partial)

    tail_steps = nb * t // LN_TM
    assert DEPTH * tail_steps == ns * (DIL_HEADS // RIDE_HEADS)
    big_o, big_lse = [], []

    def prompt_tail(layer, hp, mix, partial, shuffle=None):
        g1, b1, g2, b2 = ln(layer)
        with_mix = (mix, w_out[layer], g1, b1) if mix is not None else ()
        job = decode_job(len(DIL_GROUPS) - 1, cache_dil_kv2, 1, RIDE_HEADS, tail_steps,
                         layer * tail_steps, partial)
        hp, o, lse, partial = _layer_tail(hp, *mlp(layer), g2, b2, LN_TM, *with_mix, shuffle=shuffle,
                                          seq_len=t, sides=(job,))
        big_o.append(o)
        big_lse.append(lse)
        return hp, partial

    mix_p, c_p, n_p, m_p = _mlstm_prompt(pp.reshape(nb, t, -1), mlstm_b_gates[0], norm_g, nb, t)
    out_c_p = c_p[None]
    out_n_p = n_p[:, :, 0, :][None]
    out_m_p = m_p[:, :, 0, 0][None]
    hp, rolled = prompt_tail(0, hp, mix_p.reshape(nb * t, ML_VD), None, shuffle="to_residue")

    by_residue = lambda tab: jnp.transpose(tab.reshape(t // RESIDUES, RESIDUES, LANES), (1, 0, 2))
    job = decode_job(0, cache_dil_kv0, ns * RIDE_TM // (nb * t), DIL_HEADS, nb * t // RIDE_TM)
    pp, o0, lse0, new0 = _proj(hp, w_dil, RIDE_TM, dil_qk, (by_residue(cos_p), by_residue(sin_p)),
                               by_residue=True, sides=(job,))
    outs, lses, dil_kv_p = [], [], []
    for g, (win, _) in enumerate(DIL_GROUPS):
        o, lse = _dil_prompt_group(pp, g)
        outs.append(o)
        lses.append(lse)
        kv = pp[:, :, (t - win) // RESIDUES:, g * DIL_GW + DIL_HEADS * HEAD_DIM:(g + 1) * DIL_GW]
        kv = jnp.transpose(kv, (0, 2, 1, 3))
        dil_kv_p.append(kv.reshape(nb, win, 2, DIL_HEADS, HEAD_DIM)[None])
    hp, rolled = prompt_tail(1, hp, (outs, lses), rolled, shuffle="to_natural")

    proj_steps = nb * t // PROMPT_TM
    mid_seqs = ns // (2 * proj_steps)
    job = decode_job(1, cache_dil_kv1, mid_seqs, DIL_HEADS, proj_steps)
    pp, o1a, lse1a, new1 = _proj(hp, w_hgrn, PROMPT_TM, sides=(job,))
    mix_p, s_p = _hgrn_prompt(pp.reshape(nb, t, 4 * HG_W), b_f, hgrn_lb_logits, ng, nb, t, 2)
    hp, rolled = prompt_tail(2, hp, mix_p.reshape(nb * t, HG_W), rolled)

    job = decode_job(1, cache_dil_kv1, mid_seqs, DIL_HEADS, proj_steps, proj_steps, new1)
    pp, o1b, lse1b, new1 = _proj(hp, w_swa, PROMPT_TM, swa_qk, (cos_p, sin_p), sides=(job,))
    pp = pp.reshape(nb, t, -1)
    mix_p = _swa_prompt(pp, swa_sinks[0], nb, t)
    swa_kv_p = pp[:, t - SPAN:, SWA_Q:].reshape(nb, SPAN, 2, SWA_KV_HEADS, HEAD_DIM)[None]
    hp, rolled = prompt_tail(3, hp, mix_p, rolled)

    outs_s = [o0, jnp.concatenate([o1a, o1b], axis=0), jnp.concatenate(big_o, axis=0)]
    lses_s = [lse0, jnp.concatenate([lse1a, lse1b], axis=0), jnp.concatenate(big_lse, axis=0)]
    dil_kv_s = [position_major(c) for c in (new0, new1, rolled)]
    flat = lambda a: a.reshape(ns, DIL_HEADS * HEAD_DIM)
    hs = _dil_out_ln([flat(o) for o in outs_s], [flat(l) for l in lses_s], w_out[1], hs, *ln(1)[:2], ns)
    hs = _layer_tail(hs, *mlp(1), *ln(1)[2:], ns)

    ps = _proj(hs, w_hgrn, ns)
    mix_s, s_s = _hgrn_sample(ps, b_f, hgrn_lb_logits, ng, state_hgrn_S, 2)
    hs = _layer_tail(hs, *mlp(2), *ln(2)[2:], ns, mix_s, w_out[2], *ln(2)[:2])

    ps = _proj(hs, w_swa, ns, swa_qk, (cos_s, sin_s))
    q_s = ps[:, :SWA_Q].reshape(ns, SWA_HEADS, HEAD_DIM)
    kv_s = ps[:, SWA_Q:].reshape(ns, 2, SWA_KV_HEADS, HEAD_DIM)
    assert cache_swa_kv.shape[2] == SPAN
    mix_s, _, swa_kv_s = _decode_roll(q_s, kv_s[:, 0], kv_s[:, 1], cache_swa_kv, 1, *SWA_DECODE_BLOCK,
                                      sinks=swa_sinks[0][:, None])
    hs = _layer_tail(hs, *mlp(3), *ln(3)[2:], ns, mix_s.reshape(ns, SWA_Q), w_out[3], *ln(3)[:2])

    return (hp.reshape(nb, t, D_MODEL), hs.reshape(ns, 1, D_MODEL),
            out_c_p, c_s, out_n_p, n_s, out_m_p, m_s,
            dil_kv_p[0], dil_kv_s[0], dil_kv_p[1], dil_kv_s[1], dil_kv_p[2], dil_kv_s[2],
            s_p[None], s_s, swa_kv_p, swa_kv_s)
```
